```python
import math, functools
import jax, jax.numpy as jnp
from jax import lax
import numpy as np

D_MODEL = 1024
BATCH = 4
SEQ = 4096
DEPTH = 1
DEC_BATCH = 128
DEC_SEQ = 1
PAST_LEN = 2048
PAGE_SIZE = 128

A_HEADS = 8
A_DH = 64
A_DV = 2 * A_DH
B_HEADS = 8
B_DK = 128
B_DV = 128
D_FF = 2816
CONV_W = 3
ROPE_THETA = 10000.0
Q_BLOCK = 128
B_CHUNK = 64
EPS = 1e-6

A_QK_W = A_HEADS * 2 * A_DH
A_V_W = A_HEADS * A_DV
B_K_W = B_HEADS * B_DK
B_V_W = B_HEADS * B_DV
IN_SIZES = (A_QK_W, A_QK_W, A_V_W, B_K_W, B_K_W, B_V_W, B_V_W, D_MODEL, D_MODEL)
IN_W = sum(IN_SIZES)

kernel_name = "hybrid_diffattn_hgrn2_convffn_step"

F32 = jnp.float32


def rmsnorm(x, g):
    xf = x.astype(F32)
    xf = xf * lax.rsqrt(jnp.mean(xf * xf, axis=-1, keepdims=True) + EPS)
    return xf.astype(x.dtype) * g


def rope(x, pos):
    half = A_DH // 2
    inv = ROPE_THETA ** (-jnp.arange(half, dtype=F32) / half)
    ang = pos.astype(F32)[:, None] * inv[None, :]
    cos = jnp.cos(ang)[:, None, None, :]
    sin = jnp.sin(ang)[:, None, None, :]
    xf = x.astype(F32)
    x1, x2 = xf[..., :half], xf[..., half:]
    return jnp.concatenate([x1 * cos - x2 * sin, x2 * cos + x1 * sin], axis=-1).astype(x.dtype)


def split_in(z):
    offs = [int(o) for o in np.cumsum(IN_SIZES)[:-1]]
    return jnp.split(z, offs, axis=-1)


def diff_probs(s, lam):
    p = jax.nn.softmax(s, axis=-1)
    return p[:, :, 0] - lam * p[:, :, 1]


def attend_prompt(q, k, v, lam):
    b, t = q.shape[:2]
    nblk = t // Q_BLOCK
    scale = A_DH ** -0.5
    qb = q.reshape(b, nblk, Q_BLOCK, A_HEADS, 2, A_DH).swapaxes(0, 1)
    kpos = jnp.arange(t)

    def blk(args):
        qi, i = args
        qpos = i * Q_BLOCK + jnp.arange(Q_BLOCK)
        s = jnp.einsum('bqhcd,bkhcd->bhcqk', qi, k).astype(F32) * scale
        s = jnp.where(kpos[None, :] <= qpos[:, None], s, -jnp.inf)
        p = diff_probs(s, lam)
        return jnp.einsum('bhqk,bkhv->bqhv', p.astype(v.dtype), v)

    o = lax.map(blk, (qb, jnp.arange(nblk)))
    return o.swapaxes(0, 1).reshape(b, t, A_HEADS, A_DV)


def attend_sample(q, k, v, lam, k_past, v_past):
    tn = q.shape[1]
    p_len = k_past.shape[1]
    scale = A_DH ** -0.5
    s_past = jnp.einsum('bqhcd,bkhcd->bhcqk', q, k_past).astype(F32) * scale
    s_new = jnp.einsum('bqhcd,bkhcd->bhcqk', q, k).astype(F32) * scale
    causal = jnp.tril(jnp.ones((tn, tn), bool))
    s_new = jnp.where(causal, s_new, -jnp.inf)
    p = diff_probs(jnp.concatenate([s_past, s_new], axis=-1), lam).astype(v.dtype)
    return (jnp.einsum('bhqk,bkhv->bqhv', p[..., :p_len], v_past)
            + jnp.einsum('bhqk,bkhv->bqhv', p[..., p_len:], v))


def hgrn_chunked(q, k, g, v, s0):
    b, t = q.shape[:2]
    n = t // B_CHUNK
    tri = jnp.tril(jnp.ones((B_CHUNK, B_CHUNK), bool))[None, :, :, None, None]

    def to_chunks(a):
        return a.reshape(b, n, B_CHUNK, *a.shape[2:]).swapaxes(0, 1)

    def step(S, inp):
        qc, kc, gc, vc = inp
        cum = jnp.cumsum(gc, axis=1)
        rel = cum[:, :, None] - cum[:, None, :]
        decay = jnp.exp(jnp.where(tri, rel, -jnp.inf))
        A = jnp.einsum('bthk,btshk,bshk->bhts', qc, decay, kc)
        o = (jnp.einsum('bhts,bshv->bthv', A, vc)
             + jnp.einsum('bthk,bhkv->bthv', qc * jnp.exp(cum), S))
        last = cum[:, -1]
        S_new = (jnp.exp(last)[..., None] * S
                 + jnp.einsum('bshk,bshv->bhkv', kc * jnp.exp(last[:, None] - cum), vc))
        return S_new, o

    S, o = lax.scan(step, s0, (to_chunks(q), to_chunks(k), to_chunks(g), to_chunks(v)))
    return o.swapaxes(0, 1).reshape(b, t, B_HEADS, B_DV), S


def hgrn_steps(q, k, g, v, s0):
    def step(S, inp):
        qt, kt, gt, vt = inp
        S = jnp.exp(gt)[..., None] * S + kt[..., None] * vt[..., None, :]
        return S, jnp.einsum('bhk,bhkv->bhv', qt, S)

    S, o = lax.scan(step, s0, (q.swapaxes(0, 1), k.swapaxes(0, 1), g.swapaxes(0, 1), v.swapaxes(0, 1)))
    return o.swapaxes(0, 1), S


def token_mixer(xn, pos, p, attend, recur):
    b, t = xn.shape[:2]
    qa, ka, va, qb, fb, ib, ogb, ga, gb = split_in(xn @ p['w_in'])
    qa = rope(rmsnorm(qa.reshape(b, t, A_HEADS, 2, A_DH), p['q_norm_g']), pos)
    ka = rope(rmsnorm(ka.reshape(b, t, A_HEADS, 2, A_DH), p['k_norm_g']), pos)
    va = va.reshape(b, t, A_HEADS, A_DV)
    oa = attend(qa, ka, va, p['lam'])
    oa = rmsnorm(oa, p['subln_g']) * (1.0 - p['lam_init'])
    qh = jax.nn.silu(qb.astype(F32)).reshape(b, t, B_HEADS, B_DK)
    lb = p['lb']
    f = lb + (1.0 - lb) * jax.nn.sigmoid(fb.astype(F32))
    kh = (1.0 - f).reshape(b, t, B_HEADS, B_DK)
    gh = jnp.log(f).reshape(b, t, B_HEADS, B_DK)
    vh = ib.astype(F32).reshape(b, t, B_HEADS, B_DV)
    ob, s_new = recur(qh, kh, gh, vh)
    ob = rmsnorm(ob, p['hgrn_norm_g']) * jax.nn.silu(ogb.astype(F32).reshape(b, t, B_HEADS, B_DV))
    ob = ob.astype(xn.dtype)
    ya = oa.reshape(b, t, A_V_W) @ p['w_branch_a']
    yb = ob.reshape(b, t, B_V_W) @ p['w_branch_b']
    merged = jax.nn.sigmoid(ga) * ya + jax.nn.sigmoid(gb) * yb
    return merged @ p['w_out'], ka, va, s_new


def conv_ffn(hn, buf, p):
    t = hn.shape[1]
    u = hn @ p['w_up']
    full = jnp.concatenate([buf.astype(u.dtype), u], axis=1)
    y = p['conv_b']
    for j in range(CONV_W):
        y = y + p['conv_w'][j] * full[:, j:j + t]
    a, g = jnp.split(y, 2, axis=-1)
    return (jax.nn.silu(a) * g) @ p['w_down'], full[:, -(CONV_W - 1):]


def layer_step(x, c, pos, p, attend, recur, conv_buf):
    mod = jax.nn.silu(c) @ p['w_ada'] + p['b_ada']
    sh_m, sc_m, g_m, sh_f, sc_f, g_f = [m[:, None, :] for m in jnp.split(mod, 6, axis=-1)]
    xn = rmsnorm(x, p['rms_mix_g']) * (1.0 + sc_m) + sh_m
    mix, k_rows, v_rows, s_new = token_mixer(xn, pos, p, attend, recur)
    x = x + g_m * mix
    hn = rmsnorm(x, p['rms_ffn_g']) * (1.0 + sc_f) + sh_f
    ffn, buf_new = conv_ffn(hn, conv_buf, p)
    x = x + g_f * ffn
    return x, k_rows, v_rows, s_new, buf_new


def setup_inputs(seed: int = 0) -> dict:
    key = jax.random.key(seed)
    keys = list(jax.random.split(key, 40))

    def nrm(shape, s):
        return s * jax.random.normal(keys.pop(), shape, F32)

    n_pages = PAST_LEN // PAGE_SIZE
    n_used = DEC_BATCH * n_pages
    n_pool = n_used + (n_used + 3) // 4
    page_table = jax.random.permutation(keys.pop(), n_pool)[:n_used].reshape(DEC_BATCH, n_pages).astype(jnp.int32)
    d = D_MODEL
    return {
        'x_prompt': nrm((BATCH, SEQ, d), 1.0),
        'x_sample': nrm((DEC_BATCH, DEC_SEQ, d), 1.0),
        'cache_k': nrm((DEPTH, n_pool, PAGE_SIZE, A_HEADS, 2, A_DH), 1.0),
        'cache_v': nrm((DEPTH, n_pool, PAGE_SIZE, A_HEADS, A_DV), 1.0),
        'state_hgrn': nrm((DEPTH, DEC_BATCH, B_HEADS, B_DK, B_DV), 0.5),
        'state_conv': nrm((DEPTH, DEC_BATCH, CONV_W - 1, 2 * D_FF), 1.0),
        'page_table': page_table,
        'c_prompt': nrm((BATCH, d), 1.0),
        'c_sample': nrm((DEC_BATCH, d), 1.0),
        'w_ada': nrm((DEPTH, d, 6 * d), 0.5 * d ** -0.5),
        'b_ada': nrm((DEPTH, 6 * d), 0.01),
        'rms_mix_g': 1.0 + nrm((DEPTH, d), 0.02),
        'w_in': nrm((DEPTH, d, IN_W), d ** -0.5),
        'q_norm_g': 1.0 + nrm((DEPTH, A_DH), 0.02),
        'k_norm_g': 1.0 + nrm((DEPTH, A_DH), 0.02),
        'lambda_q1': nrm((DEPTH, A_DH), 0.1),
        'lambda_k1': nrm((DEPTH, A_DH), 0.1),
        'lambda_q2': nrm((DEPTH, A_DH), 0.1),
        'lambda_k2': nrm((DEPTH, A_DH), 0.1),
        'subln_g': 1.0 + nrm((DEPTH, A_DV), 0.02),
        'lb_logits': nrm((DEPTH + 1, B_K_W), 0.1),
        'hgrn_norm_g': 1.0 + nrm((DEPTH, B_DV), 0.02),
        'w_branch_a': nrm((DEPTH, A_V_W, d), A_V_W ** -0.5),
        'w_branch_b': nrm((DEPTH, B_V_W, d), B_V_W ** -0.5),
        'w_out': nrm((DEPTH, d, d), d ** -0.5),
        'rms_ffn_g': 1.0 + nrm((DEPTH, d), 0.02),
        'w_up': nrm((DEPTH, d, 2 * D_FF), d ** -0.5),
        'conv_w': nrm((DEPTH, CONV_W, 2 * D_FF), CONV_W ** -0.5),
        'conv_b': nrm((DEPTH, 2 * D_FF), 0.01),
        'w_down': nrm((DEPTH, D_FF, d), D_FF ** -0.5),
    }


def reference(x_prompt, x_sample, cache_k, cache_v, state_hgrn, state_conv, page_table, c_prompt, c_sample,
              w_ada, b_ada, rms_mix_g, w_in, q_norm_g, k_norm_g, lambda_q1, lambda_k1, lambda_q2, lambda_k2,
              subln_g, lb_logits, hgrn_norm_g, w_branch_a, w_branch_b, w_out, rms_ffn_g, w_up, conv_w, conv_b,
              w_down):
    bp, tp = x_prompt.shape[:2]
    bs, ts = x_sample.shape[:2]
    past_len = page_table.shape[1] * cache_k.shape[2]
    pos_p = jnp.arange(tp)
    pos_s = past_len + jnp.arange(ts)
    lb_all = jnp.cumsum(jax.nn.softmax(lb_logits.astype(F32), axis=0), axis=0)

    xp, xs = x_prompt, x_sample
    nk_p, nv_p, ns_p, nc_p = [], [], [], []
    nk_s, nv_s, ns_s, nc_s = [], [], [], []
    for l in range(DEPTH):
        lam_init = 0.8 - 0.6 * math.exp(-0.3 * l)
        lam = (jnp.exp(jnp.sum(lambda_q1[l].astype(F32) * lambda_k1[l].astype(F32)))
               - jnp.exp(jnp.sum(lambda_q2[l].astype(F32) * lambda_k2[l].astype(F32))) + lam_init)
        p = {
            'w_ada': w_ada[l], 'b_ada': b_ada[l], 'rms_mix_g': rms_mix_g[l], 'w_in': w_in[l],
            'q_norm_g': q_norm_g[l], 'k_norm_g': k_norm_g[l], 'lam': lam, 'lam_init': lam_init,
            'subln_g': subln_g[l], 'lb': lb_all[l], 'hgrn_norm_g': hgrn_norm_g[l],
            'w_branch_a': w_branch_a[l], 'w_branch_b': w_branch_b[l], 'w_out': w_out[l],
            'rms_ffn_g': rms_ffn_g[l], 'w_up': w_up[l], 'conv_w': conv_w[l], 'conv_b': conv_b[l],
            'w_down': w_down[l],
        }
        s0 = jnp.zeros((bp, B_HEADS, B_DK, B_DV), F32)
        buf0 = jnp.zeros((bp, CONV_W - 1, 2 * D_FF), xp.dtype)
        xp, kp, vp, sp, cp = layer_step(xp, c_prompt, pos_p, p, attend_prompt,
                                        functools.partial(hgrn_chunked, s0=s0), buf0)
        k_past = cache_k[l][page_table].reshape(bs, past_len, A_HEADS, 2, A_DH)
        v_past = cache_v[l][page_table].reshape(bs, past_len, A_HEADS, A_DV)
        xs, ks, vs, ss, cs = layer_step(xs, c_sample, pos_s, p,
                                        functools.partial(attend_sample, k_past=k_past, v_past=v_past),
                                        functools.partial(hgrn_steps, s0=state_hgrn[l].astype(F32)),
                                        state_conv[l])
        nk_p.append(kp); nv_p.append(vp); ns_p.append(sp.astype(state_hgrn.dtype)); nc_p.append(cp)
        nk_s.append(ks); nv_s.append(vs); ns_s.append(ss.astype(state_hgrn.dtype)); nc_s.append(cs)

    return (xp, xs,
            jnp.stack(nk_p), jnp.stack(nv_p), jnp.stack(ns_p), jnp.stack(nc_p),
            jnp.stack(nk_s), jnp.stack(nv_s), jnp.stack(ns_s), jnp.stack(nc_s))
```

```python
import functools
import math

import numpy as np
import jax
import jax.numpy as jnp
from jax import lax
from jax.experimental import pallas as pl
from jax.experimental.pallas import tpu as pltpu

F32 = jnp.float32
BF16 = jnp.bfloat16

D_MODEL = 1024
N_HEADS = 8
HEAD_W = 128
A_DH = 64
D_FF = 2816
N_SEG = 9
ROPE_THETA = 10000.0
EPS = 1e-6
NEG_BIG = -1e30
HGRN_CHUNK = 64
HGRN_LEVELS = (32, 16, 8, 4, 2, 1)
FFN_COLS = 256
VMEM_LIMIT = 56 * 1024 * 1024

NT_DIMS = (((1,), (1,)), ((), ()))
TN_DIMS = (((0,), (0,)), ((), ()))


def _const_spec(shape):
    nd = len(shape)
    return pl.BlockSpec(shape, lambda *_: (0,) * nd, pipeline_mode=pl.Buffered(1))


def _params(sem, vmem=VMEM_LIMIT):
    return pltpu.CompilerParams(dimension_semantics=sem, vmem_limit_bytes=vmem)


def _rms(x, g):
    ms = jnp.mean(x * x, axis=-1, keepdims=True)
    return (x * lax.rsqrt(ms + EPS)) * g


def _ada_kernel(c_ref, w_ref, b_ref, o_ref):
    a = jax.nn.silu(c_ref[...]).astype(BF16)
    o_ref[...] = jnp.dot(a, w_ref[...].astype(BF16), preferred_element_type=F32) + b_ref[...]


def _ada(c_all, w_ada, b_ada):
    m = c_all.shape[0]
    n = w_ada.shape[1]
    tn = 1536
    return pl.pallas_call(
        _ada_kernel,
        out_shape=jax.ShapeDtypeStruct((m, n), F32),
        grid=(n // tn,),
        in_specs=[
            _const_spec((m, D_MODEL)),
            pl.BlockSpec((D_MODEL, tn), lambda j: (0, j)),
            pl.BlockSpec((1, tn), lambda j: (0, j)),
        ],
        out_specs=pl.BlockSpec((m, tn), lambda j: (0, j)),
        compiler_params=_params(("arbitrary",)),
        name="ada_mod",
    )(c_all, w_ada, b_ada)


def _qk_norm_rope(z, g128, cos, sin, lane):
    sq = z * z
    first = lane < A_DH
    lo = jnp.sum(jnp.where(first, sq, 0.0), axis=-1, keepdims=True)
    al = jnp.sum(sq, axis=-1, keepdims=True)
    ms = jnp.where(first, lo, al - lo) * (1.0 / A_DH)
    zn = (z * lax.rsqrt(ms + EPS)) * g128
    rot = jnp.where((lane % A_DH) < A_DH // 2, pltpu.roll(zn, HEAD_W - A_DH // 2, 1), pltpu.roll(zn, A_DH // 2, 1))
    return zn * cos + rot * sin


def _lower_bound(lbl_ref):
    lbl = lbl_ref[...]
    mx = jnp.max(lbl, axis=0, keepdims=True)
    e = jnp.exp(lbl - mx)
    return e[0:1, :] / jnp.sum(e, axis=0, keepdims=True)


def _inproj_common(x, sh, sc, gmix_ref, w_ref):
    xn = _rms(x, gmix_ref[...]) * (1.0 + sc) + sh
    xb = xn.astype(BF16)

    def seg(i):
        return jnp.dot(xb, w_ref[:, i * D_MODEL:(i + 1) * D_MODEL], preferred_element_type=F32)

    return seg


def _inproj_prompt_kernel(x_ref, sh_ref, sc_ref, gmix_ref, w_ref, qg_ref, kg_ref, cos_ref, sin_ref, lbl_ref,
                          qhm_ref, ktok_ref, khm_ref, vtok_ref, vhm_ref,
                          qh_ref, kh_ref, gh_ref, vh_ref, og_ref, ga_ref, gb_ref):
    seg = _inproj_common(x_ref[0], sh_ref[0], sc_ref[0], gmix_ref, w_ref)
    cos = cos_ref[...]
    sin = sin_ref[...]
    lane = lax.broadcasted_iota(jnp.int32, (1, HEAD_W), 1)
    zq = seg(0)
    zk = seg(1)
    for h in range(N_HEADS):
        hs = slice(h * HEAD_W, (h + 1) * HEAD_W)
        q = _qk_norm_rope(zq[:, hs], qg_ref[...], cos, sin, lane) * (A_DH ** -0.5)
        qhm_ref[0, h] = q.astype(BF16)
        k = _qk_norm_rope(zk[:, hs], kg_ref[...], cos, sin, lane)
        ktok_ref[0, :, hs] = k
        khm_ref[0, h] = k.astype(BF16)
    zv = seg(2)
    vtok_ref[0] = zv
    for h in range(N_HEADS):
        vhm_ref[0, h] = zv[:, h * HEAD_W:(h + 1) * HEAD_W].astype(BF16)
    qh_ref[0] = jax.nn.silu(seg(3))
    lb = _lower_bound(lbl_ref)
    f = lb + (1.0 - lb) * jax.nn.sigmoid(seg(4))
    kh_ref[0] = 1.0 - f
    gh_ref[0] = jnp.log(f)
    vh_ref[0] = seg(5)
    og_ref[0] = jax.nn.silu(seg(6))
    ga_ref[0] = jax.nn.sigmoid(seg(7))
    gb_ref[0] = jax.nn.sigmoid(seg(8))


def _inproj_prompt(x, sh, sc, gmix, w_in, qg, kg, cos, sin, lbl, tm=256):
    b, t, _ = x.shape
    tok = jax.ShapeDtypeStruct((b, t, D_MODEL), F32)
    hm = jax.ShapeDtypeStruct((b, N_HEADS, t, HEAD_W), BF16)
    tok_spec = pl.BlockSpec((1, tm, D_MODEL), lambda i, j: (i, j, 0))
    hm_spec = pl.BlockSpec((1, N_HEADS, tm, HEAD_W), lambda i, j: (i, 0, j, 0))
    mod_spec = pl.BlockSpec((1, 1, D_MODEL), lambda i, j: (i, 0, 0))
    tab_spec = pl.BlockSpec((tm, HEAD_W), lambda i, j: (j, 0))
    return pl.pallas_call(
        _inproj_prompt_kernel,
        out_shape=(hm, tok, hm, tok, hm, tok, tok, tok, tok, tok, tok, tok),
        grid=(b, t // tm),
        in_specs=[tok_spec, mod_spec, mod_spec, _const_spec((1, D_MODEL)), _const_spec(w_in.shape),
                  _const_spec((1, HEAD_W)), _const_spec((1, HEAD_W)), tab_spec, tab_spec, _const_spec(lbl.shape)],
        out_specs=(hm_spec, tok_spec, hm_spec, tok_spec, hm_spec,
                   tok_spec, tok_spec, tok_spec, tok_spec, tok_spec, tok_spec, tok_spec),
        compiler_params=_params(("parallel", "parallel")),
        name="inproj_prompt",
    )(x, sh, sc, gmix, w_in, qg, kg, cos, sin, lbl)


def _inproj_sample_kernel(x_ref, sh_ref, sc_ref, gmix_ref, w_ref, qg_ref, kg_ref, cos_ref, sin_ref, lbl_ref,
                          q_ref, k_ref, v_ref, qt_ref, ft_ref, vh_ref, og_ref, ga_ref, gb_ref):
    seg = _inproj_common(x_ref[...], sh_ref[...], sc_ref[...], gmix_ref, w_ref)
    cos = cos_ref[...]
    sin = sin_ref[...]
    lane = lax.broadcasted_iota(jnp.int32, (1, HEAD_W), 1)
    zq = seg(0)
    zk = seg(1)
    for h in range(N_HEADS):
        hs = slice(h * HEAD_W, (h + 1) * HEAD_W)
        q_ref[:, hs] = _qk_norm_rope(zq[:, hs], qg_ref[...], cos, sin, lane) * (A_DH ** -0.5)
        k_ref[:, hs] = _qk_norm_rope(zk[:, hs], kg_ref[...], cos, sin, lane)
    v_ref[...] = seg(2)
    qh = jax.nn.silu(seg(3))
    lb = _lower_bound(lbl_ref)
    f = lb + (1.0 - lb) * jax.nn.sigmoid(seg(4))
    for h in range(N_HEADS):
        hs = slice(h * HEAD_W, (h + 1) * HEAD_W)
        qt_ref[hs, :] = qh[:, hs].T
        ft_ref[hs, :] = f[:, hs].T
    vh_ref[...] = seg(5)
    og_ref[...] = jax.nn.silu(seg(6))
    ga_ref[...] = jax.nn.sigmoid(seg(7))
    gb_ref[...] = jax.nn.sigmoid(seg(8))


def _inproj_sample(x, sh, sc, gmix, w_in, qg, kg, cos, sin, lbl):
    m = x.shape[0]
    tok = jax.ShapeDtypeStruct((m, D_MODEL), F32)
    chan = jax.ShapeDtypeStruct((D_MODEL, m), F32)
    ts = _const_spec((m, D_MODEL))
    cs = _const_spec((D_MODEL, m))
    return pl.pallas_call(
        _inproj_sample_kernel,
        out_shape=(tok, tok, tok, chan, chan, tok, tok, tok, tok),
        grid=(1,),
        in_specs=[ts, ts, ts, _const_spec((1, D_MODEL)), _const_spec(w_in.shape),
                  _const_spec((1, HEAD_W)), _const_spec((1, HEAD_W)), _const_spec((1, HEAD_W)),
                  _const_spec((1, HEAD_W)), _const_spec(lbl.shape)],
        out_specs=(ts, ts, ts, cs, cs, ts, ts, ts, ts),
        compiler_params=_params(("arbitrary",)),
        name="inproj_sample",
    )(x, sh, sc, gmix, w_in, qg, kg, cos, sin, lbl)


def _hgrn_decay_matrix():
    c = HGRN_CHUNK
    t = np.arange(c)[:, None]
    u = np.arange(c)[None, :]
    rows = [u <= t, u > t]
    for h in HGRN_LEVELS:
        odd = ((t // h) % 2) == 1
        bnd = (t // (2 * h)) * (2 * h) + h - 1
        rows.append(np.where(odd, (u > bnd) & (u <= t), (u > t) & (u <= bnd)))
    return np.concatenate(rows, axis=0).astype(np.float32)


def _hgrn_prompt_kernel(dall_ref, q_ref, k_ref, g_ref, v_ref, og_ref, gn_ref, ob_ref, sout_ref, st_ref, *, ct):
    c = HGRN_CHUNK
    tstep = pl.program_id(1)

    @pl.when(tstep == 0)
    def _():
        st_ref[...] = jnp.zeros_like(st_ref)

    ti = lax.broadcasted_iota(jnp.int32, (c, 1), 0)
    tt = lax.broadcasted_iota(jnp.int32, (c, c), 0)
    ss = lax.broadcasted_iota(jnp.int32, (c, c), 1)
    odd = [((ti // h) % 2) == 1 for h in HGRN_LEVELS]
    pair = [(tt // (2 * h)) == (ss // (2 * h)) for h in HGRN_LEVELS]
    eye = tt == ss
    dall = dall_ref[...]
    gn = gn_ref[...]

    def chunk(ci, carry):
        rows = pl.ds(pl.multiple_of(ci * c, c), c)
        g = g_ref[0, rows, :]
        g_hi = g.astype(BF16)
        r1 = g - g_hi.astype(F32)
        g_mid = r1.astype(BF16)
        g_lo = (r1 - g_mid.astype(F32)).astype(BF16)
        ee = (jnp.dot(dall, g_hi, preferred_element_type=F32)
              + jnp.dot(dall, g_mid, preferred_element_type=F32)
              + jnp.dot(dall, g_lo, preferred_element_type=F32))
        ex = jnp.exp(ee)
        q = q_ref[0, rows, :]
        k = k_ref[0, rows, :]
        v = v_ref[0, rows, :]
        og = og_ref[0, rows, :]
        q_in = q * ex[0:c]
        k_out = k * ex[c:2 * c]
        d_all = ex[c - 1:c, :]
        for h in range(N_HEADS):
            hs = slice(h * HEAD_W, (h + 1) * HEAD_W)
            qh = q[:, hs]
            kh = k[:, hs]
            a = jnp.where(eye, jnp.sum(qh * kh, axis=-1, keepdims=True), 0.0)
            for li in range(len(HGRN_LEVELS)):
                e = ex[(2 + li) * c:(3 + li) * c, hs]
                qe = jnp.where(odd[li], qh * e, 0.0).astype(BF16)
                ke = jnp.where(odd[li], 0.0, kh * e).astype(BF16)
                al = lax.dot_general(qe, ke, NT_DIMS, preferred_element_type=F32)
                a = a + (al if li == 0 else jnp.where(pair[li], al, 0.0))
            vb = v[:, hs].astype(BF16)
            st = st_ref[h]
            o = (jnp.dot(a.astype(BF16), vb, preferred_element_type=F32)
                 + lax.dot_general(q_in[:, hs].astype(BF16), st.astype(BF16), NT_DIMS, preferred_element_type=F32))
            st_ref[h] = d_all[:, hs] * st + lax.dot_general(vb, k_out[:, hs].astype(BF16), TN_DIMS,
                                                            preferred_element_type=F32)
            ob_ref[0, rows, hs] = (_rms(o, gn) * og[:, hs]).astype(BF16)
        return carry

    lax.fori_loop(0, ct // c, chunk, 0)

    @pl.when(tstep == pl.num_programs(1) - 1)
    def _():
        for h in range(N_HEADS):
            sout_ref[0, h] = st_ref[h].T


def _hgrn_prompt(qh, kh, gh, vh, og, gn, ct=256):
    b, t, _ = qh.shape
    dall = jnp.asarray(_hgrn_decay_matrix(), dtype=BF16)
    tok_spec = pl.BlockSpec((1, ct, D_MODEL), lambda i, j: (i, j, 0))
    return pl.pallas_call(
        functools.partial(_hgrn_prompt_kernel, ct=ct),
        out_shape=(jax.ShapeDtypeStruct((b, t, D_MODEL), BF16),
                   jax.ShapeDtypeStruct((b, N_HEADS, HEAD_W, HEAD_W), F32)),
        grid=(b, t // ct),
        in_specs=[_const_spec(dall.shape), tok_spec, tok_spec, tok_spec, tok_spec, tok_spec, _const_spec((1, HEAD_W))],
        out_specs=(tok_spec, pl.BlockSpec((1, N_HEADS, HEAD_W, HEAD_W), lambda i, j: (i, 0, 0, 0))),
        scratch_shapes=[pltpu.VMEM((N_HEADS, HEAD_W, HEAD_W), F32)],
        compiler_params=_params(("parallel", "arbitrary")),
        name="hgrn_prompt",
    )(dall, qh, kh, gh, vh, og, gn)


def _lambda(lam_ref, lam_init):
    lv = lam_ref[...]
    s1 = jnp.sum(lv[0:1] * lv[1:2], axis=-1, keepdims=True)
    s2 = jnp.sum(lv[2:3] * lv[3:4], axis=-1, keepdims=True)
    return jnp.exp(s1) - jnp.exp(s2) + lam_init


def _attn_prompt_kernel(lam_ref, g_ref, q_ref, k_ref, v_ref, o_ref, *, tq, tk, lam_init):
    qi = pl.program_id(2)
    q = q_ref[0, 0]
    lane = lax.broadcasted_iota(jnp.int32, (1, HEAD_W), 1)
    zero = jnp.zeros_like(q)
    qs = jnp.concatenate([jnp.where(lane < A_DH, q, zero), jnp.where(lane < A_DH, zero, q)], axis=0)
    rep = tk // HEAD_W

    def step(kb, carry, masked):
        m, l, acc = carry
        rows = pl.ds(pl.multiple_of(kb * tk, tk), tk)
        kblk = k_ref[0, 0, rows, :]
        vblk = v_ref[0, 0, rows, :]
        s = lax.dot_general(qs, kblk, NT_DIMS, preferred_element_type=F32)
        if masked:
            r = lax.broadcasted_iota(jnp.int32, (2 * tq, tk), 0)
            qpos = qi * tq + jnp.where(r >= tq, r - tq, r)
            kpos = kb * tk + lax.broadcasted_iota(jnp.int32, (2 * tq, tk), 1)
            s = jnp.where(kpos <= qpos, s, NEG_BIG)
        m_new = jnp.maximum(m, jnp.max(s, axis=-1, keepdims=True))
        alpha = jnp.exp(m - m_new)
        p = jnp.exp(s - jnp.tile(m_new, (1, rep)))
        l_new = alpha * l + jnp.sum(p, axis=-1, keepdims=True)
        acc_new = alpha * acc + jnp.dot(p.astype(BF16), vblk, preferred_element_type=F32)
        return m_new, l_new, acc_new

    init = (jnp.full((2 * tq, HEAD_W), NEG_BIG, F32), jnp.zeros((2 * tq, HEAD_W), F32),
            jnp.zeros((2 * tq, HEAD_W), F32))
    nfull = qi * (tq // tk)
    carry = lax.fori_loop(0, nfull, lambda kb, c: step(kb, c, False), init)
    for j in range(tq // tk):
        carry = step(nfull + j, carry, True)
    _, l, acc = carry
    o = acc / l
    lam = _lambda(lam_ref, lam_init)
    od = o[0:tq] - lam * o[tq:2 * tq]
    o_ref[0, 0] = _rms(od, g_ref[...]).astype(BF16)


def _attn_prompt(lam4, g_sub, q_hm, k_hm, v_hm, lam_init, tq=512, tk=256):
    b, nh, t, _ = q_hm.shape
    kv_spec = pl.BlockSpec((1, 1, t, HEAD_W), lambda i, h, j: (i, h, 0, 0))
    q_spec = pl.BlockSpec((1, 1, tq, HEAD_W), lambda i, h, j: (i, h, j, 0))
    return pl.pallas_call(
        functools.partial(_attn_prompt_kernel, tq=tq, tk=tk, lam_init=lam_init),
        out_shape=jax.ShapeDtypeStruct((b, nh, t, HEAD_W), BF16),
        grid=(b, nh, t // tq),
        in_specs=[_const_spec(lam4.shape), _const_spec((1, HEAD_W)), q_spec, kv_spec, kv_spec],
        out_specs=q_spec,
        compiler_params=_params(("parallel", "parallel", "arbitrary")),
        name="attn_prompt",
    )(lam4, g_sub, q_hm, k_hm, v_hm)


def _mix_kernel(x_ref, gm_ref, oa_ref, ob_ref, ga_ref, gb_ref, wa_ref, wb_ref, wo_ref, o_ref):
    oa = jnp.concatenate([oa_ref[0, h] for h in range(N_HEADS)], axis=-1)
    ya = jnp.dot(oa, wa_ref[...], preferred_element_type=F32)
    yb = jnp.dot(ob_ref[0].astype(BF16), wb_ref[...], preferred_element_type=F32)
    merged = ga_ref[0] * ya + gb_ref[0] * yb
    mix = jnp.dot(merged.astype(BF16), wo_ref[...], preferred_element_type=F32)
    o_ref[0] = x_ref[0] + gm_ref[0] * mix


def _mix(x, gm, oa_hm, ob, ga, gb, wa, wb, wo, tm):
    b, t, _ = x.shape
    rows_mod = gm.shape[1]
    tok_spec = pl.BlockSpec((1, tm, D_MODEL), lambda i, j: (i, j, 0))
    mod_spec = (pl.BlockSpec((1, 1, D_MODEL), lambda i, j: (i, 0, 0)) if rows_mod == 1 else tok_spec)
    return pl.pallas_call(
        _mix_kernel,
        out_shape=jax.ShapeDtypeStruct((b, t, D_MODEL), F32),
        grid=(b, t // tm),
        in_specs=[tok_spec, mod_spec, pl.BlockSpec((1, N_HEADS, tm, HEAD_W), lambda i, j: (i, 0, j, 0)),
                  tok_spec, tok_spec, tok_spec,
                  _const_spec(wa.shape), _const_spec(wb.shape), _const_spec(wo.shape)],
        out_specs=tok_spec,
        compiler_params=_params(("parallel", "parallel")),
        name="mix_out",
    )(x, gm, oa_hm, ob, ga, gb, wa, wb, wo)


def _ffn_kernel(*refs, tm, carried):
    if carried:
        (x_ref, sh_ref, sc_ref, gt_ref, g_ref, wup_ref, cw_ref, cb_ref, wdn_ref, y_ref, tail_ref, prev_ref) = refs
    else:
        (x_ref, sh_ref, sc_ref, gt_ref, g_ref, wup_ref, cw_ref, cb_ref, wdn_ref, b0_ref, b1_ref, y_ref, u_ref) = refs
    x = x_ref[0]
    hn = (_rms(x, g_ref[...]) * (1.0 + sc_ref[0]) + sh_ref[0]).astype(BF16)
    if carried:
        @pl.when(pl.program_id(1) == 0)
        def _():
            prev_ref[...] = jnp.zeros_like(prev_ref)
        row = lax.broadcasted_iota(jnp.int32, (tm, 1), 0)

    def conv(u, cols):
        if carried:
            p6 = prev_ref[6:7, cols]
            p7 = prev_ref[7:8, cols]
            u1 = jnp.where(row == 0, p7, pltpu.roll(u, 1, 0))
            u2 = jnp.where(row == 0, p6, jnp.where(row == 1, p7, pltpu.roll(u, 2, 0)))
            prev_ref[:, cols] = u[tm - 8:tm, :]
            tail_ref[0, :, cols] = u[tm - 8:tm, :]
        else:
            u2 = b0_ref[0, :, cols]
            u1 = b1_ref[0, :, cols]
            u_ref[0, :, cols] = u
        return cb_ref[:, cols] + cw_ref[0:1, cols] * u2 + cw_ref[1:2, cols] * u1 + cw_ref[2:3, cols] * u

    acc = jnp.zeros((tm, D_MODEL), F32)
    for c in range(D_FF // FFN_COLS):
        ca = slice(c * FFN_COLS, (c + 1) * FFN_COLS)
        cg = slice(D_FF + c * FFN_COLS, D_FF + (c + 1) * FFN_COLS)
        ya = conv(jnp.dot(hn, wup_ref[:, ca], preferred_element_type=F32), ca)
        yg = conv(jnp.dot(hn, wup_ref[:, cg], preferred_element_type=F32), cg)
        act = (jax.nn.silu(ya) * yg).astype(BF16)
        acc = acc + jnp.dot(act, wdn_ref[ca, :], preferred_element_type=F32)
    y_ref[0] = x + gt_ref[0] * acc


def _ffn(x1, sh, sc, gt, g, wup, cw, cb, wdn, tm, bufs=None):
    b, t, _ = x1.shape
    carried = bufs is None
    tok_spec = pl.BlockSpec((1, tm, D_MODEL), lambda i, j: (i, j, 0))
    mod_spec = (pl.BlockSpec((1, 1, D_MODEL), lambda i, j: (i, 0, 0)) if sh.shape[1] == 1 else tok_spec)
    wide_spec = pl.BlockSpec((1, tm, 2 * D_FF), lambda i, j: (i, j, 0))
    in_specs = [tok_spec, mod_spec, mod_spec, mod_spec, _const_spec((1, D_MODEL)), _const_spec(wup.shape),
                _const_spec(cw.shape), _const_spec(cb.shape), _const_spec(wdn.shape)]
    args = [x1, sh, sc, gt, g, wup, cw, cb, wdn]
    if carried:
        out_shape = (jax.ShapeDtypeStruct((b, t, D_MODEL), F32), jax.ShapeDtypeStruct((b, 8, 2 * D_FF), F32))
        out_specs = (tok_spec, pl.BlockSpec((1, 8, 2 * D_FF), lambda i, j: (i, 0, 0)))
        scratch = [pltpu.VMEM((8, 2 * D_FF), F32)]
        sem = ("parallel", "arbitrary")
    else:
        in_specs += [wide_spec, wide_spec]
        args += list(bufs)
        out_shape = (jax.ShapeDtypeStruct((b, t, D_MODEL), F32), jax.ShapeDtypeStruct((b, t, 2 * D_FF), F32))
        out_specs = (tok_spec, wide_spec)
        scratch = []
        sem = ("parallel", "parallel")
    return pl.pallas_call(
        functools.partial(_ffn_kernel, tm=tm, carried=carried),
        out_shape=out_shape,
        grid=(b, t // tm),
        in_specs=in_specs,
        out_specs=out_specs,
        scratch_shapes=scratch,
        compiler_params=_params(sem),
        name="conv_ffn",
    )(*args)


def _hgrn_sample_kernel(ft_ref, qt_ref, v_ref, og_ref, gn_ref, s_ref, so_ref, o_ref, *, nb):
    kc = pl.program_id(1)

    @pl.when(kc == 0)
    def _():
        o_ref[...] = jnp.zeros_like(o_ref)

    ft = ft_ref[...]
    qt = qt_ref[...]
    for b in range(nb):
        f = ft[:, b:b + 1]
        s_new = f * s_ref[b, 0] + (1.0 - f) * v_ref[b:b + 1, :]
        so_ref[b, 0] = s_new
        o_ref[b:b + 1, :] += jnp.sum(qt[:, b:b + 1] * s_new, axis=0, keepdims=True)

    @pl.when(kc == pl.num_programs(1) - 1)
    def _():
        o_ref[...] = _rms(o_ref[...], gn_ref[...]) * og_ref[...]


def _hgrn_sample(ft, qt, vh, og, gn, state, rows=16):
    nb = vh.shape[0]
    nkc = HEAD_W // rows
    st_spec = pl.BlockSpec((nb, 1, rows, HEAD_W), lambda h, c: (0, h, c, 0))
    ch_spec = pl.BlockSpec((rows, nb), lambda h, c: (h * nkc + c, 0))
    hd_spec = pl.BlockSpec((nb, HEAD_W), lambda h, c: (0, h))
    return pl.pallas_call(
        functools.partial(_hgrn_sample_kernel, nb=nb),
        out_shape=(jax.ShapeDtypeStruct(state.shape, F32), jax.ShapeDtypeStruct((nb, D_MODEL), F32)),
        grid=(N_HEADS, nkc),
        in_specs=[ch_spec, ch_spec, hd_spec, hd_spec, _const_spec((1, HEAD_W)), st_spec],
        out_specs=(st_spec, hd_spec),
        compiler_params=_params(("parallel", "arbitrary")),
        name="hgrn_sample",
    )(ft, qt, vh, og, gn, state)


def _attn_sample_kernel(pt_ref, lam_ref, g_ref, q_ref, kn_ref, vn_ref, ck_hbm, cv_hbm, o_ref,
                        kbuf, vbuf, ktail, vtail, sem, *, n_pages, page, lam_init):
    b = pl.program_id(0)
    nb = pl.num_programs(0)
    slot = b % 2

    def copies(bb, sl):
        out = []
        for p in range(n_pages):
            pg = pt_ref[bb, p]
            out.append(pltpu.make_async_copy(ck_hbm.at[pg], kbuf.at[sl, p], sem.at[0, sl]))
            out.append(pltpu.make_async_copy(cv_hbm.at[pg], vbuf.at[sl, p], sem.at[1, sl]))
        return out

    @pl.when(b == 0)
    def _():
        for cp in copies(0, 0):
            cp.start()
        ktail[...] = jnp.zeros_like(ktail)
        vtail[...] = jnp.zeros_like(vtail)

    @pl.when(b + 1 < nb)
    def _():
        for cp in copies(b + 1, 1 - slot):
            cp.start()

    for cp in copies(b, slot):
        cp.wait()

    q = q_ref[pl.ds(b, 1), :]
    r16 = lax.broadcasted_iota(jnp.int32, (2 * N_HEADS, D_MODEL), 0)
    ln = lax.broadcasted_iota(jnp.int32, (2 * N_HEADS, D_MODEL), 1)
    sel = ((ln // HEAD_W) == (r16 % N_HEADS)) & (((ln // A_DH) % 2) == (r16 // N_HEADS))
    qblk = jnp.where(sel, q, 0.0).astype(BF16)
    ktail[0:1, :] = kn_ref[pl.ds(b, 1), :]
    vtail[0:1, :] = vn_ref[pl.ds(b, 1), :]
    parts = [lax.dot_general(qblk, kbuf[slot, p].astype(BF16), NT_DIMS, preferred_element_type=F32)
             for p in range(n_pages)]
    parts.append(lax.dot_general(qblk, ktail[...].astype(BF16), NT_DIMS, preferred_element_type=F32))
    s = jnp.concatenate(parts, axis=1)
    col = lax.broadcasted_iota(jnp.int32, s.shape, 1)
    s = jnp.where(col <= n_pages * page, s, NEG_BIG)
    m = jnp.max(s, axis=-1, keepdims=True)
    e = jnp.exp(s - m)
    pr = e / jnp.sum(e, axis=-1, keepdims=True)
    lam = _lambda(lam_ref, lam_init)
    pd = pr[0:N_HEADS] - lam * pr[N_HEADS:2 * N_HEADS]
    pd = jnp.concatenate([pd, jnp.zeros_like(pd)], axis=0).astype(BF16)
    r = jnp.dot(pd[:, n_pages * page:], vtail[...].astype(BF16), preferred_element_type=F32)
    for p in range(n_pages):
        r = r + jnp.dot(pd[:, p * page:(p + 1) * page], vbuf[slot, p].astype(BF16), preferred_element_type=F32)
    r8 = lax.broadcasted_iota(jnp.int32, (N_HEADS, D_MODEL), 0)
    l8 = lax.broadcasted_iota(jnp.int32, (N_HEADS, D_MODEL), 1)
    o = jnp.sum(jnp.where((l8 // HEAD_W) == r8, r[0:N_HEADS], 0.0), axis=0, keepdims=True)
    for h in range(N_HEADS):
        o_ref[h, pl.ds(b, 1), :] = _rms(o[:, h * HEAD_W:(h + 1) * HEAD_W], g_ref[...])


def _attn_sample(page_table, lam4, g_sub, q, k_new, v_new, cache_k, cache_v, lam_init):
    nb, n_pages = page_table.shape
    page = cache_k.shape[1]
    full = lambda shape: pl.BlockSpec(shape, lambda i, pt: (0,) * len(shape))
    any_spec = pl.BlockSpec(memory_space=pl.ANY)
    grid_spec = pltpu.PrefetchScalarGridSpec(
        num_scalar_prefetch=1,
        grid=(nb,),
        in_specs=[full(lam4.shape), full((1, HEAD_W)), full(q.shape), full(q.shape), full(q.shape),
                  any_spec, any_spec],
        out_specs=full((N_HEADS, nb, HEAD_W)),
        scratch_shapes=[pltpu.VMEM((2, n_pages, page, D_MODEL), F32), pltpu.VMEM((2, n_pages, page, D_MODEL), F32),
                        pltpu.VMEM((page, D_MODEL), F32), pltpu.VMEM((page, D_MODEL), F32),
                        pltpu.SemaphoreType.DMA((2, 2))],
    )
    return pl.pallas_call(
        functools.partial(_attn_sample_kernel, n_pages=n_pages, page=page, lam_init=lam_init),
        out_shape=jax.ShapeDtypeStruct((N_HEADS, nb, HEAD_W), F32),
        grid_spec=grid_spec,
        compiler_params=_params(("arbitrary",)),
        name="attn_sample",
    )(page_table, lam4, g_sub, q, k_new, v_new, cache_k, cache_v)


def _rope_tables(pos):
    half = A_DH // 2
    inv = ROPE_THETA ** (-jnp.arange(half, dtype=F32) / half)
    ang = pos.astype(F32)[:, None] * inv[None, :]
    cos = jnp.cos(ang)
    sin = jnp.sin(ang)
    return jnp.tile(cos, (1, 4)), jnp.concatenate([-sin, sin, -sin, sin], axis=-1)


def kernel(x_prompt, x_sample, cache_k, cache_v, state_hgrn, state_conv, page_table, c_prompt, c_sample, w_ada, b_ada,
           rms_mix_g, w_in, q_norm_g, k_norm_g, lambda_q1, lambda_k1, lambda_q2, lambda_k2, subln_g, lb_logits,
           hgrn_norm_g, w_branch_a, w_branch_b, w_out, rms_ffn_g, w_up, conv_w, conv_b, w_down):
    bp, tp, d = x_prompt.shape
    bs, ts, _ = x_sample.shape
    assert d == D_MODEL and ts == 1 and w_in.shape[0] == 1
    n_pool, page = cache_k.shape[1], cache_k.shape[2]
    past_len = page_table.shape[1] * page
    lam_init = 0.8 - 0.6 * math.exp(-0.3 * 0)

    pad = (-(bp + bs)) % 8
    c_all = jnp.concatenate([c_prompt, c_sample, jnp.zeros((pad, d), F32)], axis=0)
    mod = _ada(c_all, w_ada[0], b_ada[0].reshape(1, -1))
    mp = [mod[:bp, i * d:(i + 1) * d].reshape(bp, 1, d) for i in range(6)]
    msamp = [mod[bp:bp + bs, i * d:(i + 1) * d] for i in range(6)]

    w_in_b = w_in[0].astype(BF16)
    wa_b = w_branch_a[0].astype(BF16)
    wb_b = w_branch_b[0].astype(BF16)
    wo_b = w_out[0].astype(BF16)
    wup_b = w_up[0].astype(BF16)
    wdn_b = w_down[0].astype(BF16)
    gmix = rms_mix_g[0].reshape(1, d)
    gffn = rms_ffn_g[0].reshape(1, d)
    qg = jnp.tile(q_norm_g[0], 2).reshape(1, HEAD_W)
    kg = jnp.tile(k_norm_g[0], 2).reshape(1, HEAD_W)
    lam4 = jnp.stack([lambda_q1[0], lambda_k1[0], lambda_q2[0], lambda_k2[0]], axis=0)
    g_sub = (subln_g[0] * (1.0 - lam_init)).reshape(1, HEAD_W)
    gn = hgrn_norm_g[0].reshape(1, HEAD_W)
    cw = conv_w[0]
    cb = conv_b[0].reshape(1, -1)

    cos_p, sin_p = _rope_tables(jnp.arange(tp))
    (q_hm, k_tok, k_hm, v_tok, v_hm, qh, kh, gh, vh, og, ga, gb) = _inproj_prompt(
        x_prompt, mp[0], mp[1], gmix, w_in_b, qg, kg, cos_p, sin_p, lb_logits)
    ob, s_prompt = _hgrn_prompt(qh, kh, gh, vh, og, gn)
    oa_hm = _attn_prompt(lam4, g_sub, q_hm, k_hm, v_hm, lam_init)
    x1 = _mix(x_prompt, mp[2], oa_hm, ob, ga, gb, wa_b, wb_b, wo_b, tm=512)
    y_prompt, tail = _ffn(x1, mp[3], mp[4], mp[5], gffn, wup_b, cw, cb, wdn_b, tm=512)

    cos_s, sin_s = _rope_tables(jnp.full((1,), past_len))
    xs = x_sample.reshape(bs, d)
    (q_s, k_s, v_s, qt_s, ft_s, vh_s, og_s, ga_s, gb_s) = _inproj_sample(
        xs, msamp[0], msamp[1], gmix, w_in_b, qg, kg, cos_s, sin_s, lb_logits)
    s_sample, ob_s = _hgrn_sample(ft_s, qt_s, vh_s, og_s, gn, state_hgrn[0])
    oa_s = _attn_sample(page_table, lam4, g_sub, q_s, k_s, v_s,
                        cache_k[0].reshape(n_pool, page, d), cache_v[0].reshape(n_pool, page, d), lam_init)
    x1_s = _mix(xs[None], msamp[2][None], oa_s[None].astype(BF16), ob_s[None], ga_s[None], gb_s[None],
                wa_b, wb_b, wo_b, tm=bs)
    y_s, u_s = _ffn(x1_s, msamp[3][None], msamp[4][None], msamp[5][None], gffn, wup_b, cw, cb, wdn_b, tm=bs,
                    bufs=(state_conv[0][:, 0][None], state_conv[0][:, 1][None]))

    new_conv_s = jnp.stack([state_conv[0][:, 1], u_s[0]], axis=1)
    return (y_prompt, y_s.reshape(bs, 1, d),
            k_tok.reshape(1, bp, tp, N_HEADS, 2, A_DH), v_tok.reshape(1, bp, tp, N_HEADS, HEAD_W),
            s_prompt[None], tail[:, 6:8][None],
            k_s.reshape(1, bs, 1, N_HEADS, 2, A_DH), v_s.reshape(1, bs, 1, N_HEADS, HEAD_W),
            s_sample[None], new_conv_s[None])
```

```python
import functools
import math

import numpy as np
import jax
import jax.numpy as jnp
from jax import lax
from jax.experimental import pallas as pl
from jax.experimental.pallas import tpu as pltpu

F32 = jnp.float32
BF16 = jnp.bfloat16

D_MODEL = 1024
N_HEADS = 8
HEAD_W = 128
A_DH = 64
D_FF = 2816
N_SEG = 9
ROPE_THETA = 10000.0
EPS = 1e-6
NEG_BIG = -1e30
HGRN_CHUNK = 64
HGRN_LEVELS = (32, 16, 8, 4, 2, 1)
FFN_COLS = 256
VMEM_LIMIT = 56 * 1024 * 1024
Q_SCALE = (A_DH ** -0.5) * math.log2(math.e)

NT_DIMS = (((1,), (1,)), ((), ()))
TN_DIMS = (((0,), (0,)), ((), ()))


def _const_spec(shape):
    nd = len(shape)
    return pl.BlockSpec(shape, lambda *_: (0,) * nd, pipeline_mode=pl.Buffered(1))


def _params(sem, vmem=VMEM_LIMIT):
    return pltpu.CompilerParams(dimension_semantics=sem, vmem_limit_bytes=vmem)


def _rms(x, g):
    ms = jnp.mean(x * x, axis=-1, keepdims=True)
    return (x * lax.rsqrt(ms + EPS)) * g


def _ada_kernel(c_ref, w_ref, b_ref, o_ref):
    a = jax.nn.silu(c_ref[...]).astype(BF16)
    o_ref[...] = jnp.dot(a, w_ref[...].astype(BF16), preferred_element_type=F32) + b_ref[...]


def _ada(c_all, w_ada, b_ada):
    m = c_all.shape[0]
    n = w_ada.shape[1]
    tn = 1536
    return pl.pallas_call(
        _ada_kernel,
        out_shape=jax.ShapeDtypeStruct((m, n), F32),
        grid=(n // tn,),
        in_specs=[
            _const_spec((m, D_MODEL)),
            pl.BlockSpec((D_MODEL, tn), lambda j: (0, j)),
            pl.BlockSpec((1, tn), lambda j: (0, j)),
        ],
        out_specs=pl.BlockSpec((m, tn), lambda j: (0, j)),
        compiler_params=_params(("arbitrary",)),
        name="ada_mod",
    )(c_all, w_ada, b_ada)


def _qk_norm_rope(z, g128, cos, sin, lane):
    sq = z * z
    first = lane < A_DH
    lo = jnp.sum(jnp.where(first, sq, 0.0), axis=-1, keepdims=True)
    al = jnp.sum(sq, axis=-1, keepdims=True)
    ms = jnp.where(first, lo, al - lo) * (1.0 / A_DH)
    zn = (z * lax.rsqrt(ms + EPS)) * g128
    rot = jnp.where((lane % A_DH) < A_DH // 2, pltpu.roll(zn, HEAD_W - A_DH // 2, 1), pltpu.roll(zn, A_DH // 2, 1))
    return zn * cos + rot * sin


def _lower_bound(lbl_ref):
    lbl = lbl_ref[...]
    mx = jnp.max(lbl, axis=0, keepdims=True)
    e = jnp.exp(lbl - mx)
    return e[0:1, :] / jnp.sum(e, axis=0, keepdims=True)


def _inproj_common(x, sh, sc, gmix_ref, w_ref):
    xn = _rms(x, gmix_ref[...]) * (1.0 + sc) + sh
    xb = xn.astype(BF16)

    def seg(i):
        return jnp.dot(xb, w_ref[:, i * D_MODEL:(i + 1) * D_MODEL], preferred_element_type=F32)

    return seg


def _inproj_prompt_kernel(x_ref, sh_ref, sc_ref, gmix_ref, w_ref, qg_ref, kg_ref, cos_ref, sin_ref, lbl_ref,
                          qhm_ref, ktok_ref, khm_ref, vtok_ref, vhm_ref,
                          qh_ref, kh_ref, gh_ref, vh_ref, og_ref, ga_ref, gb_ref):
    seg = _inproj_common(x_ref[0], sh_ref[0], sc_ref[0], gmix_ref, w_ref)
    cos = cos_ref[...]
    sin = sin_ref[...]
    lane = lax.broadcasted_iota(jnp.int32, (1, HEAD_W), 1)
    zq = seg(0)
    zk = seg(1)
    for h in range(N_HEADS):
        hs = slice(h * HEAD_W, (h + 1) * HEAD_W)
        q = _qk_norm_rope(zq[:, hs], qg_ref[...], cos, sin, lane) * Q_SCALE
        qhm_ref[0, h] = q.astype(BF16)
        k = _qk_norm_rope(zk[:, hs], kg_ref[...], cos, sin, lane)
        ktok_ref[0, :, hs] = k
        khm_ref[0, h] = k.astype(BF16)
    zv = seg(2)
    vtok_ref[0] = zv
    ones = jnp.ones((zv.shape[0], HEAD_W), BF16)
    for h in range(N_HEADS):
        vhm_ref[0, h, :, 0:HEAD_W] = zv[:, h * HEAD_W:(h + 1) * HEAD_W].astype(BF16)
        vhm_ref[0, h, :, HEAD_W:2 * HEAD_W] = ones
    qh_ref[0] = jax.nn.silu(seg(3))
    lb = _lower_bound(lbl_ref)
    f = lb + (1.0 - lb) * jax.nn.sigmoid(seg(4))
    kh_ref[0] = 1.0 - f
    gh_ref[0] = jnp.log(f)
    vh_ref[0] = seg(5)
    og_ref[0] = jax.nn.silu(seg(6))
    ga_ref[0] = jax.nn.sigmoid(seg(7))
    gb_ref[0] = jax.nn.sigmoid(seg(8))


def _inproj_prompt(x, sh, sc, gmix, w_in, qg, kg, cos, sin, lbl, tm=256):
    b, t, _ = x.shape
    tok = jax.ShapeDtypeStruct((b, t, D_MODEL), F32)
    hm = jax.ShapeDtypeStruct((b, N_HEADS, t, HEAD_W), BF16)
    hm2 = jax.ShapeDtypeStruct((b, N_HEADS, t, 2 * HEAD_W), BF16)
    tok_spec = pl.BlockSpec((1, tm, D_MODEL), lambda i, j: (i, j, 0))
    hm_spec = pl.BlockSpec((1, N_HEADS, tm, HEAD_W), lambda i, j: (i, 0, j, 0))
    hm2_spec = pl.BlockSpec((1, N_HEADS, tm, 2 * HEAD_W), lambda i, j: (i, 0, j, 0))
    mod_spec = pl.BlockSpec((1, 1, D_MODEL), lambda i, j: (i, 0, 0))
    tab_spec = pl.BlockSpec((tm, HEAD_W), lambda i, j: (j, 0))
    return pl.pallas_call(
        _inproj_prompt_kernel,
        out_shape=(hm, tok, hm, tok, hm2, tok, tok, tok, tok, tok, tok, tok),
        grid=(b, t // tm),
        in_specs=[tok_spec, mod_spec, mod_spec, _const_spec((1, D_MODEL)), _const_spec(w_in.shape),
                  _const_spec((1, HEAD_W)), _const_spec((1, HEAD_W)), tab_spec, tab_spec, _const_spec(lbl.shape)],
        out_specs=(hm_spec, tok_spec, hm_spec, tok_spec, hm2_spec,
                   tok_spec, tok_spec, tok_spec, tok_spec, tok_spec, tok_spec, tok_spec),
        compiler_params=_params(("parallel", "parallel")),
        name="inproj_prompt",
    )(x, sh, sc, gmix, w_in, qg, kg, cos, sin, lbl)


def _inproj_sample_kernel(x_ref, sh_ref, sc_ref, gmix_ref, w_ref, qg_ref, kg_ref, cos_ref, sin_ref, lbl_ref,
                          q_ref, k_ref, v_ref, qt_ref, ft_ref, vh_ref, og_ref, ga_ref, gb_ref):
    seg = _inproj_common(x_ref[...], sh_ref[...], sc_ref[...], gmix_ref, w_ref)
    cos = cos_ref[...]
    sin = sin_ref[...]
    lane = lax.broadcasted_iota(jnp.int32, (1, HEAD_W), 1)
    zq = seg(0)
    zk = seg(1)
    for h in range(N_HEADS):
        hs = slice(h * HEAD_W, (h + 1) * HEAD_W)
        q_ref[:, hs] = _qk_norm_rope(zq[:, hs], qg_ref[...], cos, sin, lane) * Q_SCALE
        k_ref[:, hs] = _qk_norm_rope(zk[:, hs], kg_ref[...], cos, sin, lane)
    v_ref[...] = seg(2)
    qh = jax.nn.silu(seg(3))
    lb = _lower_bound(lbl_ref)
    f = lb + (1.0 - lb) * jax.nn.sigmoid(seg(4))
    for h in range(N_HEADS):
        hs = slice(h * HEAD_W, (h + 1) * HEAD_W)
        qt_ref[hs, :] = qh[:, hs].T
        ft_ref[hs, :] = f[:, hs].T
    vh_ref[...] = seg(5)
    og_ref[...] = jax.nn.silu(seg(6))
    ga_ref[...] = jax.nn.sigmoid(seg(7))
    gb_ref[...] = jax.nn.sigmoid(seg(8))


def _inproj_sample(x, sh, sc, gmix, w_in, qg, kg, cos, sin, lbl):
    m = x.shape[0]
    tok = jax.ShapeDtypeStruct((m, D_MODEL), F32)
    chan = jax.ShapeDtypeStruct((D_MODEL, m), F32)
    ts = _const_spec((m, D_MODEL))
    cs = _const_spec((D_MODEL, m))
    return pl.pallas_call(
        _inproj_sample_kernel,
        out_shape=(tok, tok, tok, chan, chan, tok, tok, tok, tok),
        grid=(1,),
        in_specs=[ts, ts, ts, _const_spec((1, D_MODEL)), _const_spec(w_in.shape),
                  _const_spec((1, HEAD_W)), _const_spec((1, HEAD_W)), _const_spec((1, HEAD_W)),
                  _const_spec((1, HEAD_W)), _const_spec(lbl.shape)],
        out_specs=(ts, ts, ts, cs, cs, ts, ts, ts, ts),
        compiler_params=_params(("arbitrary",)),
        name="inproj_sample",
    )(x, sh, sc, gmix, w_in, qg, kg, cos, sin, lbl)


def _hgrn_boundary_rows(cum, h):
    c, d = cum.shape
    if 2 * h >= 8:
        grp = 2 * h
        return jnp.concatenate(
            [jnp.broadcast_to(cum[i * grp + h - 1:i * grp + h, :], (grp, d)) for i in range(c // grp)], axis=0)
    cum3 = cum.reshape(c // 8, 8, d)
    sub = lax.broadcasted_iota(jnp.int32, (c // 8, 8, d), 1)
    out = None
    for first in range(0, 8, 2 * h):
        row = jnp.broadcast_to(cum3[:, first + h - 1:first + h, :], cum3.shape)
        out = row if out is None else jnp.where(sub >= first, row, out)
    return out.reshape(c, d)


def _hgrn_prompt_kernel(ltri_ref, q_ref, k_ref, g_ref, v_ref, og_ref, gn_ref, ob_ref, sout_ref, st_ref, *, ct):
    c = HGRN_CHUNK
    tstep = pl.program_id(1)

    @pl.when(tstep == 0)
    def _():
        st_ref[...] = jnp.zeros_like(st_ref)

    tt = lax.broadcasted_iota(jnp.int32, (c, c), 0)
    ss = lax.broadcasted_iota(jnp.int32, (c, c), 1)
    meet = [((tt // (2 * h)) == (ss // (2 * h))) & (((tt // h) % 2) == 1) & (((ss // h) % 2) == 0)
            for h in HGRN_LEVELS]
    eye = tt == ss
    ltri = ltri_ref[...]
    gn = gn_ref[...]

    def chunk(ci, carry):
        rows = pl.ds(pl.multiple_of(ci * c, c), c)
        g = g_ref[0, rows, :]
        g_hi = g.astype(BF16)
        r1 = g - g_hi.astype(F32)
        g_mid = r1.astype(BF16)
        g_lo = (r1 - g_mid.astype(F32)).astype(BF16)
        cum = (jnp.dot(ltri, g_hi, preferred_element_type=F32)
               + jnp.dot(ltri, g_mid, preferred_element_type=F32)
               + jnp.dot(ltri, g_lo, preferred_element_type=F32))
        q = q_ref[0, rows, :]
        k = k_ref[0, rows, :]
        v = v_ref[0, rows, :]
        og = og_ref[0, rows, :]
        q_in = q * jnp.exp(cum)
        k_out = k * jnp.exp(-jnp.abs(cum - cum[c - 1:c, :]))
        d_all = jnp.exp(cum[c - 1:c, :])
        ex = [jnp.exp(-jnp.abs(cum - _hgrn_boundary_rows(cum, h))) for h in HGRN_LEVELS]
        for h in range(N_HEADS):
            hs = slice(h * HEAD_W, (h + 1) * HEAD_W)
            qh = q[:, hs]
            kh = k[:, hs]
            a = jnp.where(eye, jnp.sum(qh * kh, axis=-1, keepdims=True), 0.0)
            for li in range(len(HGRN_LEVELS)):
                e = ex[li][:, hs]
                al = lax.dot_general((qh * e).astype(BF16), (kh * e).astype(BF16), NT_DIMS,
                                     preferred_element_type=F32)
                a = a + jnp.where(meet[li], al, 0.0)
            vb = v[:, hs].astype(BF16)
            st = st_ref[h]
            o = (jnp.dot(a.astype(BF16), vb, preferred_element_type=F32)
                 + lax.dot_general(q_in[:, hs].astype(BF16), st.astype(BF16), NT_DIMS, preferred_element_type=F32))
            st_ref[h] = d_all[:, hs] * st + lax.dot_general(vb, k_out[:, hs].astype(BF16), TN_DIMS,
                                                            preferred_element_type=F32)
            ob_ref[0, rows, hs] = (_rms(o, gn) * og[:, hs]).astype(BF16)
        return carry

    lax.fori_loop(0, ct // c, chunk, 0)

    @pl.when(tstep == pl.num_programs(1) - 1)
    def _():
        for h in range(N_HEADS):
            sout_ref[0, h] = st_ref[h].T


def _hgrn_prompt(qh, kh, gh, vh, og, gn, ct=256):
    b, t, _ = qh.shape
    dall = jnp.asarray(np.tril(np.ones((HGRN_CHUNK, HGRN_CHUNK), np.float32)), dtype=BF16)
    tok_spec = pl.BlockSpec((1, ct, D_MODEL), lambda i, j: (i, j, 0))
    return pl.pallas_call(
        functools.partial(_hgrn_prompt_kernel, ct=ct),
        out_shape=(jax.ShapeDtypeStruct((b, t, D_MODEL), BF16),
                   jax.ShapeDtypeStruct((b, N_HEADS, HEAD_W, HEAD_W), F32)),
        grid=(b, t // ct),
        in_specs=[_const_spec(dall.shape), tok_spec, tok_spec, tok_spec, tok_spec, tok_spec, _const_spec((1, HEAD_W))],
        out_specs=(tok_spec, pl.BlockSpec((1, N_HEADS, HEAD_W, HEAD_W), lambda i, j: (i, 0, 0, 0))),
        scratch_shapes=[pltpu.VMEM((N_HEADS, HEAD_W, HEAD_W), F32)],
        compiler_params=_params(("parallel", "arbitrary")),
        name="hgrn_prompt",
    )(dall, qh, kh, gh, vh, og, gn)


def _lambda(lam_ref, lam_init):
    lv = lam_ref[...]
    s1 = jnp.sum(lv[0:1] * lv[1:2], axis=-1, keepdims=True)
    s2 = jnp.sum(lv[2:3] * lv[3:4], axis=-1, keepdims=True)
    return jnp.exp(s1) - jnp.exp(s2) + lam_init


def _attn_prompt_kernel(lam_ref, g_ref, q_ref, k_ref, v_ref, o_ref, *, tq, tk, lam_init):
    qi = pl.program_id(2)
    q = q_ref[0, 0]
    lane = lax.broadcasted_iota(jnp.int32, (1, HEAD_W), 1)
    zero = jnp.zeros_like(q)
    qs = jnp.concatenate([jnp.where(lane < A_DH, q, zero), jnp.where(lane < A_DH, zero, q)], axis=0)
    rep = tk // HEAD_W

    def scores(kb):
        rows = pl.ds(pl.multiple_of(kb * tk, tk), tk)
        return lax.dot_general(qs, k_ref[0, 0, rows, :], NT_DIMS, preferred_element_type=F32)

    def consume(s, kb, m, acc, masked):
        if masked:
            r = lax.broadcasted_iota(jnp.int32, (2 * tq, tk), 0)
            qpos = qi * tq + jnp.where(r >= tq, r - tq, r)
            kpos = kb * tk + lax.broadcasted_iota(jnp.int32, (2 * tq, tk), 1)
            s = jnp.where(kpos <= qpos, s, NEG_BIG)
        m_new = jnp.maximum(m, jnp.max(s, axis=-1, keepdims=True))
        alpha = jnp.exp2(m - m_new)
        p = jnp.exp2(s - jnp.tile(m_new, (1, rep)))
        rows = pl.ds(pl.multiple_of(kb * tk, tk), tk)
        pv = jnp.dot(p.astype(BF16), v_ref[0, 0, rows, :], preferred_element_type=F32)
        return m_new, jnp.tile(alpha, (1, 2)) * acc + pv

    nfull = qi * (tq // tk)
    ndiag = tq // tk

    def body(kb, carry):
        s, m, acc = carry
        s_next = scores(kb + 1)
        m, acc = consume(s, kb, m, acc, False)
        return s_next, m, acc

    s, m, acc = lax.fori_loop(0, nfull, body, (scores(0), jnp.full((2 * tq, HEAD_W), NEG_BIG, F32),
                                               jnp.zeros((2 * tq, 2 * HEAD_W), F32)))
    for j in range(ndiag):
        s_next = scores(nfull + j + 1) if j + 1 < ndiag else None
        m, acc = consume(s, nfull + j, m, acc, True)
        s = s_next
    o = acc[:, 0:HEAD_W] / acc[:, HEAD_W:2 * HEAD_W]
    lam = _lambda(lam_ref, lam_init)
    od = o[0:tq] - lam * o[tq:2 * tq]
    o_ref[0, 0] = _rms(od, g_ref[...]).astype(BF16)


def _attn_prompt(lam4, g_sub, q_hm, k_hm, v_hm, lam_init, tq=512, tk=512):
    b, nh, t, _ = q_hm.shape
    k_spec = pl.BlockSpec((1, 1, t, HEAD_W), lambda i, h, j: (i, h, 0, 0))
    v_spec = pl.BlockSpec((1, 1, t, 2 * HEAD_W), lambda i, h, j: (i, h, 0, 0))
    q_spec = pl.BlockSpec((1, 1, tq, HEAD_W), lambda i, h, j: (i, h, j, 0))
    return pl.pallas_call(
        functools.partial(_attn_prompt_kernel, tq=tq, tk=tk, lam_init=lam_init),
        out_shape=jax.ShapeDtypeStruct((b, nh, t, HEAD_W), BF16),
        grid=(b, nh, t // tq),
        in_specs=[_const_spec(lam4.shape), _const_spec((1, HEAD_W)), q_spec, k_spec, v_spec],
        out_specs=q_spec,
        compiler_params=_params(("parallel", "parallel", "arbitrary")),
        name="attn_prompt",
    )(lam4, g_sub, q_hm, k_hm, v_hm)


def _mix_kernel(x_ref, gm_ref, oa_ref, ob_ref, ga_ref, gb_ref, wa_ref, wb_ref, wo_ref, o_ref):
    oa = jnp.concatenate([oa_ref[0, h] for h in range(N_HEADS)], axis=-1)
    ya = jnp.dot(oa, wa_ref[...], preferred_element_type=F32)
    yb = jnp.dot(ob_ref[0].astype(BF16), wb_ref[...], preferred_element_type=F32)
    merged = ga_ref[0] * ya + gb_ref[0] * yb
    mix = jnp.dot(merged.astype(BF16), wo_ref[...], preferred_element_type=F32)
    o_ref[0] = x_ref[0] + gm_ref[0] * mix


def _mix(x, gm, oa_hm, ob, ga, gb, wa, wb, wo, tm):
    b, t, _ = x.shape
    rows_mod = gm.shape[1]
    tok_spec = pl.BlockSpec((1, tm, D_MODEL), lambda i, j: (i, j, 0))
    mod_spec = (pl.BlockSpec((1, 1, D_MODEL), lambda i, j: (i, 0, 0)) if rows_mod == 1 else tok_spec)
    return pl.pallas_call(
        _mix_kernel,
        out_shape=jax.ShapeDtypeStruct((b, t, D_MODEL), F32),
        grid=(b, t // tm),
        in_specs=[tok_spec, mod_spec, pl.BlockSpec((1, N_HEADS, tm, HEAD_W), lambda i, j: (i, 0, j, 0)),
                  tok_spec, tok_spec, tok_spec,
                  _const_spec(wa.shape), _const_spec(wb.shape), _const_spec(wo.shape)],
        out_specs=tok_spec,
        compiler_params=_params(("parallel", "parallel")),
        name="mix_out",
    )(x, gm, oa_hm, ob, ga, gb, wa, wb, wo)


def _ffn_kernel(*refs, tm, carried):
    if carried:
        (x_ref, sh_ref, sc_ref, gt_ref, g_ref, wup_ref, cw_ref, cb_ref, wdn_ref, y_ref, tail_ref, prev_ref) = refs
    else:
        (x_ref, sh_ref, sc_ref, gt_ref, g_ref, wup_ref, cw_ref, cb_ref, wdn_ref, b0_ref, b1_ref, y_ref, u_ref) = refs
    x = x_ref[0]
    hn = (_rms(x, g_ref[...]) * (1.0 + sc_ref[0]) + sh_ref[0]).astype(BF16)
    if carried:
        @pl.when(pl.program_id(1) == 0)
        def _():
            prev_ref[...] = jnp.zeros_like(prev_ref)
        row = lax.broadcasted_iota(jnp.int32, (tm, 1), 0)

    def conv(u, cols):
        if carried:
            p6 = prev_ref[6:7, cols]
            p7 = prev_ref[7:8, cols]
            u1 = jnp.where(row == 0, p7, pltpu.roll(u, 1, 0))
            u2 = jnp.where(row == 0, p6, jnp.where(row == 1, p7, pltpu.roll(u, 2, 0)))
            prev_ref[:, cols] = u[tm - 8:tm, :]
            tail_ref[0, :, cols] = u[tm - 8:tm, :]
        else:
            u2 = b0_ref[0, :, cols]
            u1 = b1_ref[0, :, cols]
            u_ref[0, :, cols] = u
        return cb_ref[:, cols] + cw_ref[0:1, cols] * u2 + cw_ref[1:2, cols] * u1 + cw_ref[2:3, cols] * u

    acc = jnp.zeros((tm, D_MODEL), F32)
    for c in range(D_FF // FFN_COLS):
        ca = slice(c * FFN_COLS, (c + 1) * FFN_COLS)
        cg = slice(D_FF + c * FFN_COLS, D_FF + (c + 1) * FFN_COLS)
        ya = conv(jnp.dot(hn, wup_ref[:, ca], preferred_element_type=F32), ca)
        yg = conv(jnp.dot(hn, wup_ref[:, cg], preferred_element_type=F32), cg)
        act = (jax.nn.silu(ya) * yg).astype(BF16)
        acc = acc + jnp.dot(act, wdn_ref[ca, :], preferred_element_type=F32)
    y_ref[0] = x + gt_ref[0] * acc


def _ffn(x1, sh, sc, gt, g, wup, cw, cb, wdn, tm, bufs=None):
    b, t, _ = x1.shape
    carried = bufs is None
    tok_spec = pl.BlockSpec((1, tm, D_MODEL), lambda i, j: (i, j, 0))
    mod_spec = (pl.BlockSpec((1, 1, D_MODEL), lambda i, j: (i, 0, 0)) if sh.shape[1] == 1 else tok_spec)
    wide_spec = pl.BlockSpec((1, tm, 2 * D_FF), lambda i, j: (i, j, 0))
    in_specs = [tok_spec, mod_spec, mod_spec, mod_spec, _const_spec((1, D_MODEL)), _const_spec(wup.shape),
                _const_spec(cw.shape), _const_spec(cb.shape), _const_spec(wdn.shape)]
    args = [x1, sh, sc, gt, g, wup, cw, cb, wdn]
    if carried:
        out_shape = (jax.ShapeDtypeStruct((b, t, D_MODEL), F32), jax.ShapeDtypeStruct((b, 8, 2 * D_FF), F32))
        out_specs = (tok_spec, pl.BlockSpec((1, 8, 2 * D_FF), lambda i, j: (i, 0, 0)))
        scratch = [pltpu.VMEM((8, 2 * D_FF), F32)]
        sem = ("parallel", "arbitrary")
    else:
        in_specs += [wide_spec, wide_spec]
        args += list(bufs)
        out_shape = (jax.ShapeDtypeStruct((b, t, D_MODEL), F32), jax.ShapeDtypeStruct((b, t, 2 * D_FF), F32))
        out_specs = (tok_spec, wide_spec)
        scratch = []
        sem = ("parallel", "parallel")
    return pl.pallas_call(
        functools.partial(_ffn_kernel, tm=tm, carried=carried),
        out_shape=out_shape,
        grid=(b, t // tm),
        in_specs=in_specs,
        out_specs=out_specs,
        scratch_shapes=scratch,
        compiler_params=_params(sem),
        name="conv_ffn",
    )(*args)


def _hgrn_sample_kernel(ft_ref, qt_ref, v_ref, og_ref, gn_ref, s_ref, so_ref, o_ref, *, nb):
    kc = pl.program_id(1)

    @pl.when(kc == 0)
    def _():
        o_ref[...] = jnp.zeros_like(o_ref)

    ft = ft_ref[...]
    qt = qt_ref[...]
    for b in range(nb):
        f = ft[:, b:b + 1]
        s_new = f * s_ref[b, 0] + (1.0 - f) * v_ref[b:b + 1, :]
        so_ref[b, 0] = s_new
        o_ref[b:b + 1, :] += jnp.sum(qt[:, b:b + 1] * s_new, axis=0, keepdims=True)

    @pl.when(kc == pl.num_programs(1) - 1)
    def _():
        o_ref[...] = _rms(o_ref[...], gn_ref[...]) * og_ref[...]


def _hgrn_sample(ft, qt, vh, og, gn, state, rows=16):
    nb = vh.shape[0]
    nkc = HEAD_W // rows
    st_spec = pl.BlockSpec((nb, 1, rows, HEAD_W), lambda h, c: (0, h, c, 0))
    ch_spec = pl.BlockSpec((rows, nb), lambda h, c: (h * nkc + c, 0))
    hd_spec = pl.BlockSpec((nb, HEAD_W), lambda h, c: (0, h))
    return pl.pallas_call(
        functools.partial(_hgrn_sample_kernel, nb=nb),
        out_shape=(jax.ShapeDtypeStruct(state.shape, F32), jax.ShapeDtypeStruct((nb, D_MODEL), F32)),
        grid=(N_HEADS, nkc),
        in_specs=[ch_spec, ch_spec, hd_spec, hd_spec, _const_spec((1, HEAD_W)), st_spec],
        out_specs=(st_spec, hd_spec),
        compiler_params=_params(("parallel", "arbitrary")),
        name="hgrn_sample",
    )(ft, qt, vh, og, gn, state)


def _attn_sample_kernel(pt_ref, lam_ref, g_ref, q_ref, kn_ref, vn_ref, ck_hbm, cv_hbm, o_ref,
                        kbuf, vbuf, sem, *, n_pages, page, lam_init):
    b = pl.program_id(0)
    nb = pl.num_programs(0)
    slot = b % 2

    def copies(bb, sl):
        out = []
        for p in range(n_pages):
            pg = pt_ref[bb, p]
            out.append(pltpu.make_async_copy(ck_hbm.at[pg], kbuf.at[sl, p], sem.at[0, sl]))
            out.append(pltpu.make_async_copy(cv_hbm.at[pg], vbuf.at[sl, p], sem.at[1, sl]))
        return out

    @pl.when(b == 0)
    def _():
        for cp in copies(0, 0):
            cp.start()

    @pl.when(b + 1 < nb)
    def _():
        for cp in copies(b + 1, 1 - slot):
            cp.start()

    for cp in copies(b, slot):
        cp.wait()

    q = q_ref[pl.ds(b, 1), :]
    r16 = lax.broadcasted_iota(jnp.int32, (2 * N_HEADS, D_MODEL), 0)
    ln = lax.broadcasted_iota(jnp.int32, (2 * N_HEADS, D_MODEL), 1)
    sel = ((ln // HEAD_W) == (r16 % N_HEADS)) & (((ln // A_DH) % 2) == (r16 // N_HEADS))
    qsel = jnp.where(sel, q, 0.0)
    qblk = qsel.astype(BF16)
    s = jnp.concatenate([jnp.dot(qblk, kbuf[slot, p].astype(BF16), preferred_element_type=F32)
                         for p in range(n_pages)], axis=1)
    s_new = jnp.sum(qsel * kn_ref[pl.ds(b, 1), :], axis=-1, keepdims=True)
    m = jnp.maximum(jnp.max(s, axis=-1, keepdims=True), s_new)
    e = jnp.exp2(s - m)
    e_new = jnp.exp2(s_new - m)
    inv_l = 1.0 / (jnp.sum(e, axis=-1, keepdims=True) + e_new)
    lam = _lambda(lam_ref, lam_init)
    pr = e * inv_l
    pr_new = e_new * inv_l
    pd = pr[0:N_HEADS] - lam * pr[N_HEADS:2 * N_HEADS]
    pd = jnp.concatenate([pd, jnp.zeros_like(pd)], axis=0).astype(BF16)
    pd_new = pr_new[0:N_HEADS] - lam * pr_new[N_HEADS:2 * N_HEADS]
    vn = vn_ref[pl.ds(b, 1), :]
    for h in range(N_HEADS):
        r = jnp.zeros((2 * N_HEADS, HEAD_W), F32)
        for p in range(n_pages):
            r = r + jnp.dot(pd[:, p * page:(p + 1) * page], vbuf[slot, p, :, h, :].astype(BF16),
                            preferred_element_type=F32)
        oh = r[h:h + 1, :] + pd_new[h:h + 1, :] * vn[:, h * HEAD_W:(h + 1) * HEAD_W]
        o_ref[h, pl.ds(b, 1), :] = _rms(oh, g_ref[...])


def _attn_sample(page_table, lam4, g_sub, q, k_new, v_new, cache_kt, cache_v, lam_init):
    nb, n_pages = page_table.shape
    page = cache_v.shape[1]
    full = lambda shape: pl.BlockSpec(shape, lambda i, pt: (0,) * len(shape))
    any_spec = pl.BlockSpec(memory_space=pl.ANY)
    grid_spec = pltpu.PrefetchScalarGridSpec(
        num_scalar_prefetch=1,
        grid=(nb,),
        in_specs=[full(lam4.shape), full((1, HEAD_W)), full(q.shape), full(q.shape), full(q.shape),
                  any_spec, any_spec],
        out_specs=full((N_HEADS, nb, HEAD_W)),
        scratch_shapes=[pltpu.VMEM((2, n_pages, D_MODEL, page), F32),
                        pltpu.VMEM((2, n_pages, page, N_HEADS, HEAD_W), F32),
                        pltpu.SemaphoreType.DMA((2, 2))],
    )
    return pl.pallas_call(
        functools.partial(_attn_sample_kernel, n_pages=n_pages, page=page, lam_init=lam_init),
        out_shape=jax.ShapeDtypeStruct((N_HEADS, nb, HEAD_W), F32),
        grid_spec=grid_spec,
        compiler_params=_params(("arbitrary",)),
        name="attn_sample",
    )(page_table, lam4, g_sub, q, k_new, v_new, cache_kt, cache_v)


def _rope_tables(pos):
    half = A_DH // 2
    inv = ROPE_THETA ** (-jnp.arange(half, dtype=F32) / half)
    ang = pos.astype(F32)[:, None] * inv[None, :]
    cos = jnp.cos(ang)
    sin = jnp.sin(ang)
    return jnp.tile(cos, (1, 4)), jnp.concatenate([-sin, sin, -sin, sin], axis=-1)


def kernel(x_prompt, x_sample, cache_k, cache_v, state_hgrn, state_conv, page_table, c_prompt, c_sample, w_ada, b_ada,
           rms_mix_g, w_in, q_norm_g, k_norm_g, lambda_q1, lambda_k1, lambda_q2, lambda_k2, subln_g, lb_logits,
           hgrn_norm_g, w_branch_a, w_branch_b, w_out, rms_ffn_g, w_up, conv_w, conv_b, w_down):
    bp, tp, d = x_prompt.shape
    bs, ts, _ = x_sample.shape
    assert d == D_MODEL and ts == 1 and w_in.shape[0] == 1
    n_pool, page = cache_k.shape[1], cache_k.shape[2]
    past_len = page_table.shape[1] * page
    lam_init = 0.8 - 0.6 * math.exp(-0.3 * 0)

    pad = (-(bp + bs)) % 8
    c_all = jnp.concatenate([c_prompt, c_sample, jnp.zeros((pad, d), F32)], axis=0)
    mod = _ada(c_all, w_ada[0], b_ada[0].reshape(1, -1))
    mp = [mod[:bp, i * d:(i + 1) * d].reshape(bp, 1, d) for i in range(6)]
    msamp = [mod[bp:bp + bs, i * d:(i + 1) * d] for i in range(6)]

    w_in_b = w_in[0].astype(BF16)
    wa_b = w_branch_a[0].astype(BF16)
    wb_b = w_branch_b[0].astype(BF16)
    wo_b = w_out[0].astype(BF16)
    wup_b = w_up[0].astype(BF16)
    wdn_b = w_down[0].astype(BF16)
    gmix = rms_mix_g[0].reshape(1, d)
    gffn = rms_ffn_g[0].reshape(1, d)
    qg = jnp.tile(q_norm_g[0], 2).reshape(1, HEAD_W)
    kg = jnp.tile(k_norm_g[0], 2).reshape(1, HEAD_W)
    lam4 = jnp.stack([lambda_q1[0], lambda_k1[0], lambda_q2[0], lambda_k2[0]], axis=0)
    g_sub = (subln_g[0] * (1.0 - lam_init)).reshape(1, HEAD_W)
    gn = hgrn_norm_g[0].reshape(1, HEAD_W)
    cw = conv_w[0]
    cb = conv_b[0].reshape(1, -1)

    cos_p, sin_p = _rope_tables(jnp.arange(tp))
    (q_hm, k_tok, k_hm, v_tok, v_hm, qh, kh, gh, vh, og, ga, gb) = _inproj_prompt(
        x_prompt, mp[0], mp[1], gmix, w_in_b, qg, kg, cos_p, sin_p, lb_logits)
    ob, s_prompt = _hgrn_prompt(qh, kh, gh, vh, og, gn)
    oa_hm = _attn_prompt(lam4, g_sub, q_hm, k_hm, v_hm, lam_init)
    x1 = _mix(x_prompt, mp[2], oa_hm, ob, ga, gb, wa_b, wb_b, wo_b, tm=512)
    y_prompt, tail = _ffn(x1, mp[3], mp[4], mp[5], gffn, wup_b, cw, cb, wdn_b, tm=512)

    cos_s, sin_s = _rope_tables(jnp.full((1,), past_len))
    xs = x_sample.reshape(bs, d)
    (q_s, k_s, v_s, qt_s, ft_s, vh_s, og_s, ga_s, gb_s) = _inproj_sample(
        xs, msamp[0], msamp[1], gmix, w_in_b, qg, kg, cos_s, sin_s, lb_logits)
    s_sample, ob_s = _hgrn_sample(ft_s, qt_s, vh_s, og_s, gn, state_hgrn[0])
    cache_kt = jnp.transpose(cache_k[0], (0, 2, 3, 4, 1)).reshape(n_pool, d, page)
    oa_s = _attn_sample(page_table, lam4, g_sub, q_s, k_s, v_s, cache_kt, cache_v[0], lam_init)
    x1_s = _mix(xs[None], msamp[2][None], oa_s[None].astype(BF16), ob_s[None], ga_s[None], gb_s[None],
                wa_b, wb_b, wo_b, tm=bs)
    y_s, u_s = _ffn(x1_s, msamp[3][None], msamp[4][None], msamp[5][None], gffn, wup_b, cw, cb, wdn_b, tm=bs,
                    bufs=(state_conv[0][:, 0][None], state_conv[0][:, 1][None]))

    new_conv_s = jnp.stack([state_conv[0][:, 1], u_s[0]], axis=1)
    return (y_prompt, y_s.reshape(bs, 1, d),
            k_tok.reshape(1, bp, tp, N_HEADS, 2, A_DH), v_tok.reshape(1, bp, tp, N_HEADS, HEAD_W),
            s_prompt[None], tail[:, 6:8][None],
            k_s.reshape(1, bs, 1, N_HEADS, 2, A_DH), v_s.reshape(1, bs, 1, N_HEADS, HEAD_W),
            s_sample[None], new_conv_s[None])
```

```python
import functools
import math

import numpy as np
import jax
import jax.numpy as jnp
from jax import lax
from jax.experimental import pallas as pl
from jax.experimental.pallas import tpu as pltpu

F32 = jnp.float32
BF16 = jnp.bfloat16

D_MODEL = 1024
N_HEADS = 8
HEAD_W = 128
A_DH = 64
D_FF = 2816
N_SEG = 9
ROPE_THETA = 10000.0
EPS = 1e-6
NEG_BIG = -1e30
HGRN_CHUNK = 64
HGRN_LEVELS = (32, 16, 8, 4, 2, 1)
FFN_COLS = 256
VMEM_LIMIT = 56 * 1024 * 1024
Q_SCALE = (A_DH ** -0.5) * math.log2(math.e)

N_ACC = 8
ATTN_ROW_GROUP = 256
NT_DIMS = (((1,), (1,)), ((), ()))
TN_DIMS = (((0,), (0,)), ((), ()))


def _const_spec(shape):
    nd = len(shape)
    return pl.BlockSpec(shape, lambda *_: (0,) * nd, pipeline_mode=pl.Buffered(1))


def _params(sem, vmem=VMEM_LIMIT):
    return pltpu.CompilerParams(dimension_semantics=sem, vmem_limit_bytes=vmem)


def _rms(x, g):
    ms = jnp.mean(x * x, axis=-1, keepdims=True)
    return (x * lax.rsqrt(ms + EPS)) * g


def _ada_kernel(c_ref, w_ref, b_ref, o_ref):
    a = jax.nn.silu(c_ref[...]).astype(BF16)
    o_ref[...] = jnp.dot(a, w_ref[...].astype(BF16), preferred_element_type=F32) + b_ref[...]


def _ada(c_all, w_ada, b_ada):
    m = c_all.shape[0]
    n = w_ada.shape[1]
    tn = 1536
    return pl.pallas_call(
        _ada_kernel,
        out_shape=jax.ShapeDtypeStruct((m, n), F32),
        grid=(n // tn,),
        in_specs=[
            _const_spec((m, D_MODEL)),
            pl.BlockSpec((D_MODEL, tn), lambda j: (0, j)),
            pl.BlockSpec((1, tn), lambda j: (0, j)),
        ],
        out_specs=pl.BlockSpec((m, tn), lambda j: (0, j)),
        compiler_params=_params(("arbitrary",)),
        name="ada_mod",
    )(c_all, w_ada, b_ada)


def _qk_norm_rope(z, g128, cos, sin, lane):
    sq = z * z
    first = lane < A_DH
    lo = jnp.sum(jnp.where(first, sq, 0.0), axis=-1, keepdims=True)
    al = jnp.sum(sq, axis=-1, keepdims=True)
    ms = jnp.where(first, lo, al - lo) * (1.0 / A_DH)
    zn = (z * lax.rsqrt(ms + EPS)) * g128
    rot = jnp.where((lane % A_DH) < A_DH // 2, pltpu.roll(zn, HEAD_W - A_DH // 2, 1), pltpu.roll(zn, A_DH // 2, 1))
    return zn * cos + rot * sin


def _lower_bound(lbl_ref):
    lbl = lbl_ref[...]
    mx = jnp.max(lbl, axis=0, keepdims=True)
    e = jnp.exp(lbl - mx)
    return e[0:1, :] / jnp.sum(e, axis=0, keepdims=True)


def _inproj_common(x, sh, sc, gmix_ref, w_ref):
    xn = _rms(x, gmix_ref[...]) * (1.0 + sc) + sh
    xb = xn.astype(BF16)

    def seg(i):
        return jnp.dot(xb, w_ref[:, i * D_MODEL:(i + 1) * D_MODEL], preferred_element_type=F32)

    return seg


def _inproj_prompt_kernel(x_ref, sh_ref, sc_ref, gmix_ref, w_ref, qg_ref, kg_ref, cos_ref, sin_ref, lbl_ref,
                          qhm_ref, ktok_ref, khm_ref, vtok_ref, vhm_ref,
                          qh_ref, kh_ref, gh_ref, vh_ref, og_ref, ga_ref, gb_ref):
    seg = _inproj_common(x_ref[0], sh_ref[0], sc_ref[0], gmix_ref, w_ref)
    cos = cos_ref[...]
    sin = sin_ref[...]
    lane = lax.broadcasted_iota(jnp.int32, (1, HEAD_W), 1)
    zq = seg(0)
    zk = seg(1)
    for h in range(N_HEADS):
        hs = slice(h * HEAD_W, (h + 1) * HEAD_W)
        q = _qk_norm_rope(zq[:, hs], qg_ref[...], cos, sin, lane) * Q_SCALE
        qhm_ref[0, h] = q.astype(BF16)
        k = _qk_norm_rope(zk[:, hs], kg_ref[...], cos, sin, lane)
        ktok_ref[0, :, hs] = k
        khm_ref[0, h] = k.astype(BF16)
    zv = seg(2)
    vtok_ref[0] = zv
    ones = jnp.ones((zv.shape[0], HEAD_W), BF16)
    for h in range(N_HEADS):
        vhm_ref[0, h, :, 0:HEAD_W] = zv[:, h * HEAD_W:(h + 1) * HEAD_W].astype(BF16)
        vhm_ref[0, h, :, HEAD_W:2 * HEAD_W] = ones
    qh_ref[0] = jax.nn.silu(seg(3))
    lb = _lower_bound(lbl_ref)
    f = lb + (1.0 - lb) * jax.nn.sigmoid(seg(4))
    kh_ref[0] = 1.0 - f
    gh_ref[0] = jnp.log(f)
    vh_ref[0] = seg(5)
    og_ref[0] = jax.nn.silu(seg(6))
    ga_ref[0] = jax.nn.sigmoid(seg(7))
    gb_ref[0] = jax.nn.sigmoid(seg(8))


def _inproj_prompt(x, sh, sc, gmix, w_in, qg, kg, cos, sin, lbl, tm=256):
    b, t, _ = x.shape
    tok = jax.ShapeDtypeStruct((b, t, D_MODEL), F32)
    hm = jax.ShapeDtypeStruct((b, N_HEADS, t, HEAD_W), BF16)
    hm2 = jax.ShapeDtypeStruct((b, N_HEADS, t, 2 * HEAD_W), BF16)
    tok_spec = pl.BlockSpec((1, tm, D_MODEL), lambda i, j: (i, j, 0))
    hm_spec = pl.BlockSpec((1, N_HEADS, tm, HEAD_W), lambda i, j: (i, 0, j, 0))
    hm2_spec = pl.BlockSpec((1, N_HEADS, tm, 2 * HEAD_W), lambda i, j: (i, 0, j, 0))
    mod_spec = pl.BlockSpec((1, 1, D_MODEL), lambda i, j: (i, 0, 0))
    tab_spec = pl.BlockSpec((tm, HEAD_W), lambda i, j: (j, 0))
    return pl.pallas_call(
        _inproj_prompt_kernel,
        out_shape=(hm, tok, hm, tok, hm2, tok, tok, tok, tok, tok, tok, tok),
        grid=(b, t // tm),
        in_specs=[tok_spec, mod_spec, mod_spec, _const_spec((1, D_MODEL)), _const_spec(w_in.shape),
                  _const_spec((1, HEAD_W)), _const_spec((1, HEAD_W)), tab_spec, tab_spec, _const_spec(lbl.shape)],
        out_specs=(hm_spec, tok_spec, hm_spec, tok_spec, hm2_spec,
                   tok_spec, tok_spec, tok_spec, tok_spec, tok_spec, tok_spec, tok_spec),
        compiler_params=_params(("parallel", "parallel")),
        name="inproj_prompt",
    )(x, sh, sc, gmix, w_in, qg, kg, cos, sin, lbl)


def _inproj_sample_kernel(x_ref, sh_ref, sc_ref, gmix_ref, w_ref, qg_ref, kg_ref, cos_ref, sin_ref, lbl_ref,
                          q_ref, k_ref, v_ref, qt_ref, ft_ref, vh_ref, og_ref, ga_ref, gb_ref):
    seg = _inproj_common(x_ref[...], sh_ref[...], sc_ref[...], gmix_ref, w_ref)
    cos = cos_ref[...]
    sin = sin_ref[...]
    lane = lax.broadcasted_iota(jnp.int32, (1, HEAD_W), 1)
    zq = seg(0)
    zk = seg(1)
    for h in range(N_HEADS):
        hs = slice(h * HEAD_W, (h + 1) * HEAD_W)
        q_ref[:, hs] = _qk_norm_rope(zq[:, hs], qg_ref[...], cos, sin, lane) * Q_SCALE
        k_ref[:, hs] = _qk_norm_rope(zk[:, hs], kg_ref[...], cos, sin, lane)
    v_ref[...] = seg(2)
    qh = jax.nn.silu(seg(3))
    lb = _lower_bound(lbl_ref)
    f = lb + (1.0 - lb) * jax.nn.sigmoid(seg(4))
    for h in range(N_HEADS):
        hs = slice(h * HEAD_W, (h + 1) * HEAD_W)
        qt_ref[hs, :] = qh[:, hs].T
        ft_ref[hs, :] = f[:, hs].T
    vh_ref[...] = seg(5)
    og_ref[...] = jax.nn.silu(seg(6))
    ga_ref[...] = jax.nn.sigmoid(seg(7))
    gb_ref[...] = jax.nn.sigmoid(seg(8))


def _inproj_sample(x, sh, sc, gmix, w_in, qg, kg, cos, sin, lbl):
    m = x.shape[0]
    tok = jax.ShapeDtypeStruct((m, D_MODEL), F32)
    chan = jax.ShapeDtypeStruct((D_MODEL, m), F32)
    ts = _const_spec((m, D_MODEL))
    cs = _const_spec((D_MODEL, m))
    return pl.pallas_call(
        _inproj_sample_kernel,
        out_shape=(tok, tok, tok, chan, chan, tok, tok, tok, tok),
        grid=(1,),
        in_specs=[ts, ts, ts, _const_spec((1, D_MODEL)), _const_spec(w_in.shape),
                  _const_spec((1, HEAD_W)), _const_spec((1, HEAD_W)), _const_spec((1, HEAD_W)),
                  _const_spec((1, HEAD_W)), _const_spec(lbl.shape)],
        out_specs=(ts, ts, ts, cs, cs, ts, ts, ts, ts),
        compiler_params=_params(("arbitrary",)),
        name="inproj_sample",
    )(x, sh, sc, gmix, w_in, qg, kg, cos, sin, lbl)


def _hgrn_boundary_rows(cum, h):
    c, d = cum.shape
    if 2 * h >= 8:
        grp = 2 * h
        return jnp.concatenate(
            [jnp.broadcast_to(cum[i * grp + h - 1:i * grp + h, :], (grp, d)) for i in range(c // grp)], axis=0)
    cum3 = cum.reshape(c // 8, 8, d)
    sub = lax.broadcasted_iota(jnp.int32, (c // 8, 8, d), 1)
    out = None
    for first in range(0, 8, 2 * h):
        row = jnp.broadcast_to(cum3[:, first + h - 1:first + h, :], cum3.shape)
        out = row if out is None else jnp.where(sub >= first, row, out)
    return out.reshape(c, d)


def _hgrn_prompt_kernel(ltri_ref, q_ref, k_ref, g_ref, v_ref, og_ref, gn_ref, ob_ref, sout_ref, st_ref, *, ct):
    c = HGRN_CHUNK
    tstep = pl.program_id(1)

    @pl.when(tstep == 0)
    def _():
        st_ref[...] = jnp.zeros_like(st_ref)

    tt = lax.broadcasted_iota(jnp.int32, (c, c), 0)
    ss = lax.broadcasted_iota(jnp.int32, (c, c), 1)
    meet = [((tt // (2 * h)) == (ss // (2 * h))) & (((tt // h) % 2) == 1) & (((ss // h) % 2) == 0)
            for h in HGRN_LEVELS]
    eye = tt == ss
    ltri = ltri_ref[...]
    gn = gn_ref[...]

    def chunk(ci, carry):
        rows = pl.ds(pl.multiple_of(ci * c, c), c)
        g = g_ref[0, rows, :]
        g_hi = g.astype(BF16)
        r1 = g - g_hi.astype(F32)
        g_mid = r1.astype(BF16)
        g_lo = (r1 - g_mid.astype(F32)).astype(BF16)
        cum = (jnp.dot(ltri, g_hi, preferred_element_type=F32)
               + jnp.dot(ltri, g_mid, preferred_element_type=F32)
               + jnp.dot(ltri, g_lo, preferred_element_type=F32))
        q = q_ref[0, rows, :]
        k = k_ref[0, rows, :]
        v = v_ref[0, rows, :]
        og = og_ref[0, rows, :]
        q_in = q * jnp.exp(cum)
        k_out = k * jnp.exp(-jnp.abs(cum - cum[c - 1:c, :]))
        d_all = jnp.exp(cum[c - 1:c, :])
        ex = [jnp.exp(-jnp.abs(cum - _hgrn_boundary_rows(cum, h))) for h in HGRN_LEVELS]
        for h in range(N_HEADS):
            hs = slice(h * HEAD_W, (h + 1) * HEAD_W)
            qh = q[:, hs]
            kh = k[:, hs]
            a = jnp.where(eye, jnp.sum(qh * kh, axis=-1, keepdims=True), 0.0)
            for li in range(len(HGRN_LEVELS)):
                e = ex[li][:, hs]
                al = lax.dot_general((qh * e).astype(BF16), (kh * e).astype(BF16), NT_DIMS,
                                     preferred_element_type=F32)
                a = a + jnp.where(meet[li], al, 0.0)
            vb = v[:, hs].astype(BF16)
            st = st_ref[h]
            o = (jnp.dot(a.astype(BF16), vb, preferred_element_type=F32)
                 + lax.dot_general(q_in[:, hs].astype(BF16), st.astype(BF16), NT_DIMS, preferred_element_type=F32))
            st_ref[h] = d_all[:, hs] * st + lax.dot_general(vb, k_out[:, hs].astype(BF16), TN_DIMS,
                                                            preferred_element_type=F32)
            ob_ref[0, rows, hs] = (_rms(o, gn) * og[:, hs]).astype(BF16)
        return carry

    lax.fori_loop(0, ct // c, chunk, 0)

    @pl.when(tstep == pl.num_programs(1) - 1)
    def _():
        for h in range(N_HEADS):
            sout_ref[0, h] = st_ref[h].T


def _hgrn_prompt(qh, kh, gh, vh, og, gn, ct=256):
    b, t, _ = qh.shape
    dall = jnp.asarray(np.tril(np.ones((HGRN_CHUNK, HGRN_CHUNK), np.float32)), dtype=BF16)
    tok_spec = pl.BlockSpec((1, ct, D_MODEL), lambda i, j: (i, j, 0))
    return pl.pallas_call(
        functools.partial(_hgrn_prompt_kernel, ct=ct),
        out_shape=(jax.ShapeDtypeStruct((b, t, D_MODEL), BF16),
                   jax.ShapeDtypeStruct((b, N_HEADS, HEAD_W, HEAD_W), F32)),
        grid=(b, t // ct),
        in_specs=[_const_spec(dall.shape), tok_spec, tok_spec, tok_spec, tok_spec, tok_spec, _const_spec((1, HEAD_W))],
        out_specs=(tok_spec, pl.BlockSpec((1, N_HEADS, HEAD_W, HEAD_W), lambda i, j: (i, 0, 0, 0))),
        scratch_shapes=[pltpu.VMEM((N_HEADS, HEAD_W, HEAD_W), F32)],
        compiler_params=_params(("parallel", "arbitrary")),
        name="hgrn_prompt",
    )(dall, qh, kh, gh, vh, og, gn)


def _lambda(lam_ref, lam_init):
    lv = lam_ref[...]
    s1 = jnp.sum(lv[0:1] * lv[1:2], axis=-1, keepdims=True)
    s2 = jnp.sum(lv[2:3] * lv[3:4], axis=-1, keepdims=True)
    return jnp.exp(s1) - jnp.exp(s2) + lam_init


def _attn_prompt_kernel(lam_ref, g_ref, q_ref, k_ref, v_ref, o_ref, *, tq, tk, lam_init):
    qi = pl.program_id(2)
    q = q_ref[0, 0]
    lane = lax.broadcasted_iota(jnp.int32, (1, HEAD_W), 1)
    zero = jnp.zeros_like(q)
    qs = jnp.concatenate([jnp.where(lane < A_DH, q, zero), jnp.where(lane < A_DH, zero, q)], axis=0)
    rg = ATTN_ROW_GROUP
    ng = 2 * tq // rg

    def block(k0, carry, masked):
        ms, accs = carry
        out_m, out_a = [], []
        for g in range(ng):
            r0 = (g * rg) % tq
            width = min(tk, r0 + rg) if masked else tk
            rows = pl.ds(pl.multiple_of(k0, tk), width)
            s = lax.dot_general(qs[g * rg:(g + 1) * rg], k_ref[0, 0, rows, :], NT_DIMS,
                                preferred_element_type=F32)
            if masked:
                qpos = r0 + lax.broadcasted_iota(jnp.int32, (rg, width), 0)
                kpos = lax.broadcasted_iota(jnp.int32, (rg, width), 1)
                s = jnp.where(kpos <= qpos, s, NEG_BIG)
            m_new = jnp.maximum(ms[g], jnp.max(s, axis=-1, keepdims=True))
            alpha = jnp.exp2(ms[g] - m_new)
            p = jnp.exp2(s - jnp.tile(m_new, (1, width // HEAD_W)))
            pv = jnp.dot(p.astype(BF16), v_ref[0, 0, rows, :], preferred_element_type=F32)
            out_m.append(m_new)
            out_a.append(jnp.tile(alpha, (1, 2)) * accs[g] + pv)
        return tuple(out_m), tuple(out_a)

    carry = (tuple(jnp.full((rg, HEAD_W), NEG_BIG, F32) for _ in range(ng)),
             tuple(jnp.zeros((rg, 2 * HEAD_W), F32) for _ in range(ng)))
    carry = lax.fori_loop(0, qi, lambda i, c: block(i * tk, c, False), carry)
    carry = block(qi * tq, carry, True)
    acc = jnp.concatenate(carry[1], axis=0)
    o = acc[:, 0:HEAD_W] / acc[:, HEAD_W:2 * HEAD_W]
    lam = _lambda(lam_ref, lam_init)
    od = o[0:tq] - lam * o[tq:2 * tq]
    o_ref[0, 0] = _rms(od, g_ref[...]).astype(BF16)


def _attn_prompt(lam4, g_sub, q_hm, k_hm, v_hm, lam_init, tq=1024):
    b, nh, t, _ = q_hm.shape
    tk = tq
    k_spec = pl.BlockSpec((1, 1, t, HEAD_W), lambda i, h, j: (i, h, 0, 0))
    v_spec = pl.BlockSpec((1, 1, t, 2 * HEAD_W), lambda i, h, j: (i, h, 0, 0))
    q_spec = pl.BlockSpec((1, 1, tq, HEAD_W), lambda i, h, j: (i, h, j, 0))
    return pl.pallas_call(
        functools.partial(_attn_prompt_kernel, tq=tq, tk=tk, lam_init=lam_init),
        out_shape=jax.ShapeDtypeStruct((b, nh, t, HEAD_W), BF16),
        grid=(b, nh, t // tq),
        in_specs=[_const_spec(lam4.shape), _const_spec((1, HEAD_W)), q_spec, k_spec, v_spec],
        out_specs=q_spec,
        compiler_params=_params(("parallel", "parallel", "arbitrary")),
        name="attn_prompt",
    )(lam4, g_sub, q_hm, k_hm, v_hm)


def _mix_kernel(x_ref, gm_ref, oa_ref, ob_ref, ga_ref, gb_ref, wa_ref, wb_ref, wo_ref, o_ref):
    oa = jnp.concatenate([oa_ref[0, h] for h in range(N_HEADS)], axis=-1)
    ya = jnp.dot(oa, wa_ref[...], preferred_element_type=F32)
    yb = jnp.dot(ob_ref[0].astype(BF16), wb_ref[...], preferred_element_type=F32)
    merged = ga_ref[0] * ya + gb_ref[0] * yb
    mix = jnp.dot(merged.astype(BF16), wo_ref[...], preferred_element_type=F32)
    o_ref[0] = x_ref[0] + gm_ref[0] * mix


def _mix(x, gm, oa_hm, ob, ga, gb, wa, wb, wo, tm):
    b, t, _ = x.shape
    rows_mod = gm.shape[1]
    tok_spec = pl.BlockSpec((1, tm, D_MODEL), lambda i, j: (i, j, 0))
    mod_spec = (pl.BlockSpec((1, 1, D_MODEL), lambda i, j: (i, 0, 0)) if rows_mod == 1 else tok_spec)
    return pl.pallas_call(
        _mix_kernel,
        out_shape=jax.ShapeDtypeStruct((b, t, D_MODEL), F32),
        grid=(b, t // tm),
        in_specs=[tok_spec, mod_spec, pl.BlockSpec((1, N_HEADS, tm, HEAD_W), lambda i, j: (i, 0, j, 0)),
                  tok_spec, tok_spec, tok_spec,
                  _const_spec(wa.shape), _const_spec(wb.shape), _const_spec(wo.shape)],
        out_specs=tok_spec,
        compiler_params=_params(("parallel", "parallel")),
        name="mix_out",
    )(x, gm, oa_hm, ob, ga, gb, wa, wb, wo)


def _ffn_kernel(*refs, tm, carried):
    if carried:
        (x_ref, sh_ref, sc_ref, gt_ref, g_ref, wup_ref, cw_ref, cb_ref, wdn_ref, y_ref, tail_ref, prev_ref) = refs
    else:
        (x_ref, sh_ref, sc_ref, gt_ref, g_ref, wup_ref, cw_ref, cb_ref, wdn_ref, b0_ref, b1_ref, y_ref, u_ref) = refs
    x = x_ref[0]
    hn = (_rms(x, g_ref[...]) * (1.0 + sc_ref[0]) + sh_ref[0]).astype(BF16)
    if carried:
        @pl.when(pl.program_id(1) == 0)
        def _():
            prev_ref[...] = jnp.zeros_like(prev_ref)
        row = lax.broadcasted_iota(jnp.int32, (tm, 1), 0)

    def conv(u, cols):
        if carried:
            p6 = prev_ref[6:7, cols]
            p7 = prev_ref[7:8, cols]
            u1 = jnp.where(row == 0, p7, pltpu.roll(u, 1, 0))
            u2 = jnp.where(row == 0, p6, jnp.where(row == 1, p7, pltpu.roll(u, 2, 0)))
            prev_ref[:, cols] = u[tm - 8:tm, :]
            tail_ref[0, :, cols] = u[tm - 8:tm, :]
        else:
            u2 = b0_ref[0, :, cols]
            u1 = b1_ref[0, :, cols]
            u_ref[0, :, cols] = u
        return cb_ref[:, cols] + cw_ref[0:1, cols] * u2 + cw_ref[1:2, cols] * u1 + cw_ref[2:3, cols] * u

    acc = jnp.zeros((tm, D_MODEL), F32)
    for c in range(D_FF // FFN_COLS):
        ca = slice(c * FFN_COLS, (c + 1) * FFN_COLS)
        cg = slice(D_FF + c * FFN_COLS, D_FF + (c + 1) * FFN_COLS)
        ya = conv(jnp.dot(hn, wup_ref[:, ca], preferred_element_type=F32), ca)
        yg = conv(jnp.dot(hn, wup_ref[:, cg], preferred_element_type=F32), cg)
        act = (jax.nn.silu(ya) * yg).astype(BF16)
        acc = acc + jnp.dot(act, wdn_ref[ca, :], preferred_element_type=F32)
    y_ref[0] = x + gt_ref[0] * acc


def _ffn(x1, sh, sc, gt, g, wup, cw, cb, wdn, tm, bufs=None):
    b, t, _ = x1.shape
    carried = bufs is None
    tok_spec = pl.BlockSpec((1, tm, D_MODEL), lambda i, j: (i, j, 0))
    mod_spec = (pl.BlockSpec((1, 1, D_MODEL), lambda i, j: (i, 0, 0)) if sh.shape[1] == 1 else tok_spec)
    wide_spec = pl.BlockSpec((1, tm, 2 * D_FF), lambda i, j: (i, j, 0))
    in_specs = [tok_spec, mod_spec, mod_spec, mod_spec, _const_spec((1, D_MODEL)), _const_spec(wup.shape),
                _const_spec(cw.shape), _const_spec(cb.shape), _const_spec(wdn.shape)]
    args = [x1, sh, sc, gt, g, wup, cw, cb, wdn]
    if carried:
        out_shape = (jax.ShapeDtypeStruct((b, t, D_MODEL), F32), jax.ShapeDtypeStruct((b, 8, 2 * D_FF), F32))
        out_specs = (tok_spec, pl.BlockSpec((1, 8, 2 * D_FF), lambda i, j: (i, 0, 0)))
        scratch = [pltpu.VMEM((8, 2 * D_FF), F32)]
        sem = ("parallel", "arbitrary")
    else:
        in_specs += [wide_spec, wide_spec]
        args += list(bufs)
        out_shape = (jax.ShapeDtypeStruct((b, t, D_MODEL), F32), jax.ShapeDtypeStruct((b, t, 2 * D_FF), F32))
        out_specs = (tok_spec, wide_spec)
        scratch = []
        sem = ("parallel", "parallel")
    return pl.pallas_call(
        functools.partial(_ffn_kernel, tm=tm, carried=carried),
        out_shape=out_shape,
        grid=(b, t // tm),
        in_specs=in_specs,
        out_specs=out_specs,
        scratch_shapes=scratch,
        compiler_params=_params(sem),
        name="conv_ffn",
    )(*args)


def _hgrn_sample_kernel(ft_ref, qt_ref, v_ref, og_ref, gn_ref, s_ref, so_ref, o_ref, *, nb):
    kc = pl.program_id(1)

    @pl.when(kc == 0)
    def _():
        o_ref[...] = jnp.zeros_like(o_ref)

    ft = ft_ref[...]
    qt = qt_ref[...]
    for b in range(nb):
        f = ft[:, b:b + 1]
        s_new = f * s_ref[b, 0] + (1.0 - f) * v_ref[b:b + 1, :]
        so_ref[b, 0] = s_new
        o_ref[b:b + 1, :] += jnp.sum(qt[:, b:b + 1] * s_new, axis=0, keepdims=True)

    @pl.when(kc == pl.num_programs(1) - 1)
    def _():
        o_ref[...] = _rms(o_ref[...], gn_ref[...]) * og_ref[...]


def _hgrn_sample(ft, qt, vh, og, gn, state, rows=16):
    nb = vh.shape[0]
    nkc = HEAD_W // rows
    st_spec = pl.BlockSpec((nb, 1, rows, HEAD_W), lambda h, c: (0, h, c, 0))
    ch_spec = pl.BlockSpec((rows, nb), lambda h, c: (h * nkc + c, 0))
    hd_spec = pl.BlockSpec((nb, HEAD_W), lambda h, c: (0, h))
    return pl.pallas_call(
        functools.partial(_hgrn_sample_kernel, nb=nb),
        out_shape=(jax.ShapeDtypeStruct(state.shape, F32), jax.ShapeDtypeStruct((nb, D_MODEL), F32)),
        grid=(N_HEADS, nkc),
        in_specs=[ch_spec, ch_spec, hd_spec, hd_spec, _const_spec((1, HEAD_W)), st_spec],
        out_specs=(st_spec, hd_spec),
        compiler_params=_params(("parallel", "arbitrary")),
        name="hgrn_sample",
    )(ft, qt, vh, og, gn, state)


def _attn_sample_kernel(pt_ref, lam_ref, g_ref, q_ref, kn_ref, vn_ref, ck_hbm, cv_hbm, o_ref,
                        kbuf, vbuf, pd_ref, sem, *, n_pages, page, lam_init):
    b = pl.program_id(0)
    nb = pl.num_programs(0)
    slot = b % 2

    def copies(bb, sl):
        out = []
        for p in range(n_pages):
            pg = pt_ref[bb, p]
            out.append(pltpu.make_async_copy(ck_hbm.at[pg], kbuf.at[sl, p], sem.at[0, sl]))
            out.append(pltpu.make_async_copy(cv_hbm.at[pg], vbuf.at[sl, p], sem.at[1, sl]))
        return out

    @pl.when(b == 0)
    def _():
        for cp in copies(0, 0):
            cp.start()

    @pl.when(b + 1 < nb)
    def _():
        for cp in copies(b + 1, 1 - slot):
            cp.start()

    for cp in copies(b, slot):
        cp.wait()

    q = q_ref[pl.ds(b, 1), :]
    r16 = lax.broadcasted_iota(jnp.int32, (2 * N_HEADS, D_MODEL), 0)
    ln = lax.broadcasted_iota(jnp.int32, (2 * N_HEADS, D_MODEL), 1)
    sel = ((ln // HEAD_W) == (r16 % N_HEADS)) & (((ln // A_DH) % 2) == (r16 // N_HEADS))
    qsel = jnp.where(sel, q, 0.0)
    qblk = qsel.astype(BF16)
    s = jnp.concatenate([jnp.dot(qblk, kbuf[slot, p].astype(BF16), preferred_element_type=F32)
                         for p in range(n_pages)], axis=1)
    s_new = jnp.sum(qsel * kn_ref[pl.ds(b, 1), :], axis=-1, keepdims=True)
    m = jnp.maximum(jnp.max(s, axis=-1, keepdims=True), s_new)
    e = jnp.exp2(s - m)
    e_new = jnp.exp2(s_new - m)
    inv_l = 1.0 / (jnp.sum(e, axis=-1, keepdims=True) + e_new)
    lam = _lambda(lam_ref, lam_init)
    pr = e * inv_l
    pr_new = e_new * inv_l
    pd = pr[0:N_HEADS] - lam * pr[N_HEADS:2 * N_HEADS]
    pd_new = pr_new[0:N_HEADS] - lam * pr_new[N_HEADS:2 * N_HEADS]
    for p in range(n_pages):
        pd_ref[p] = pd[:, p * page:(p + 1) * page]

    def page_sum(p, accs):
        w = pd_ref[p]
        accs = list(accs)
        for pos in range(page):
            accs[pos % N_ACC] = accs[pos % N_ACC] + w[:, pos:pos + 1] * vbuf[slot, p, pos]
        return tuple(accs)

    accs = lax.fori_loop(0, n_pages, page_sum, tuple(jnp.zeros((N_HEADS, HEAD_W), F32) for _ in range(N_ACC)))
    o = pd_new * vn_ref[b]
    for a in accs:
        o = o + a
    o_ref[b] = _rms(o, g_ref[...])


def _attn_sample(page_table, lam4, g_sub, q, k_new, v_new, cache_kt, cache_v, lam_init):
    nb, n_pages = page_table.shape
    page = cache_v.shape[1]
    full = lambda shape: pl.BlockSpec(shape, lambda i, pt: (0,) * len(shape))
    any_spec = pl.BlockSpec(memory_space=pl.ANY)
    grid_spec = pltpu.PrefetchScalarGridSpec(
        num_scalar_prefetch=1,
        grid=(nb,),
        in_specs=[full(lam4.shape), full((1, HEAD_W)), full(q.shape), full(q.shape), full(v_new.shape),
                  any_spec, any_spec],
        out_specs=full((nb, N_HEADS, HEAD_W)),
        scratch_shapes=[pltpu.VMEM((2, n_pages, D_MODEL, page), F32),
                        pltpu.VMEM((2, n_pages, page, N_HEADS, HEAD_W), F32),
                        pltpu.VMEM((n_pages, N_HEADS, page), F32),
                        pltpu.SemaphoreType.DMA((2, 2))],
    )
    return pl.pallas_call(
        functools.partial(_attn_sample_kernel, n_pages=n_pages, page=page, lam_init=lam_init),
        out_shape=jax.ShapeDtypeStruct((nb, N_HEADS, HEAD_W), F32),
        grid_spec=grid_spec,
        compiler_params=_params(("arbitrary",)),
        name="attn_sample",
    )(page_table, lam4, g_sub, q, k_new, v_new, cache_kt, cache_v)


def _rope_tables(pos):
    half = A_DH // 2
    inv = ROPE_THETA ** (-jnp.arange(half, dtype=F32) / half)
    ang = pos.astype(F32)[:, None] * inv[None, :]
    cos = jnp.cos(ang)
    sin = jnp.sin(ang)
    return jnp.tile(cos, (1, 4)), jnp.concatenate([-sin, sin, -sin, sin], axis=-1)


def kernel(x_prompt, x_sample, cache_k, cache_v, state_hgrn, state_conv, page_table, c_prompt, c_sample, w_ada, b_ada,
           rms_mix_g, w_in, q_norm_g, k_norm_g, lambda_q1, lambda_k1, lambda_q2, lambda_k2, subln_g, lb_logits,
           hgrn_norm_g, w_branch_a, w_branch_b, w_out, rms_ffn_g, w_up, conv_w, conv_b, w_down):
    bp, tp, d = x_prompt.shape
    bs, ts, _ = x_sample.shape
    assert d == D_MODEL and ts == 1 and w_in.shape[0] == 1
    n_pool, page = cache_k.shape[1], cache_k.shape[2]
    past_len = page_table.shape[1] * page
    lam_init = 0.8 - 0.6 * math.exp(-0.3 * 0)

    pad = (-(bp + bs)) % 8
    c_all = jnp.concatenate([c_prompt, c_sample, jnp.zeros((pad, d), F32)], axis=0)
    mod = _ada(c_all, w_ada[0], b_ada[0].reshape(1, -1))
    mp = [mod[:bp, i * d:(i + 1) * d].reshape(bp, 1, d) for i in range(6)]
    msamp = [mod[bp:bp + bs, i * d:(i + 1) * d] for i in range(6)]

    w_in_b = w_in[0].astype(BF16)
    wa_b = w_branch_a[0].astype(BF16)
    wb_b = w_branch_b[0].astype(BF16)
    wo_b = w_out[0].astype(BF16)
    wup_b = w_up[0].astype(BF16)
    wdn_b = w_down[0].astype(BF16)
    gmix = rms_mix_g[0].reshape(1, d)
    gffn = rms_ffn_g[0].reshape(1, d)
    qg = jnp.tile(q_norm_g[0], 2).reshape(1, HEAD_W)
    kg = jnp.tile(k_norm_g[0], 2).reshape(1, HEAD_W)
    lam4 = jnp.stack([lambda_q1[0], lambda_k1[0], lambda_q2[0], lambda_k2[0]], axis=0)
    g_sub = (subln_g[0] * (1.0 - lam_init)).reshape(1, HEAD_W)
    gn = hgrn_norm_g[0].reshape(1, HEAD_W)
    cw = conv_w[0]
    cb = conv_b[0].reshape(1, -1)

    cos_p, sin_p = _rope_tables(jnp.arange(tp))
    (q_hm, k_tok, k_hm, v_tok, v_hm, qh, kh, gh, vh, og, ga, gb) = _inproj_prompt(
        x_prompt, mp[0], mp[1], gmix, w_in_b, qg, kg, cos_p, sin_p, lb_logits)
    ob, s_prompt = _hgrn_prompt(qh, kh, gh, vh, og, gn)
    oa_hm = _attn_prompt(lam4, g_sub, q_hm, k_hm, v_hm, lam_init)
    x1 = _mix(x_prompt, mp[2], oa_hm, ob, ga, gb, wa_b, wb_b, wo_b, tm=512)
    y_prompt, tail = _ffn(x1, mp[3], mp[4], mp[5], gffn, wup_b, cw, cb, wdn_b, tm=512)

    cos_s, sin_s = _rope_tables(jnp.full((1,), past_len))
    xs = x_sample.reshape(bs, d)
    (q_s, k_s, v_s, qt_s, ft_s, vh_s, og_s, ga_s, gb_s) = _inproj_sample(
        xs, msamp[0], msamp[1], gmix, w_in_b, qg, kg, cos_s, sin_s, lb_logits)
    s_sample, ob_s = _hgrn_sample(ft_s, qt_s, vh_s, og_s, gn, state_hgrn[0])
    cache_kt = jnp.transpose(cache_k[0], (0, 2, 3, 4, 1)).reshape(n_pool, d, page)
    oa_s = _attn_sample(page_table, lam4, g_sub, q_s, k_s, v_s.reshape(bs, N_HEADS, HEAD_W), cache_kt, cache_v[0],
                        lam_init)
    oa_s = jnp.transpose(oa_s, (1, 0, 2))
    x1_s = _mix(xs[None], msamp[2][None], oa_s[None].astype(BF16), ob_s[None], ga_s[None], gb_s[None],
                wa_b, wb_b, wo_b, tm=bs)
    y_s, u_s = _ffn(x1_s, msamp[3][None], msamp[4][None], msamp[5][None], gffn, wup_b, cw, cb, wdn_b, tm=bs,
                    bufs=(state_conv[0][:, 0][None], state_conv[0][:, 1][None]))

    new_conv_s = jnp.stack([state_conv[0][:, 1], u_s[0]], axis=1)
    return (y_prompt, y_s.reshape(bs, 1, d),
            k_tok.reshape(1, bp, tp, N_HEADS, 2, A_DH), v_tok.reshape(1, bp, tp, N_HEADS, HEAD_W),
            s_prompt[None], tail[:, 6:8][None],
            k_s.reshape(1, bs, 1, N_HEADS, 2, A_DH), v_s.reshape(1, bs, 1, N_HEADS, HEAD_W),
            s_sample[None], new_conv_s[None])
```

```python
import functools
import math

import numpy as np
import jax
import jax.numpy as jnp
from jax import lax
from jax.experimental import pallas as pl
from jax.experimental.pallas import tpu as pltpu

F32 = jnp.float32
BF16 = jnp.bfloat16

D_MODEL = 1024
N_HEADS = 8
HEAD_W = 128
A_DH = 64
D_FF = 2816
N_SEG = 9
ROPE_THETA = 10000.0
EPS = 1e-6
NEG_BIG = -1e30
HGRN_CHUNK = 64
HGRN_LEVELS = (32, 16, 8, 4, 2, 1)
FFN_COLS = 256
VMEM_LIMIT = 56 * 1024 * 1024
Q_SCALE = (A_DH ** -0.5) * math.log2(math.e)

N_ACC = 8
ATTN_ROW_GROUP = 256
NT_DIMS = (((1,), (1,)), ((), ()))
TN_DIMS = (((0,), (0,)), ((), ()))


def _const_spec(shape):
    nd = len(shape)
    return pl.BlockSpec(shape, lambda *_: (0,) * nd, pipeline_mode=pl.Buffered(1))


def _params(sem, vmem=VMEM_LIMIT):
    return pltpu.CompilerParams(dimension_semantics=sem, vmem_limit_bytes=vmem)


def _rms(x, g):
    ms = jnp.mean(x * x, axis=-1, keepdims=True)
    return (x * lax.rsqrt(ms + EPS)) * g


def _ada_kernel(c_ref, w_ref, b_ref, o_ref):
    a = jax.nn.silu(c_ref[...]).astype(BF16)
    o_ref[...] = jnp.dot(a, w_ref[...].astype(BF16), preferred_element_type=F32) + b_ref[...]


def _ada(c_all, w_ada, b_ada):
    m = c_all.shape[0]
    n = w_ada.shape[1]
    tn = 1536
    return pl.pallas_call(
        _ada_kernel,
        out_shape=jax.ShapeDtypeStruct((m, n), F32),
        grid=(n // tn,),
        in_specs=[
            _const_spec((m, D_MODEL)),
            pl.BlockSpec((D_MODEL, tn), lambda j: (0, j)),
            pl.BlockSpec((1, tn), lambda j: (0, j)),
        ],
        out_specs=pl.BlockSpec((m, tn), lambda j: (0, j)),
        compiler_params=_params(("arbitrary",)),
        name="ada_mod",
    )(c_all, w_ada, b_ada)


def _qk_norm_rope(z, g128, cos, sin, lane):
    sq = z * z
    first = lane < A_DH
    lo = jnp.sum(jnp.where(first, sq, 0.0), axis=-1, keepdims=True)
    al = jnp.sum(sq, axis=-1, keepdims=True)
    ms = jnp.where(first, lo, al - lo) * (1.0 / A_DH)
    zn = (z * lax.rsqrt(ms + EPS)) * g128
    rot = jnp.where((lane % A_DH) < A_DH // 2, pltpu.roll(zn, HEAD_W - A_DH // 2, 1), pltpu.roll(zn, A_DH // 2, 1))
    return zn * cos + rot * sin


def _lower_bound(lbl_ref):
    lbl = lbl_ref[...]
    mx = jnp.max(lbl, axis=0, keepdims=True)
    e = jnp.exp(lbl - mx)
    return e[0:1, :] / jnp.sum(e, axis=0, keepdims=True)


def _inproj_common(x, sh, sc, gmix_ref, w_ref):
    xn = _rms(x, gmix_ref[...]) * (1.0 + sc) + sh
    xb = xn.astype(BF16)

    def seg(i):
        return jnp.dot(xb, w_ref[:, i * D_MODEL:(i + 1) * D_MODEL], preferred_element_type=F32)

    return seg


def _inproj_prompt_kernel(x_ref, sh_ref, sc_ref, gmix_ref, w_ref, qg_ref, kg_ref, cos_ref, sin_ref, lbl_ref,
                          qhm_ref, ktok_ref, khm_ref, vtok_ref, vhm_ref,
                          qh_ref, kh_ref, gh_ref, vh_ref, og_ref, ga_ref, gb_ref):
    seg = _inproj_common(x_ref[0], sh_ref[0], sc_ref[0], gmix_ref, w_ref)
    cos = cos_ref[...]
    sin = sin_ref[...]
    lane = lax.broadcasted_iota(jnp.int32, (1, HEAD_W), 1)
    zq = seg(0)
    zk = seg(1)
    for h in range(N_HEADS):
        hs = slice(h * HEAD_W, (h + 1) * HEAD_W)
        q = _qk_norm_rope(zq[:, hs], qg_ref[...], cos, sin, lane) * Q_SCALE
        qhm_ref[0, h] = q.astype(BF16)
        k = _qk_norm_rope(zk[:, hs], kg_ref[...], cos, sin, lane)
        ktok_ref[0, :, hs] = k
        khm_ref[0, h] = k.astype(BF16)
    zv = seg(2)
    vtok_ref[0] = zv
    ones = jnp.ones((zv.shape[0], HEAD_W), BF16)
    for h in range(N_HEADS):
        vhm_ref[0, h, :, 0:HEAD_W] = zv[:, h * HEAD_W:(h + 1) * HEAD_W].astype(BF16)
        vhm_ref[0, h, :, HEAD_W:2 * HEAD_W] = ones
    qh_ref[0] = jax.nn.silu(seg(3))
    lb = _lower_bound(lbl_ref)
    f = lb + (1.0 - lb) * jax.nn.sigmoid(seg(4))
    kh_ref[0] = 1.0 - f
    gh_ref[0] = jnp.log(f)
    vh_ref[0] = seg(5)
    og_ref[0] = jax.nn.silu(seg(6))
    ga_ref[0] = jax.nn.sigmoid(seg(7))
    gb_ref[0] = jax.nn.sigmoid(seg(8))


def _inproj_prompt(x, sh, sc, gmix, w_in, qg, kg, cos, sin, lbl, tm=256):
    b, t, _ = x.shape
    tok = jax.ShapeDtypeStruct((b, t, D_MODEL), F32)
    hm = jax.ShapeDtypeStruct((b, N_HEADS, t, HEAD_W), BF16)
    hm2 = jax.ShapeDtypeStruct((b, N_HEADS, t, 2 * HEAD_W), BF16)
    tok_spec = pl.BlockSpec((1, tm, D_MODEL), lambda i, j: (i, j, 0))
    hm_spec = pl.BlockSpec((1, N_HEADS, tm, HEAD_W), lambda i, j: (i, 0, j, 0))
    hm2_spec = pl.BlockSpec((1, N_HEADS, tm, 2 * HEAD_W), lambda i, j: (i, 0, j, 0))
    mod_spec = pl.BlockSpec((1, 1, D_MODEL), lambda i, j: (i, 0, 0))
    tab_spec = pl.BlockSpec((tm, HEAD_W), lambda i, j: (j, 0))
    return pl.pallas_call(
        _inproj_prompt_kernel,
        out_shape=(hm, tok, hm, tok, hm2, tok, tok, tok, tok, tok, tok, tok),
        grid=(b, t // tm),
        in_specs=[tok_spec, mod_spec, mod_spec, _const_spec((1, D_MODEL)), _const_spec(w_in.shape),
                  _const_spec((1, HEAD_W)), _const_spec((1, HEAD_W)), tab_spec, tab_spec, _const_spec(lbl.shape)],
        out_specs=(hm_spec, tok_spec, hm_spec, tok_spec, hm2_spec,
                   tok_spec, tok_spec, tok_spec, tok_spec, tok_spec, tok_spec, tok_spec),
        compiler_params=_params(("parallel", "parallel")),
        name="inproj_prompt",
    )(x, sh, sc, gmix, w_in, qg, kg, cos, sin, lbl)


def _inproj_sample_kernel(x_ref, sh_ref, sc_ref, gmix_ref, w_ref, qg_ref, kg_ref, cos_ref, sin_ref, lbl_ref,
                          q_ref, k_ref, v_ref, qt_ref, ft_ref, vh_ref, og_ref, ga_ref, gb_ref):
    seg = _inproj_common(x_ref[...], sh_ref[...], sc_ref[...], gmix_ref, w_ref)
    cos = cos_ref[...]
    sin = sin_ref[...]
    lane = lax.broadcasted_iota(jnp.int32, (1, HEAD_W), 1)
    zq = seg(0)
    zk = seg(1)
    for h in range(N_HEADS):
        hs = slice(h * HEAD_W, (h + 1) * HEAD_W)
        q_ref[:, hs] = _qk_norm_rope(zq[:, hs], qg_ref[...], cos, sin, lane) * Q_SCALE
        k_ref[:, hs] = _qk_norm_rope(zk[:, hs], kg_ref[...], cos, sin, lane)
    v_ref[...] = seg(2)
    qh = jax.nn.silu(seg(3))
    lb = _lower_bound(lbl_ref)
    f = lb + (1.0 - lb) * jax.nn.sigmoid(seg(4))
    for h in range(N_HEADS):
        hs = slice(h * HEAD_W, (h + 1) * HEAD_W)
        qt_ref[hs, :] = qh[:, hs].T
        ft_ref[hs, :] = f[:, hs].T
    vh_ref[...] = seg(5)
    og_ref[...] = jax.nn.silu(seg(6))
    ga_ref[...] = jax.nn.sigmoid(seg(7))
    gb_ref[...] = jax.nn.sigmoid(seg(8))


def _inproj_sample(x, sh, sc, gmix, w_in, qg, kg, cos, sin, lbl):
    m = x.shape[0]
    tok = jax.ShapeDtypeStruct((m, D_MODEL), F32)
    chan = jax.ShapeDtypeStruct((D_MODEL, m), F32)
    ts = _const_spec((m, D_MODEL))
    cs = _const_spec((D_MODEL, m))
    return pl.pallas_call(
        _inproj_sample_kernel,
        out_shape=(tok, tok, tok, chan, chan, tok, tok, tok, tok),
        grid=(1,),
        in_specs=[ts, ts, ts, _const_spec((1, D_MODEL)), _const_spec(w_in.shape),
                  _const_spec((1, HEAD_W)), _const_spec((1, HEAD_W)), _const_spec((1, HEAD_W)),
                  _const_spec((1, HEAD_W)), _const_spec(lbl.shape)],
        out_specs=(ts, ts, ts, cs, cs, ts, ts, ts, ts),
        compiler_params=_params(("arbitrary",)),
        name="inproj_sample",
    )(x, sh, sc, gmix, w_in, qg, kg, cos, sin, lbl)


def _hgrn_boundary_rows(cum, h):
    c, d = cum.shape
    if 2 * h >= 8:
        grp = 2 * h
        return jnp.concatenate(
            [jnp.broadcast_to(cum[i * grp + h - 1:i * grp + h, :], (grp, d)) for i in range(c // grp)], axis=0)
    cum3 = cum.reshape(c // 8, 8, d)
    sub = lax.broadcasted_iota(jnp.int32, (c // 8, 8, d), 1)
    out = None
    for first in range(0, 8, 2 * h):
        row = jnp.broadcast_to(cum3[:, first + h - 1:first + h, :], cum3.shape)
        out = row if out is None else jnp.where(sub >= first, row, out)
    return out.reshape(c, d)


def _hgrn_prompt_kernel(ltri_ref, q_ref, k_ref, g_ref, v_ref, og_ref, gn_ref, ob_ref, sout_ref, st_ref, *, ct):
    c = HGRN_CHUNK
    tstep = pl.program_id(1)

    @pl.when(tstep == 0)
    def _():
        st_ref[...] = jnp.zeros_like(st_ref)

    tt = lax.broadcasted_iota(jnp.int32, (c, c), 0)
    ss = lax.broadcasted_iota(jnp.int32, (c, c), 1)
    meet = [((tt // (2 * h)) == (ss // (2 * h))) & (((tt // h) % 2) == 1) & (((ss // h) % 2) == 0)
            for h in HGRN_LEVELS]
    eye = tt == ss
    ltri = ltri_ref[...]
    gn = gn_ref[...]

    def chunk(ci, carry):
        rows = pl.ds(pl.multiple_of(ci * c, c), c)
        g = g_ref[0, rows, :]
        g_hi = g.astype(BF16)
        r1 = g - g_hi.astype(F32)
        g_mid = r1.astype(BF16)
        g_lo = (r1 - g_mid.astype(F32)).astype(BF16)
        cum = (jnp.dot(ltri, g_hi, preferred_element_type=F32)
               + jnp.dot(ltri, g_mid, preferred_element_type=F32)
               + jnp.dot(ltri, g_lo, preferred_element_type=F32))
        q = q_ref[0, rows, :]
        k = k_ref[0, rows, :]
        v = v_ref[0, rows, :]
        og = og_ref[0, rows, :]
        q_in = q * jnp.exp(cum)
        k_out = k * jnp.exp(-jnp.abs(cum - cum[c - 1:c, :]))
        d_all = jnp.exp(cum[c - 1:c, :])
        ex = [jnp.exp(-jnp.abs(cum - _hgrn_boundary_rows(cum, h))) for h in HGRN_LEVELS]
        for h in range(N_HEADS):
            hs = slice(h * HEAD_W, (h + 1) * HEAD_W)
            qh = q[:, hs]
            kh = k[:, hs]
            a = jnp.where(eye, jnp.sum(qh * kh, axis=-1, keepdims=True), 0.0)
            for li in range(len(HGRN_LEVELS)):
                e = ex[li][:, hs]
                al = lax.dot_general((qh * e).astype(BF16), (kh * e).astype(BF16), NT_DIMS,
                                     preferred_element_type=F32)
                a = a + jnp.where(meet[li], al, 0.0)
            vb = v[:, hs].astype(BF16)
            st = st_ref[h]
            o = (jnp.dot(a.astype(BF16), vb, preferred_element_type=F32)
                 + lax.dot_general(q_in[:, hs].astype(BF16), st.astype(BF16), NT_DIMS, preferred_element_type=F32))
            st_ref[h] = d_all[:, hs] * st + lax.dot_general(vb, k_out[:, hs].astype(BF16), TN_DIMS,
                                                            preferred_element_type=F32)
            ob_ref[0, rows, hs] = (_rms(o, gn) * og[:, hs]).astype(BF16)
        return carry

    lax.fori_loop(0, ct // c, chunk, 0)

    @pl.when(tstep == pl.num_programs(1) - 1)
    def _():
        for h in range(N_HEADS):
            sout_ref[0, h] = st_ref[h].T


def _hgrn_prompt(qh, kh, gh, vh, og, gn, ct=256):
    b, t, _ = qh.shape
    dall = jnp.asarray(np.tril(np.ones((HGRN_CHUNK, HGRN_CHUNK), np.float32)), dtype=BF16)
    tok_spec = pl.BlockSpec((1, ct, D_MODEL), lambda i, j: (i, j, 0))
    return pl.pallas_call(
        functools.partial(_hgrn_prompt_kernel, ct=ct),
        out_shape=(jax.ShapeDtypeStruct((b, t, D_MODEL), BF16),
                   jax.ShapeDtypeStruct((b, N_HEADS, HEAD_W, HEAD_W), F32)),
        grid=(b, t // ct),
        in_specs=[_const_spec(dall.shape), tok_spec, tok_spec, tok_spec, tok_spec, tok_spec, _const_spec((1, HEAD_W))],
        out_specs=(tok_spec, pl.BlockSpec((1, N_HEADS, HEAD_W, HEAD_W), lambda i, j: (i, 0, 0, 0))),
        scratch_shapes=[pltpu.VMEM((N_HEADS, HEAD_W, HEAD_W), F32)],
        compiler_params=_params(("parallel", "arbitrary")),
        name="hgrn_prompt",
    )(dall, qh, kh, gh, vh, og, gn)


def _lambda(lam_ref, lam_init):
    lv = lam_ref[...]
    s1 = jnp.sum(lv[0:1] * lv[1:2], axis=-1, keepdims=True)
    s2 = jnp.sum(lv[2:3] * lv[3:4], axis=-1, keepdims=True)
    return jnp.exp(s1) - jnp.exp(s2) + lam_init


def _attn_prompt_step(lam_ref, g_ref, q_ref, k_ref, v_ref, o_ref, *, tq, tk, lam_init):
    qi = pl.program_id(2)
    q = q_ref[0, 0]
    lane = lax.broadcasted_iota(jnp.int32, (1, HEAD_W), 1)
    zero = jnp.zeros_like(q)
    qs = jnp.concatenate([jnp.where(lane < A_DH, q, zero), jnp.where(lane < A_DH, zero, q)], axis=0)
    rg = ATTN_ROW_GROUP
    ng = 2 * tq // rg

    def block(k0, carry, masked):
        ms, accs = carry
        out_m, out_a = [], []
        for g in range(ng):
            r0 = (g * rg) % tq
            width = min(tk, r0 + rg) if masked else tk
            rows = pl.ds(pl.multiple_of(k0, tk), width)
            s = lax.dot_general(qs[g * rg:(g + 1) * rg], k_ref[0, 0, rows, :], NT_DIMS,
                                preferred_element_type=F32)
            if masked:
                qpos = r0 + lax.broadcasted_iota(jnp.int32, (rg, width), 0)
                kpos = lax.broadcasted_iota(jnp.int32, (rg, width), 1)
                s = jnp.where(kpos <= qpos, s, NEG_BIG)
            m_new = jnp.maximum(ms[g], jnp.max(s, axis=-1, keepdims=True))
            alpha = jnp.exp2(ms[g] - m_new)
            p = jnp.exp2(s - jnp.tile(m_new, (1, width // HEAD_W)))
            pv = jnp.dot(p.astype(BF16), v_ref[0, 0, rows, :], preferred_element_type=F32)
            out_m.append(m_new)
            out_a.append(jnp.tile(alpha, (1, 2)) * accs[g] + pv)
        return tuple(out_m), tuple(out_a)

    carry = (tuple(jnp.full((rg, HEAD_W), NEG_BIG, F32) for _ in range(ng)),
             tuple(jnp.zeros((rg, 2 * HEAD_W), F32) for _ in range(ng)))
    carry = lax.fori_loop(0, qi, lambda i, c: block(i * tk, c, False), carry)
    carry = block(qi * tq, carry, True)
    acc = jnp.concatenate(carry[1], axis=0)
    o = acc[:, 0:HEAD_W] / acc[:, HEAD_W:2 * HEAD_W]
    lam = _lambda(lam_ref, lam_init)
    od = o[0:tq] - lam * o[tq:2 * tq]
    o_ref[0, 0] = _rms(od, g_ref[...]).astype(BF16)


def _mix_kernel(x_ref, gm_ref, oa_ref, ob_ref, ga_ref, gb_ref, wa_ref, wb_ref, wo_ref, o_ref):
    oa = jnp.concatenate([oa_ref[0, h] for h in range(N_HEADS)], axis=-1)
    ya = jnp.dot(oa, wa_ref[...], preferred_element_type=F32)
    yb = jnp.dot(ob_ref[0].astype(BF16), wb_ref[...], preferred_element_type=F32)
    merged = ga_ref[0] * ya + gb_ref[0] * yb
    mix = jnp.dot(merged.astype(BF16), wo_ref[...], preferred_element_type=F32)
    o_ref[0] = x_ref[0] + gm_ref[0] * mix


def _mix(x, gm, oa_hm, ob, ga, gb, wa, wb, wo, tm):
    b, t, _ = x.shape
    rows_mod = gm.shape[1]
    tok_spec = pl.BlockSpec((1, tm, D_MODEL), lambda i, j: (i, j, 0))
    mod_spec = (pl.BlockSpec((1, 1, D_MODEL), lambda i, j: (i, 0, 0)) if rows_mod == 1 else tok_spec)
    return pl.pallas_call(
        _mix_kernel,
        out_shape=jax.ShapeDtypeStruct((b, t, D_MODEL), F32),
        grid=(b, t // tm),
        in_specs=[tok_spec, mod_spec, pl.BlockSpec((1, N_HEADS, tm, HEAD_W), lambda i, j: (i, 0, j, 0)),
                  tok_spec, tok_spec, tok_spec,
                  _const_spec(wa.shape), _const_spec(wb.shape), _const_spec(wo.shape)],
        out_specs=tok_spec,
        compiler_params=_params(("parallel", "parallel")),
        name="mix_out",
    )(x, gm, oa_hm, ob, ga, gb, wa, wb, wo)


def _ffn_kernel(*refs, tm, carried):
    if carried:
        (x_ref, sh_ref, sc_ref, gt_ref, g_ref, wup_ref, cw_ref, cb_ref, wdn_ref, y_ref, tail_ref, prev_ref) = refs
    else:
        (x_ref, sh_ref, sc_ref, gt_ref, g_ref, wup_ref, cw_ref, cb_ref, wdn_ref, b0_ref, b1_ref, y_ref, u_ref) = refs
    x = x_ref[0]
    hn = (_rms(x, g_ref[...]) * (1.0 + sc_ref[0]) + sh_ref[0]).astype(BF16)
    if carried:
        @pl.when(pl.program_id(1) == 0)
        def _():
            prev_ref[...] = jnp.zeros_like(prev_ref)
        row = lax.broadcasted_iota(jnp.int32, (tm, 1), 0)

    def conv(u, cols):
        if carried:
            p6 = prev_ref[6:7, cols]
            p7 = prev_ref[7:8, cols]
            u1 = jnp.where(row == 0, p7, pltpu.roll(u, 1, 0))
            u2 = jnp.where(row == 0, p6, jnp.where(row == 1, p7, pltpu.roll(u, 2, 0)))
            prev_ref[:, cols] = u[tm - 8:tm, :]
            tail_ref[0, :, cols] = u[tm - 8:tm, :]
        else:
            u2 = b0_ref[0, :, cols]
            u1 = b1_ref[0, :, cols]
            u_ref[0, :, cols] = u
        return cb_ref[:, cols] + cw_ref[0:1, cols] * u2 + cw_ref[1:2, cols] * u1 + cw_ref[2:3, cols] * u

    nchunk = D_FF // FFN_COLS

    def cols(c):
        return slice(c * FFN_COLS, (c + 1) * FFN_COLS), slice(D_FF + c * FFN_COLS, D_FF + (c + 1) * FFN_COLS)

    def up(c):
        ca, cg = cols(c)
        return (jnp.dot(hn, wup_ref[:, ca], preferred_element_type=F32),
                jnp.dot(hn, wup_ref[:, cg], preferred_element_type=F32))

    acc = jnp.zeros((tm, D_MODEL), F32)
    ups = [up(0), up(1)]
    acts = []
    for c in range(nchunk):
        ua, ug = ups[c]
        if c + 2 < nchunk:
            ups.append(up(c + 2))
        ca, cg = cols(c)
        acts.append((jax.nn.silu(conv(ua, ca)) * conv(ug, cg)).astype(BF16))
        if c >= 1:
            acc = acc + jnp.dot(acts[c - 1], wdn_ref[cols(c - 1)[0], :], preferred_element_type=F32)
    acc = acc + jnp.dot(acts[nchunk - 1], wdn_ref[cols(nchunk - 1)[0], :], preferred_element_type=F32)
    y_ref[0] = x + gt_ref[0] * acc


def _ffn(x1, sh, sc, gt, g, wup, cw, cb, wdn, tm, bufs=None):
    b, t, _ = x1.shape
    carried = bufs is None
    tok_spec = pl.BlockSpec((1, tm, D_MODEL), lambda i, j: (i, j, 0))
    mod_spec = (pl.BlockSpec((1, 1, D_MODEL), lambda i, j: (i, 0, 0)) if sh.shape[1] == 1 else tok_spec)
    wide_spec = pl.BlockSpec((1, tm, 2 * D_FF), lambda i, j: (i, j, 0))
    in_specs = [tok_spec, mod_spec, mod_spec, mod_spec, _const_spec((1, D_MODEL)), _const_spec(wup.shape),
                _const_spec(cw.shape), _const_spec(cb.shape), _const_spec(wdn.shape)]
    args = [x1, sh, sc, gt, g, wup, cw, cb, wdn]
    if carried:
        out_shape = (jax.ShapeDtypeStruct((b, t, D_MODEL), F32), jax.ShapeDtypeStruct((b, 8, 2 * D_FF), F32))
        out_specs = (tok_spec, pl.BlockSpec((1, 8, 2 * D_FF), lambda i, j: (i, 0, 0)))
        scratch = [pltpu.VMEM((8, 2 * D_FF), F32)]
        sem = ("parallel", "arbitrary")
    else:
        in_specs += [wide_spec, wide_spec]
        args += list(bufs)
        out_shape = (jax.ShapeDtypeStruct((b, t, D_MODEL), F32), jax.ShapeDtypeStruct((b, t, 2 * D_FF), F32))
        out_specs = (tok_spec, wide_spec)
        scratch = []
        sem = ("parallel", "parallel")
    return pl.pallas_call(
        functools.partial(_ffn_kernel, tm=tm, carried=carried),
        out_shape=out_shape,
        grid=(b, t // tm),
        in_specs=in_specs,
        out_specs=out_specs,
        scratch_shapes=scratch,
        compiler_params=_params(sem),
        name="conv_ffn",
    )(*args)


def _hgrn_sample_kernel(ft_ref, qt_ref, v_ref, og_ref, gn_ref, s_ref, so_ref, o_ref, *, nb):
    kc = pl.program_id(1)

    @pl.when(kc == 0)
    def _():
        o_ref[...] = jnp.zeros_like(o_ref)

    ft = ft_ref[...]
    qt = qt_ref[...]
    for b in range(nb):
        f = ft[:, b:b + 1]
        s_new = f * s_ref[b, 0] + (1.0 - f) * v_ref[b:b + 1, :]
        so_ref[b, 0] = s_new
        o_ref[b:b + 1, :] += jnp.sum(qt[:, b:b + 1] * s_new, axis=0, keepdims=True)

    @pl.when(kc == pl.num_programs(1) - 1)
    def _():
        o_ref[...] = _rms(o_ref[...], gn_ref[...]) * og_ref[...]


def _hgrn_sample(ft, qt, vh, og, gn, state, rows=16):
    nb = vh.shape[0]
    nkc = HEAD_W // rows
    st_spec = pl.BlockSpec((nb, 1, rows, HEAD_W), lambda h, c: (0, h, c, 0))
    ch_spec = pl.BlockSpec((rows, nb), lambda h, c: (h * nkc + c, 0))
    hd_spec = pl.BlockSpec((nb, HEAD_W), lambda h, c: (0, h))
    return pl.pallas_call(
        functools.partial(_hgrn_sample_kernel, nb=nb),
        out_shape=(jax.ShapeDtypeStruct(state.shape, F32), jax.ShapeDtypeStruct((nb, D_MODEL), F32)),
        grid=(N_HEADS, nkc),
        in_specs=[ch_spec, ch_spec, hd_spec, hd_spec, _const_spec((1, HEAD_W)), st_spec],
        out_specs=(st_spec, hd_spec),
        compiler_params=_params(("parallel", "arbitrary")),
        name="hgrn_sample",
    )(ft, qt, vh, og, gn, state)


def _page_copies(pt_ref, ck_hbm, cv_hbm, kbuf, vbuf, sem, b, n_pages):
    sl = b % 2
    out = []
    for p in range(n_pages):
        pg = pt_ref[b, p]
        out.append(pltpu.make_async_copy(ck_hbm.at[pg], kbuf.at[sl, p], sem.at[0, sl]))
        out.append(pltpu.make_async_copy(cv_hbm.at[pg], vbuf.at[sl, p], sem.at[1, sl]))
    return out


def _attn_sample_one(b, lam_ref, g_ref, q_ref, kn_ref, vn_ref, o_ref, kbuf, vbuf, pd_ref, *, n_pages, page, lam_init):
    slot = b % 2
    q = q_ref[pl.ds(b, 1), :]
    r16 = lax.broadcasted_iota(jnp.int32, (2 * N_HEADS, D_MODEL), 0)
    ln = lax.broadcasted_iota(jnp.int32, (2 * N_HEADS, D_MODEL), 1)
    sel = ((ln // HEAD_W) == (r16 % N_HEADS)) & (((ln // A_DH) % 2) == (r16 // N_HEADS))
    qsel = jnp.where(sel, q, 0.0)
    qblk = qsel.astype(BF16)
    s = jnp.concatenate([jnp.dot(qblk, kbuf[slot, p].astype(BF16), preferred_element_type=F32)
                         for p in range(n_pages)], axis=1)
    s_new = jnp.sum(qsel * kn_ref[pl.ds(b, 1), :], axis=-1, keepdims=True)
    m = jnp.maximum(jnp.max(s, axis=-1, keepdims=True), s_new)
    e = jnp.exp2(s - m)
    e_new = jnp.exp2(s_new - m)
    inv_l = 1.0 / (jnp.sum(e, axis=-1, keepdims=True) + e_new)
    lam = _lambda(lam_ref, lam_init)
    pr = e * inv_l
    pr_new = e_new * inv_l
    pd = pr[0:N_HEADS] - lam * pr[N_HEADS:2 * N_HEADS]
    pd_new = pr_new[0:N_HEADS] - lam * pr_new[N_HEADS:2 * N_HEADS]
    for p in range(n_pages):
        pd_ref[p] = pd[:, p * page:(p + 1) * page]

    def page_sum(p, accs):
        w = pd_ref[p]
        accs = list(accs)
        for pos in range(page):
            accs[pos % N_ACC] = accs[pos % N_ACC] + w[:, pos:pos + 1] * vbuf[slot, p, pos]
        return tuple(accs)

    accs = lax.fori_loop(0, n_pages, page_sum, tuple(jnp.zeros((N_HEADS, HEAD_W), F32) for _ in range(N_ACC)))
    o = pd_new * vn_ref[b]
    for a in accs:
        o = o + a
    o_ref[b] = _rms(o, g_ref[...])


def _attn_kernel(pt_ref, lam_ref, g_ref, q_ref, k_ref, v_ref, qs_ref, kn_ref, vn_ref, ck_hbm, cv_hbm,
                 o_ref, os_ref, kbuf, vbuf, pd_ref, sem, *, tq, lam_init, n_pages, page, per_step):
    step = (pl.program_id(0) * pl.num_programs(1) + pl.program_id(1)) * pl.num_programs(2) + pl.program_id(2)
    n_steps = pl.num_programs(0) * pl.num_programs(1) * pl.num_programs(2)
    first = step * per_step
    copies = functools.partial(_page_copies, pt_ref, ck_hbm, cv_hbm, kbuf, vbuf, sem, n_pages=n_pages)

    @pl.when(step == 0)
    def _():
        for cp in copies(0):
            cp.start()

    if per_step == 1:
        @pl.when(step + 1 < n_steps)
        def _():
            for cp in copies(step + 1):
                cp.start()

    _attn_prompt_step(lam_ref, g_ref, q_ref, k_ref, v_ref, o_ref, tq=tq, tk=tq, lam_init=lam_init)

    for j in range(per_step):
        b = first + j
        if per_step > 1:
            @pl.when(b + 1 < n_steps * per_step)
            def _():
                for cp in copies(b + 1):
                    cp.start()
        for cp in copies(b):
            cp.wait()
        _attn_sample_one(b, lam_ref, g_ref, qs_ref, kn_ref, vn_ref, os_ref, kbuf, vbuf, pd_ref,
                         n_pages=n_pages, page=page, lam_init=lam_init)


def _attention(page_table, lam4, g_sub, q_hm, k_hm, v_hm, q_s, k_new, v_new, cache_kt, cache_v, lam_init, tq=1024):
    b, nh, t, _ = q_hm.shape
    nb, n_pages = page_table.shape
    page = cache_v.shape[1]
    n_steps = b * nh * (t // tq)
    assert nb % n_steps == 0, (nb, n_steps)
    full = lambda shape: pl.BlockSpec(shape, lambda i, h, j, pt: (0,) * len(shape))
    any_spec = pl.BlockSpec(memory_space=pl.ANY)
    k_spec = pl.BlockSpec((1, 1, t, HEAD_W), lambda i, h, j, pt: (i, h, 0, 0))
    v_spec = pl.BlockSpec((1, 1, t, 2 * HEAD_W), lambda i, h, j, pt: (i, h, 0, 0))
    q_spec = pl.BlockSpec((1, 1, tq, HEAD_W), lambda i, h, j, pt: (i, h, j, 0))
    grid_spec = pltpu.PrefetchScalarGridSpec(
        num_scalar_prefetch=1,
        grid=(b, nh, t // tq),
        in_specs=[full(lam4.shape), full((1, HEAD_W)), q_spec, k_spec, v_spec,
                  full(q_s.shape), full(k_new.shape), full(v_new.shape), any_spec, any_spec],
        out_specs=(q_spec, full((nb, N_HEADS, HEAD_W))),
        scratch_shapes=[pltpu.VMEM((2, n_pages, D_MODEL, page), F32),
                        pltpu.VMEM((2, n_pages, page, N_HEADS, HEAD_W), F32),
                        pltpu.VMEM((n_pages, N_HEADS, page), F32),
                        pltpu.SemaphoreType.DMA((2, 2))],
    )
    return pl.pallas_call(
        functools.partial(_attn_kernel, tq=tq, lam_init=lam_init, n_pages=n_pages, page=page,
                          per_step=nb // n_steps),
        out_shape=(jax.ShapeDtypeStruct((b, nh, t, HEAD_W), BF16), jax.ShapeDtypeStruct((nb, N_HEADS, HEAD_W), F32)),
        grid_spec=grid_spec,
        compiler_params=_params(("arbitrary", "arbitrary", "arbitrary")),
        name="attention",
    )(page_table, lam4, g_sub, q_hm, k_hm, v_hm, q_s, k_new, v_new, cache_kt, cache_v)


def _rope_tables(pos):
    half = A_DH // 2
    inv = ROPE_THETA ** (-jnp.arange(half, dtype=F32) / half)
    ang = pos.astype(F32)[:, None] * inv[None, :]
    cos = jnp.cos(ang)
    sin = jnp.sin(ang)
    return jnp.tile(cos, (1, 4)), jnp.concatenate([-sin, sin, -sin, sin], axis=-1)


def kernel(x_prompt, x_sample, cache_k, cache_v, state_hgrn, state_conv, page_table, c_prompt, c_sample, w_ada, b_ada,
           rms_mix_g, w_in, q_norm_g, k_norm_g, lambda_q1, lambda_k1, lambda_q2, lambda_k2, subln_g, lb_logits,
           hgrn_norm_g, w_branch_a, w_branch_b, w_out, rms_ffn_g, w_up, conv_w, conv_b, w_down):
    bp, tp, d = x_prompt.shape
    bs, ts, _ = x_sample.shape
    assert d == D_MODEL and ts == 1 and w_in.shape[0] == 1
    n_pool, page = cache_k.shape[1], cache_k.shape[2]
    past_len = page_table.shape[1] * page
    lam_init = 0.8 - 0.6 * math.exp(-0.3 * 0)

    pad = (-(bp + bs)) % 8
    c_all = jnp.concatenate([c_prompt, c_sample, jnp.zeros((pad, d), F32)], axis=0)
    mod = _ada(c_all, w_ada[0], b_ada[0].reshape(1, -1))
    mp = [mod[:bp, i * d:(i + 1) * d].reshape(bp, 1, d) for i in range(6)]
    msamp = [mod[bp:bp + bs, i * d:(i + 1) * d] for i in range(6)]

    w_in_b = w_in[0].astype(BF16)
    wa_b = w_branch_a[0].astype(BF16)
    wb_b = w_branch_b[0].astype(BF16)
    wo_b = w_out[0].astype(BF16)
    wup_b = w_up[0].astype(BF16)
    wdn_b = w_down[0].astype(BF16)
    gmix = rms_mix_g[0].reshape(1, d)
    gffn = rms_ffn_g[0].reshape(1, d)
    qg = jnp.tile(q_norm_g[0], 2).reshape(1, HEAD_W)
    kg = jnp.tile(k_norm_g[0], 2).reshape(1, HEAD_W)
    lam4 = jnp.stack([lambda_q1[0], lambda_k1[0], lambda_q2[0], lambda_k2[0]], axis=0)
    g_sub = (subln_g[0] * (1.0 - lam_init)).reshape(1, HEAD_W)
    gn = hgrn_norm_g[0].reshape(1, HEAD_W)
    cw = conv_w[0]
    cb = conv_b[0].reshape(1, -1)

    cos_p, sin_p = _rope_tables(jnp.arange(tp))
    (q_hm, k_tok, k_hm, v_tok, v_hm, qh, kh, gh, vh, og, ga, gb) = _inproj_prompt(
        x_prompt, mp[0], mp[1], gmix, w_in_b, qg, kg, cos_p, sin_p, lb_logits)
    ob, s_prompt = _hgrn_prompt(qh, kh, gh, vh, og, gn)
    cos_s, sin_s = _rope_tables(jnp.full((1,), past_len))
    xs = x_sample.reshape(bs, d)
    (q_s, k_s, v_s, qt_s, ft_s, vh_s, og_s, ga_s, gb_s) = _inproj_sample(
        xs, msamp[0], msamp[1], gmix, w_in_b, qg, kg, cos_s, sin_s, lb_logits)
    s_sample, ob_s = _hgrn_sample(ft_s, qt_s, vh_s, og_s, gn, state_hgrn[0])

    cache_kt = jnp.transpose(cache_k[0], (0, 2, 3, 4, 1)).reshape(n_pool, d, page)
    oa_hm, oa_s = _attention(page_table, lam4, g_sub, q_hm, k_hm, v_hm, q_s, k_s, v_s.reshape(bs, N_HEADS, HEAD_W),
                             cache_kt, cache_v[0], lam_init)

    x1 = _mix(x_prompt, mp[2], oa_hm, ob, ga, gb, wa_b, wb_b, wo_b, tm=512)
    y_prompt, tail = _ffn(x1, mp[3], mp[4], mp[5], gffn, wup_b, cw, cb, wdn_b, tm=512)
    oa_s = jnp.transpose(oa_s, (1, 0, 2))
    x1_s = _mix(xs[None], msamp[2][None], oa_s[None].astype(BF16), ob_s[None], ga_s[None], gb_s[None],
                wa_b, wb_b, wo_b, tm=bs)
    y_s, u_s = _ffn(x1_s, msamp[3][None], msamp[4][None], msamp[5][None], gffn, wup_b, cw, cb, wdn_b, tm=bs,
                    bufs=(state_conv[0][:, 0][None], state_conv[0][:, 1][None]))

    new_conv_s = jnp.stack([state_conv[0][:, 1], u_s[0]], axis=1)
    return (y_prompt, y_s.reshape(bs, 1, d),
            k_tok.reshape(1, bp, tp, N_HEADS, 2, A_DH), v_tok.reshape(1, bp, tp, N_HEADS, HEAD_W),
            s_prompt[None], tail[:, 6:8][None],
            k_s.reshape(1, bs, 1, N_HEADS, 2, A_DH), v_s.reshape(1, bs, 1, N_HEADS, HEAD_W),
            s_sample[None], new_conv_s[None])
```

```python
import functools
import math

import numpy as np
import jax
import jax.numpy as jnp
from jax import lax
from jax.experimental import pallas as pl
from jax.experimental.pallas import tpu as pltpu

F32 = jnp.float32
BF16 = jnp.bfloat16

D_MODEL = 1024
N_HEADS = 8
HEAD_W = 128
A_DH = 64
D_FF = 2816
N_SEG = 9
ROPE_THETA = 10000.0
EPS = 1e-6
NEG_BIG = -1e30
HGRN_CHUNK = 64
HGRN_LEVELS = (32, 16, 8, 4, 2, 1)
FFN_COLS = 256
VMEM_LIMIT = 56 * 1024 * 1024
Q_SCALE = (A_DH ** -0.5) * math.log2(math.e)

ATTN_ROW_GROUP = 256
NT_DIMS = (((1,), (1,)), ((), ()))
TN_DIMS = (((0,), (0,)), ((), ()))


def _const_spec(shape):
    nd = len(shape)
    return pl.BlockSpec(shape, lambda *_: (0,) * nd, pipeline_mode=pl.Buffered(1))


def _params(sem, vmem=VMEM_LIMIT):
    return pltpu.CompilerParams(dimension_semantics=sem, vmem_limit_bytes=vmem)


def _rms(x, g):
    ms = jnp.mean(x * x, axis=-1, keepdims=True)
    return (x * lax.rsqrt(ms + EPS)) * g


def _ada_kernel(c_ref, w_ref, b_ref, o_ref):
    a = jax.nn.silu(c_ref[...]).astype(BF16)
    o_ref[...] = jnp.dot(a, w_ref[...].astype(BF16), preferred_element_type=F32) + b_ref[...]


def _ada(c_all, w_ada, b_ada):
    m = c_all.shape[0]
    n = w_ada.shape[1]
    tn = 1536
    return pl.pallas_call(
        _ada_kernel,
        out_shape=jax.ShapeDtypeStruct((m, n), F32),
        grid=(n // tn,),
        in_specs=[
            _const_spec((m, D_MODEL)),
            pl.BlockSpec((D_MODEL, tn), lambda j: (0, j)),
            pl.BlockSpec((1, tn), lambda j: (0, j)),
        ],
        out_specs=pl.BlockSpec((m, tn), lambda j: (0, j)),
        compiler_params=_params(("arbitrary",)),
        name="ada_mod",
    )(c_all, w_ada, b_ada)


def _qk_norm_rope(z, g128, cos, sin, lane):
    sq = z * z
    first = lane < A_DH
    lo = jnp.sum(jnp.where(first, sq, 0.0), axis=-1, keepdims=True)
    al = jnp.sum(sq, axis=-1, keepdims=True)
    ms = jnp.where(first, lo, al - lo) * (1.0 / A_DH)
    zn = (z * lax.rsqrt(ms + EPS)) * g128
    rot = jnp.where((lane % A_DH) < A_DH // 2, pltpu.roll(zn, HEAD_W - A_DH // 2, 1), pltpu.roll(zn, A_DH // 2, 1))
    return zn * cos + rot * sin


def _lower_bound(lbl_ref):
    lbl = lbl_ref[...]
    mx = jnp.max(lbl, axis=0, keepdims=True)
    e = jnp.exp(lbl - mx)
    return e[0:1, :] / jnp.sum(e, axis=0, keepdims=True)


def _inproj_common(x, sh, sc, gmix_ref, w_ref):
    xn = _rms(x, gmix_ref[...]) * (1.0 + sc) + sh
    xb = xn.astype(BF16)

    def seg(i):
        return jnp.dot(xb, w_ref[:, i * D_MODEL:(i + 1) * D_MODEL], preferred_element_type=F32)

    return seg


def _inproj_prompt_kernel(x_ref, sh_ref, sc_ref, gmix_ref, w_ref, qg_ref, kg_ref, cos_ref, sin_ref, lbl_ref,
                          qhm_ref, ktok_ref, khm_ref, vtok_ref, vhm_ref,
                          qh_ref, kh_ref, gh_ref, vh_ref, og_ref, ga_ref, gb_ref):
    seg = _inproj_common(x_ref[0], sh_ref[0], sc_ref[0], gmix_ref, w_ref)
    cos = cos_ref[...]
    sin = sin_ref[...]
    lane = lax.broadcasted_iota(jnp.int32, (1, HEAD_W), 1)
    zq = seg(0)
    zk = seg(1)
    for h in range(N_HEADS):
        hs = slice(h * HEAD_W, (h + 1) * HEAD_W)
        q = _qk_norm_rope(zq[:, hs], qg_ref[...], cos, sin, lane) * Q_SCALE
        qhm_ref[0, h] = q.astype(BF16)
        k = _qk_norm_rope(zk[:, hs], kg_ref[...], cos, sin, lane)
        ktok_ref[0, :, hs] = k
        khm_ref[0, h] = k.astype(BF16)
    zv = seg(2)
    vtok_ref[0] = zv
    ones = jnp.ones((zv.shape[0], HEAD_W), BF16)
    for h in range(N_HEADS):
        vhm_ref[0, h, :, 0:HEAD_W] = zv[:, h * HEAD_W:(h + 1) * HEAD_W].astype(BF16)
        vhm_ref[0, h, :, HEAD_W:2 * HEAD_W] = ones
    qh_ref[0] = jax.nn.silu(seg(3))
    lb = _lower_bound(lbl_ref)
    f = lb + (1.0 - lb) * jax.nn.sigmoid(seg(4))
    kh_ref[0] = 1.0 - f
    gh_ref[0] = jnp.log(f)
    vh_ref[0] = seg(5)
    og_ref[0] = jax.nn.silu(seg(6))
    ga_ref[0] = jax.nn.sigmoid(seg(7))
    gb_ref[0] = jax.nn.sigmoid(seg(8))


def _inproj_prompt(x, sh, sc, gmix, w_in, qg, kg, cos, sin, lbl, tm=256):
    b, t, _ = x.shape
    tok = jax.ShapeDtypeStruct((b, t, D_MODEL), F32)
    hm = jax.ShapeDtypeStruct((b, N_HEADS, t, HEAD_W), BF16)
    hm2 = jax.ShapeDtypeStruct((b, N_HEADS, t, 2 * HEAD_W), BF16)
    tok_spec = pl.BlockSpec((1, tm, D_MODEL), lambda i, j: (i, j, 0))
    hm_spec = pl.BlockSpec((1, N_HEADS, tm, HEAD_W), lambda i, j: (i, 0, j, 0))
    hm2_spec = pl.BlockSpec((1, N_HEADS, tm, 2 * HEAD_W), lambda i, j: (i, 0, j, 0))
    mod_spec = pl.BlockSpec((1, 1, D_MODEL), lambda i, j: (i, 0, 0))
    tab_spec = pl.BlockSpec((tm, HEAD_W), lambda i, j: (j, 0))
    return pl.pallas_call(
        _inproj_prompt_kernel,
        out_shape=(hm, tok, hm, tok, hm2, tok, tok, tok, tok, tok, tok, tok),
        grid=(b, t // tm),
        in_specs=[tok_spec, mod_spec, mod_spec, _const_spec((1, D_MODEL)), _const_spec(w_in.shape),
                  _const_spec((1, HEAD_W)), _const_spec((1, HEAD_W)), tab_spec, tab_spec, _const_spec(lbl.shape)],
        out_specs=(hm_spec, tok_spec, hm_spec, tok_spec, hm2_spec,
                   tok_spec, tok_spec, tok_spec, tok_spec, tok_spec, tok_spec, tok_spec),
        compiler_params=_params(("parallel", "parallel")),
        name="inproj_prompt",
    )(x, sh, sc, gmix, w_in, qg, kg, cos, sin, lbl)


def _inproj_sample_kernel(x_ref, sh_ref, sc_ref, gmix_ref, w_ref, qg_ref, kg_ref, cos_ref, sin_ref, lbl_ref,
                          q_ref, k_ref, v_ref, qt_ref, ft_ref, vh_ref, og_ref, ga_ref, gb_ref):
    seg = _inproj_common(x_ref[...], sh_ref[...], sc_ref[...], gmix_ref, w_ref)
    cos = cos_ref[...]
    sin = sin_ref[...]
    lane = lax.broadcasted_iota(jnp.int32, (1, HEAD_W), 1)
    zq = seg(0)
    zk = seg(1)
    for h in range(N_HEADS):
        hs = slice(h * HEAD_W, (h + 1) * HEAD_W)
        q_ref[:, hs] = _qk_norm_rope(zq[:, hs], qg_ref[...], cos, sin, lane) * Q_SCALE
        k_ref[:, hs] = _qk_norm_rope(zk[:, hs], kg_ref[...], cos, sin, lane)
    v_ref[...] = seg(2)
    qh = jax.nn.silu(seg(3))
    lb = _lower_bound(lbl_ref)
    f = lb + (1.0 - lb) * jax.nn.sigmoid(seg(4))
    for h in range(N_HEADS):
        hs = slice(h * HEAD_W, (h + 1) * HEAD_W)
        qt_ref[hs, :] = qh[:, hs].T
        ft_ref[hs, :] = f[:, hs].T
    vh_ref[...] = seg(5)
    og_ref[...] = jax.nn.silu(seg(6))
    ga_ref[...] = jax.nn.sigmoid(seg(7))
    gb_ref[...] = jax.nn.sigmoid(seg(8))


def _inproj_sample(x, sh, sc, gmix, w_in, qg, kg, cos, sin, lbl):
    m = x.shape[0]
    tok = jax.ShapeDtypeStruct((m, D_MODEL), F32)
    chan = jax.ShapeDtypeStruct((D_MODEL, m), F32)
    ts = _const_spec((m, D_MODEL))
    cs = _const_spec((D_MODEL, m))
    return pl.pallas_call(
        _inproj_sample_kernel,
        out_shape=(tok, tok, tok, chan, chan, tok, tok, tok, tok),
        grid=(1,),
        in_specs=[ts, ts, ts, _const_spec((1, D_MODEL)), _const_spec(w_in.shape),
                  _const_spec((1, HEAD_W)), _const_spec((1, HEAD_W)), _const_spec((1, HEAD_W)),
                  _const_spec((1, HEAD_W)), _const_spec(lbl.shape)],
        out_specs=(ts, ts, ts, cs, cs, ts, ts, ts, ts),
        compiler_params=_params(("arbitrary",)),
        name="inproj_sample",
    )(x, sh, sc, gmix, w_in, qg, kg, cos, sin, lbl)


def _hgrn_boundary_rows(cum, h):
    c, d = cum.shape
    if 2 * h >= 8:
        grp = 2 * h
        return jnp.concatenate(
            [jnp.broadcast_to(cum[i * grp + h - 1:i * grp + h, :], (grp, d)) for i in range(c // grp)], axis=0)
    cum3 = cum.reshape(c // 8, 8, d)
    sub = lax.broadcasted_iota(jnp.int32, (c // 8, 8, d), 1)
    out = None
    for first in range(0, 8, 2 * h):
        row = jnp.broadcast_to(cum3[:, first + h - 1:first + h, :], cum3.shape)
        out = row if out is None else jnp.where(sub >= first, row, out)
    return out.reshape(c, d)


def _hgrn_prompt_kernel(ltri_ref, q_ref, k_ref, g_ref, v_ref, og_ref, gn_ref, ob_ref, sout_ref, st_ref, *, ct):
    c = HGRN_CHUNK
    tstep = pl.program_id(1)

    @pl.when(tstep == 0)
    def _():
        st_ref[...] = jnp.zeros_like(st_ref)

    tt = lax.broadcasted_iota(jnp.int32, (c, c), 0)
    ss = lax.broadcasted_iota(jnp.int32, (c, c), 1)
    meet = [((tt // (2 * h)) == (ss // (2 * h))) & (((tt // h) % 2) == 1) & (((ss // h) % 2) == 0)
            for h in HGRN_LEVELS]
    eye = tt == ss
    ltri = ltri_ref[...]
    gn = gn_ref[...]

    def chunk(ci, carry):
        rows = pl.ds(pl.multiple_of(ci * c, c), c)
        g = g_ref[0, rows, :]
        g_hi = g.astype(BF16)
        r1 = g - g_hi.astype(F32)
        g_mid = r1.astype(BF16)
        g_lo = (r1 - g_mid.astype(F32)).astype(BF16)
        cum = (jnp.dot(ltri, g_hi, preferred_element_type=F32)
               + jnp.dot(ltri, g_mid, preferred_element_type=F32)
               + jnp.dot(ltri, g_lo, preferred_element_type=F32))
        q = q_ref[0, rows, :]
        k = k_ref[0, rows, :]
        v = v_ref[0, rows, :]
        og = og_ref[0, rows, :]
        q_in = q * jnp.exp(cum)
        k_out = k * jnp.exp(-jnp.abs(cum - cum[c - 1:c, :]))
        d_all = jnp.exp(cum[c - 1:c, :])
        ex = [jnp.exp(-jnp.abs(cum - _hgrn_boundary_rows(cum, h))) for h in HGRN_LEVELS]
        for h in range(N_HEADS):
            hs = slice(h * HEAD_W, (h + 1) * HEAD_W)
            qh = q[:, hs]
            kh = k[:, hs]
            a = jnp.where(eye, jnp.sum(qh * kh, axis=-1, keepdims=True), 0.0)
            for li in range(len(HGRN_LEVELS)):
                e = ex[li][:, hs]
                al = lax.dot_general((qh * e).astype(BF16), (kh * e).astype(BF16), NT_DIMS,
                                     preferred_element_type=F32)
                a = a + jnp.where(meet[li], al, 0.0)
            vb = v[:, hs].astype(BF16)
            st = st_ref[h]
            o = (jnp.dot(a.astype(BF16), vb, preferred_element_type=F32)
                 + lax.dot_general(q_in[:, hs].astype(BF16), st.astype(BF16), NT_DIMS, preferred_element_type=F32))
            st_ref[h] = d_all[:, hs] * st + lax.dot_general(vb, k_out[:, hs].astype(BF16), TN_DIMS,
                                                            preferred_element_type=F32)
            ob_ref[0, rows, hs] = (_rms(o, gn) * og[:, hs]).astype(BF16)
        return carry

    lax.fori_loop(0, ct // c, chunk, 0)

    @pl.when(tstep == pl.num_programs(1) - 1)
    def _():
        for h in range(N_HEADS):
            sout_ref[0, h] = st_ref[h].T


def _hgrn_prompt(qh, kh, gh, vh, og, gn, ct=256):
    b, t, _ = qh.shape
    dall = jnp.asarray(np.tril(np.ones((HGRN_CHUNK, HGRN_CHUNK), np.float32)), dtype=BF16)
    tok_spec = pl.BlockSpec((1, ct, D_MODEL), lambda i, j: (i, j, 0))
    return pl.pallas_call(
        functools.partial(_hgrn_prompt_kernel, ct=ct),
        out_shape=(jax.ShapeDtypeStruct((b, t, D_MODEL), BF16),
                   jax.ShapeDtypeStruct((b, N_HEADS, HEAD_W, HEAD_W), F32)),
        grid=(b, t // ct),
        in_specs=[_const_spec(dall.shape), tok_spec, tok_spec, tok_spec, tok_spec, tok_spec, _const_spec((1, HEAD_W))],
        out_specs=(tok_spec, pl.BlockSpec((1, N_HEADS, HEAD_W, HEAD_W), lambda i, j: (i, 0, 0, 0))),
        scratch_shapes=[pltpu.VMEM((N_HEADS, HEAD_W, HEAD_W), F32)],
        compiler_params=_params(("parallel", "arbitrary")),
        name="hgrn_prompt",
    )(dall, qh, kh, gh, vh, og, gn)


def _lambda(lam_ref, lam_init):
    lv = lam_ref[...]
    s1 = jnp.sum(lv[0:1] * lv[1:2], axis=-1, keepdims=True)
    s2 = jnp.sum(lv[2:3] * lv[3:4], axis=-1, keepdims=True)
    return jnp.exp(s1) - jnp.exp(s2) + lam_init


def _attn_prompt_step(lam_ref, g_ref, q_ref, k_ref, v_ref, o_ref, *, tq, tk, lam_init):
    qi = pl.program_id(2)
    q = q_ref[0, 0]
    lane = lax.broadcasted_iota(jnp.int32, (1, HEAD_W), 1)
    zero = jnp.zeros_like(q)
    qs = jnp.concatenate([jnp.where(lane < A_DH, q, zero), jnp.where(lane < A_DH, zero, q)], axis=0)
    rg = ATTN_ROW_GROUP
    ng = 2 * tq // rg

    def block(k0, carry, masked):
        ms, accs = carry
        out_m, out_a = [], []
        for g in range(ng):
            r0 = (g * rg) % tq
            width = min(tk, r0 + rg) if masked else tk
            rows = pl.ds(pl.multiple_of(k0, tk), width)
            s = lax.dot_general(qs[g * rg:(g + 1) * rg], k_ref[0, 0, rows, :], NT_DIMS,
                                preferred_element_type=F32)
            if masked:
                qpos = r0 + lax.broadcasted_iota(jnp.int32, (rg, width), 0)
                kpos = lax.broadcasted_iota(jnp.int32, (rg, width), 1)
                s = jnp.where(kpos <= qpos, s, NEG_BIG)
            m_new = jnp.maximum(ms[g], jnp.max(s, axis=-1, keepdims=True))
            alpha = jnp.exp2(ms[g] - m_new)
            p = jnp.exp2(s - jnp.tile(m_new, (1, width // HEAD_W)))
            pv = jnp.dot(p.astype(BF16), v_ref[0, 0, rows, :], preferred_element_type=F32)
            out_m.append(m_new)
            out_a.append(jnp.tile(alpha, (1, 2)) * accs[g] + pv)
        return tuple(out_m), tuple(out_a)

    carry = (tuple(jnp.full((rg, HEAD_W), NEG_BIG, F32) for _ in range(ng)),
             tuple(jnp.zeros((rg, 2 * HEAD_W), F32) for _ in range(ng)))
    carry = lax.fori_loop(0, qi, lambda i, c: block(i * tk, c, False), carry)
    carry = block(qi * tq, carry, True)
    acc = jnp.concatenate(carry[1], axis=0)
    o = acc[:, 0:HEAD_W] / acc[:, HEAD_W:2 * HEAD_W]
    lam = _lambda(lam_ref, lam_init)
    od = o[0:tq] - lam * o[tq:2 * tq]
    o_ref[0, 0] = _rms(od, g_ref[...]).astype(BF16)


def _mix_kernel(x_ref, gm_ref, oa_ref, ob_ref, ga_ref, gb_ref, wa_ref, wb_ref, wo_ref, o_ref):
    oa = jnp.concatenate([oa_ref[0, h] for h in range(N_HEADS)], axis=-1)
    ya = jnp.dot(oa, wa_ref[...], preferred_element_type=F32)
    yb = jnp.dot(ob_ref[0].astype(BF16), wb_ref[...], preferred_element_type=F32)
    merged = ga_ref[0] * ya + gb_ref[0] * yb
    mix = jnp.dot(merged.astype(BF16), wo_ref[...], preferred_element_type=F32)
    o_ref[0] = x_ref[0] + gm_ref[0] * mix


def _mix(x, gm, oa_hm, ob, ga, gb, wa, wb, wo, tm):
    b, t, _ = x.shape
    rows_mod = gm.shape[1]
    tok_spec = pl.BlockSpec((1, tm, D_MODEL), lambda i, j: (i, j, 0))
    mod_spec = (pl.BlockSpec((1, 1, D_MODEL), lambda i, j: (i, 0, 0)) if rows_mod == 1 else tok_spec)
    return pl.pallas_call(
        _mix_kernel,
        out_shape=jax.ShapeDtypeStruct((b, t, D_MODEL), F32),
        grid=(b, t // tm),
        in_specs=[tok_spec, mod_spec, pl.BlockSpec((1, N_HEADS, tm, HEAD_W), lambda i, j: (i, 0, j, 0)),
                  tok_spec, tok_spec, tok_spec,
                  _const_spec(wa.shape), _const_spec(wb.shape), _const_spec(wo.shape)],
        out_specs=tok_spec,
        compiler_params=_params(("parallel", "parallel")),
        name="mix_out",
    )(x, gm, oa_hm, ob, ga, gb, wa, wb, wo)


def _ffn_kernel(*refs, tm, carried):
    if carried:
        (x_ref, sh_ref, sc_ref, gt_ref, g_ref, wup_ref, cw_ref, cb_ref, wdn_ref, y_ref, tail_ref, prev_ref) = refs
    else:
        (x_ref, sh_ref, sc_ref, gt_ref, g_ref, wup_ref, cw_ref, cb_ref, wdn_ref, b0_ref, b1_ref, y_ref, u_ref) = refs
    x = x_ref[0]
    hn = (_rms(x, g_ref[...]) * (1.0 + sc_ref[0]) + sh_ref[0]).astype(BF16)
    if carried:
        @pl.when(pl.program_id(1) == 0)
        def _():
            prev_ref[...] = jnp.zeros_like(prev_ref)
        row = lax.broadcasted_iota(jnp.int32, (tm, 1), 0)

    def conv(u, cols):
        if carried:
            p6 = prev_ref[6:7, cols]
            p7 = prev_ref[7:8, cols]
            u1 = jnp.where(row == 0, p7, pltpu.roll(u, 1, 0))
            u2 = jnp.where(row == 0, p6, jnp.where(row == 1, p7, pltpu.roll(u, 2, 0)))
            prev_ref[:, cols] = u[tm - 8:tm, :]
            tail_ref[0, :, cols] = u[tm - 8:tm, :]
        else:
            u2 = b0_ref[0, :, cols]
            u1 = b1_ref[0, :, cols]
            u_ref[0, :, cols] = u
        return cb_ref[:, cols] + cw_ref[0:1, cols] * u2 + cw_ref[1:2, cols] * u1 + cw_ref[2:3, cols] * u

    nchunk = D_FF // FFN_COLS

    def cols(c):
        return slice(c * FFN_COLS, (c + 1) * FFN_COLS), slice(D_FF + c * FFN_COLS, D_FF + (c + 1) * FFN_COLS)

    def up(c):
        ca, cg = cols(c)
        return (jnp.dot(hn, wup_ref[:, ca], preferred_element_type=F32),
                jnp.dot(hn, wup_ref[:, cg], preferred_element_type=F32))

    acc = jnp.zeros((tm, D_MODEL), F32)
    ups = [up(0), up(1)]
    acts = []
    for c in range(nchunk):
        ua, ug = ups[c]
        if c + 2 < nchunk:
            ups.append(up(c + 2))
        ca, cg = cols(c)
        acts.append((jax.nn.silu(conv(ua, ca)) * conv(ug, cg)).astype(BF16))
        if c >= 1:
            acc = acc + jnp.dot(acts[c - 1], wdn_ref[cols(c - 1)[0], :], preferred_element_type=F32)
    acc = acc + jnp.dot(acts[nchunk - 1], wdn_ref[cols(nchunk - 1)[0], :], preferred_element_type=F32)
    y_ref[0] = x + gt_ref[0] * acc


def _ffn(x1, sh, sc, gt, g, wup, cw, cb, wdn, tm, bufs=None):
    b, t, _ = x1.shape
    carried = bufs is None
    tok_spec = pl.BlockSpec((1, tm, D_MODEL), lambda i, j: (i, j, 0))
    mod_spec = (pl.BlockSpec((1, 1, D_MODEL), lambda i, j: (i, 0, 0)) if sh.shape[1] == 1 else tok_spec)
    wide_spec = pl.BlockSpec((1, tm, 2 * D_FF), lambda i, j: (i, j, 0))
    in_specs = [tok_spec, mod_spec, mod_spec, mod_spec, _const_spec((1, D_MODEL)), _const_spec(wup.shape),
                _const_spec(cw.shape), _const_spec(cb.shape), _const_spec(wdn.shape)]
    args = [x1, sh, sc, gt, g, wup, cw, cb, wdn]
    if carried:
        out_shape = (jax.ShapeDtypeStruct((b, t, D_MODEL), F32), jax.ShapeDtypeStruct((b, 8, 2 * D_FF), F32))
        out_specs = (tok_spec, pl.BlockSpec((1, 8, 2 * D_FF), lambda i, j: (i, 0, 0)))
        scratch = [pltpu.VMEM((8, 2 * D_FF), F32)]
        sem = ("parallel", "arbitrary")
    else:
        in_specs += [wide_spec, wide_spec]
        args += list(bufs)
        out_shape = (jax.ShapeDtypeStruct((b, t, D_MODEL), F32), jax.ShapeDtypeStruct((b, t, 2 * D_FF), F32))
        out_specs = (tok_spec, wide_spec)
        scratch = []
        sem = ("parallel", "parallel")
    return pl.pallas_call(
        functools.partial(_ffn_kernel, tm=tm, carried=carried),
        out_shape=out_shape,
        grid=(b, t // tm),
        in_specs=in_specs,
        out_specs=out_specs,
        scratch_shapes=scratch,
        compiler_params=_params(sem),
        name="conv_ffn",
    )(*args)


def _hgrn_sample_kernel(ft_ref, qt_ref, v_ref, og_ref, gn_ref, s_ref, so_ref, o_ref, *, nb):
    kc = pl.program_id(1)

    @pl.when(kc == 0)
    def _():
        o_ref[...] = jnp.zeros_like(o_ref)

    ft = ft_ref[...]
    qt = qt_ref[...]
    for b in range(nb):
        f = ft[:, b:b + 1]
        s_new = f * s_ref[b, 0] + (1.0 - f) * v_ref[b:b + 1, :]
        so_ref[b, 0] = s_new
        o_ref[b:b + 1, :] += jnp.sum(qt[:, b:b + 1] * s_new, axis=0, keepdims=True)

    @pl.when(kc == pl.num_programs(1) - 1)
    def _():
        o_ref[...] = _rms(o_ref[...], gn_ref[...]) * og_ref[...]


def _hgrn_sample(ft, qt, vh, og, gn, state, rows=16):
    nb = vh.shape[0]
    nkc = HEAD_W // rows
    st_spec = pl.BlockSpec((nb, 1, rows, HEAD_W), lambda h, c: (0, h, c, 0))
    ch_spec = pl.BlockSpec((rows, nb), lambda h, c: (h * nkc + c, 0))
    hd_spec = pl.BlockSpec((nb, HEAD_W), lambda h, c: (0, h))
    return pl.pallas_call(
        functools.partial(_hgrn_sample_kernel, nb=nb),
        out_shape=(jax.ShapeDtypeStruct(state.shape, F32), jax.ShapeDtypeStruct((nb, D_MODEL), F32)),
        grid=(N_HEADS, nkc),
        in_specs=[ch_spec, ch_spec, hd_spec, hd_spec, _const_spec((1, HEAD_W)), st_spec],
        out_specs=(st_spec, hd_spec),
        compiler_params=_params(("parallel", "arbitrary")),
        name="hgrn_sample",
    )(ft, qt, vh, og, gn, state)


def _page_copies(pt_ref, ck_hbm, cv_hbm, kbuf, vbuf, sem, b, n_pages):
    sl = b % 2
    out = []
    for p in range(n_pages):
        pg = pt_ref[b, p]
        out.append(pltpu.make_async_copy(ck_hbm.at[pg], kbuf.at[sl, p], sem.at[0, sl]))
        out.append(pltpu.make_async_copy(cv_hbm.at[pg], vbuf.at[sl, p], sem.at[1, sl]))
    return out


def _attn_sample_one(b, lam_ref, g_ref, q_ref, kn_ref, vn_ref, o_ref, kbuf, vbuf, *, n_pages, page, lam_init):
    slot = b % 2
    q = q_ref[pl.ds(b, 1), :]
    r16 = lax.broadcasted_iota(jnp.int32, (2 * N_HEADS, D_MODEL), 0)
    ln = lax.broadcasted_iota(jnp.int32, (2 * N_HEADS, D_MODEL), 1)
    sel = ((ln // HEAD_W) == (r16 % N_HEADS)) & (((ln // A_DH) % 2) == (r16 // N_HEADS))
    qsel = jnp.where(sel, q, 0.0)
    qblk = qsel.astype(BF16)
    s = jnp.concatenate([jnp.dot(qblk, kbuf[slot, p].astype(BF16), preferred_element_type=F32)
                         for p in range(n_pages)], axis=1)
    s_new = jnp.sum(qsel * kn_ref[pl.ds(b, 1), :], axis=-1, keepdims=True)
    m = jnp.maximum(jnp.max(s, axis=-1, keepdims=True), s_new)
    e = jnp.exp2(s - m)
    e_new = jnp.exp2(s_new - m)
    inv_l = 1.0 / (jnp.sum(e, axis=-1, keepdims=True) + e_new)
    lam = _lambda(lam_ref, lam_init)
    pr = e * inv_l
    pr_new = e_new * inv_l
    pd = pr[0:N_HEADS] - lam * pr[N_HEADS:2 * N_HEADS]
    pd_new = pr_new[0:N_HEADS] - lam * pr_new[N_HEADS:2 * N_HEADS]

    per = HEAD_W // N_HEADS
    lane = lax.broadcasted_iota(jnp.int32, (N_HEADS, HEAD_W), 1)
    own = (lane % N_HEADS) == lax.broadcasted_iota(jnp.int32, (N_HEADS, HEAD_W), 0)
    pad = jnp.zeros((N_HEADS, HEAD_W), F32)

    acc = jnp.zeros((2 * N_HEADS, HEAD_W), F32)
    for p in range(n_pages):
        w = pd[:, p * page:(p + 1) * page]
        for g in range(page // per):
            wx = jnp.take_along_axis(w, g * per + lane // N_HEADS, axis=1)
            lhs = jnp.concatenate([jnp.where(own, wx, 0.0), pad], axis=0).astype(BF16)
            rhs = vbuf[slot, p, pl.ds(g * per, per)].reshape(HEAD_W, HEAD_W).astype(BF16)
            acc = acc + jnp.dot(lhs, rhs, preferred_element_type=F32)
    o_ref[b] = _rms(acc[0:N_HEADS] + pd_new * vn_ref[b], g_ref[...])


def _attn_kernel(pt_ref, lam_ref, g_ref, q_ref, k_ref, v_ref, qs_ref, kn_ref, vn_ref, ck_hbm, cv_hbm,
                 o_ref, os_ref, kbuf, vbuf, sem, *, tq, lam_init, n_pages, page, per_step):
    step = (pl.program_id(0) * pl.num_programs(1) + pl.program_id(1)) * pl.num_programs(2) + pl.program_id(2)
    n_steps = pl.num_programs(0) * pl.num_programs(1) * pl.num_programs(2)
    first = step * per_step
    copies = functools.partial(_page_copies, pt_ref, ck_hbm, cv_hbm, kbuf, vbuf, sem, n_pages=n_pages)

    @pl.when(step == 0)
    def _():
        for cp in copies(0):
            cp.start()

    if per_step == 1:
        @pl.when(step + 1 < n_steps)
        def _():
            for cp in copies(step + 1):
                cp.start()

    _attn_prompt_step(lam_ref, g_ref, q_ref, k_ref, v_ref, o_ref, tq=tq, tk=tq, lam_init=lam_init)

    for j in range(per_step):
        b = first + j
        if per_step > 1:
            @pl.when(b + 1 < n_steps * per_step)
            def _():
                for cp in copies(b + 1):
                    cp.start()
        for cp in copies(b):
            cp.wait()
        _attn_sample_one(b, lam_ref, g_ref, qs_ref, kn_ref, vn_ref, os_ref, kbuf, vbuf,
                         n_pages=n_pages, page=page, lam_init=lam_init)


def _attention(page_table, lam4, g_sub, q_hm, k_hm, v_hm, q_s, k_new, v_new, cache_kt, cache_v, lam_init, tq=1024):
    b, nh, t, _ = q_hm.shape
    nb, n_pages = page_table.shape
    page = cache_v.shape[1]
    n_steps = b * nh * (t // tq)
    assert nb % n_steps == 0, (nb, n_steps)
    full = lambda shape: pl.BlockSpec(shape, lambda i, h, j, pt: (0,) * len(shape))
    any_spec = pl.BlockSpec(memory_space=pl.ANY)
    k_spec = pl.BlockSpec((1, 1, t, HEAD_W), lambda i, h, j, pt: (i, h, 0, 0))
    v_spec = pl.BlockSpec((1, 1, t, 2 * HEAD_W), lambda i, h, j, pt: (i, h, 0, 0))
    q_spec = pl.BlockSpec((1, 1, tq, HEAD_W), lambda i, h, j, pt: (i, h, j, 0))
    grid_spec = pltpu.PrefetchScalarGridSpec(
        num_scalar_prefetch=1,
        grid=(b, nh, t // tq),
        in_specs=[full(lam4.shape), full((1, HEAD_W)), q_spec, k_spec, v_spec,
                  full(q_s.shape), full(k_new.shape), full(v_new.shape), any_spec, any_spec],
        out_specs=(q_spec, full((nb, N_HEADS, HEAD_W))),
        scratch_shapes=[pltpu.VMEM((2, n_pages, D_MODEL, page), F32),
                        pltpu.VMEM((2, n_pages, page, N_HEADS, HEAD_W), F32),
                        pltpu.SemaphoreType.DMA((2, 2))],
    )
    return pl.pallas_call(
        functools.partial(_attn_kernel, tq=tq, lam_init=lam_init, n_pages=n_pages, page=page,
                          per_step=nb // n_steps),
        out_shape=(jax.ShapeDtypeStruct((b, nh, t, HEAD_W), BF16), jax.ShapeDtypeStruct((nb, N_HEADS, HEAD_W), F32)),
        grid_spec=grid_spec,
        compiler_params=_params(("arbitrary", "arbitrary", "arbitrary")),
        name="attention",
    )(page_table, lam4, g_sub, q_hm, k_hm, v_hm, q_s, k_new, v_new, cache_kt, cache_v)


def _rope_tables(pos):
    half = A_DH // 2
    inv = ROPE_THETA ** (-jnp.arange(half, dtype=F32) / half)
    ang = pos.astype(F32)[:, None] * inv[None, :]
    cos = jnp.cos(ang)
    sin = jnp.sin(ang)
    return jnp.tile(cos, (1, 4)), jnp.concatenate([-sin, sin, -sin, sin], axis=-1)


def kernel(x_prompt, x_sample, cache_k, cache_v, state_hgrn, state_conv, page_table, c_prompt, c_sample, w_ada, b_ada,
           rms_mix_g, w_in, q_norm_g, k_norm_g, lambda_q1, lambda_k1, lambda_q2, lambda_k2, subln_g, lb_logits,
           hgrn_norm_g, w_branch_a, w_branch_b, w_out, rms_ffn_g, w_up, conv_w, conv_b, w_down):
    bp, tp, d = x_prompt.shape
    bs, ts, _ = x_sample.shape
    assert d == D_MODEL and ts == 1 and w_in.shape[0] == 1
    n_pool, page = cache_k.shape[1], cache_k.shape[2]
    past_len = page_table.shape[1] * page
    lam_init = 0.8 - 0.6 * math.exp(-0.3 * 0)

    pad = (-(bp + bs)) % 8
    c_all = jnp.concatenate([c_prompt, c_sample, jnp.zeros((pad, d), F32)], axis=0)
    mod = _ada(c_all, w_ada[0], b_ada[0].reshape(1, -1))
    mp = [mod[:bp, i * d:(i + 1) * d].reshape(bp, 1, d) for i in range(6)]
    msamp = [mod[bp:bp + bs, i * d:(i + 1) * d] for i in range(6)]

    w_in_b = w_in[0].astype(BF16)
    wa_b = w_branch_a[0].astype(BF16)
    wb_b = w_branch_b[0].astype(BF16)
    wo_b = w_out[0].astype(BF16)
    wup_b = w_up[0].astype(BF16)
    wdn_b = w_down[0].astype(BF16)
    gmix = rms_mix_g[0].reshape(1, d)
    gffn = rms_ffn_g[0].reshape(1, d)
    qg = jnp.tile(q_norm_g[0], 2).reshape(1, HEAD_W)
    kg = jnp.tile(k_norm_g[0], 2).reshape(1, HEAD_W)
    lam4 = jnp.stack([lambda_q1[0], lambda_k1[0], lambda_q2[0], lambda_k2[0]], axis=0)
    g_sub = (subln_g[0] * (1.0 - lam_init)).reshape(1, HEAD_W)
    gn = hgrn_norm_g[0].reshape(1, HEAD_W)
    cw = conv_w[0]
    cb = conv_b[0].reshape(1, -1)

    cos_p, sin_p = _rope_tables(jnp.arange(tp))
    (q_hm, k_tok, k_hm, v_tok, v_hm, qh, kh, gh, vh, og, ga, gb) = _inproj_prompt(
        x_prompt, mp[0], mp[1], gmix, w_in_b, qg, kg, cos_p, sin_p, lb_logits)
    ob, s_prompt = _hgrn_prompt(qh, kh, gh, vh, og, gn)
    cos_s, sin_s = _rope_tables(jnp.full((1,), past_len))
    xs = x_sample.reshape(bs, d)
    (q_s, k_s, v_s, qt_s, ft_s, vh_s, og_s, ga_s, gb_s) = _inproj_sample(
        xs, msamp[0], msamp[1], gmix, w_in_b, qg, kg, cos_s, sin_s, lb_logits)
    s_sample, ob_s = _hgrn_sample(ft_s, qt_s, vh_s, og_s, gn, state_hgrn[0])

    cache_kt = jnp.transpose(cache_k[0], (0, 2, 3, 4, 1)).reshape(n_pool, d, page)
    oa_hm, oa_s = _attention(page_table, lam4, g_sub, q_hm, k_hm, v_hm, q_s, k_s, v_s.reshape(bs, N_HEADS, HEAD_W),
                             cache_kt, cache_v[0], lam_init)

    x1 = _mix(x_prompt, mp[2], oa_hm, ob, ga, gb, wa_b, wb_b, wo_b, tm=512)
    y_prompt, tail = _ffn(x1, mp[3], mp[4], mp[5], gffn, wup_b, cw, cb, wdn_b, tm=512)
    oa_s = jnp.transpose(oa_s, (1, 0, 2))
    x1_s = _mix(xs[None], msamp[2][None], oa_s[None].astype(BF16), ob_s[None], ga_s[None], gb_s[None],
                wa_b, wb_b, wo_b, tm=bs)
    y_s, u_s = _ffn(x1_s, msamp[3][None], msamp[4][None], msamp[5][None], gffn, wup_b, cw, cb, wdn_b, tm=bs,
                    bufs=(state_conv[0][:, 0][None], state_conv[0][:, 1][None]))

    new_conv_s = jnp.stack([state_conv[0][:, 1], u_s[0]], axis=1)
    return (y_prompt, y_s.reshape(bs, 1, d),
            k_tok.reshape(1, bp, tp, N_HEADS, 2, A_DH), v_tok.reshape(1, bp, tp, N_HEADS, HEAD_W),
            s_prompt[None], tail[:, 6:8][None],
            k_s.reshape(1, bs, 1, N_HEADS, 2, A_DH), v_s.reshape(1, bs, 1, N_HEADS, HEAD_W),
            s_sample[None], new_conv_s[None])
```

```python
import functools
import math

import numpy as np
import jax
import jax.numpy as jnp
from jax import lax
from jax.experimental import pallas as pl
from jax.experimental.pallas import tpu as pltpu

F32 = jnp.float32
BF16 = jnp.bfloat16

D_MODEL = 1024
N_HEADS = 8
HEAD_W = 128
A_DH = 64
D_FF = 2816
N_SEG = 9
ROPE_THETA = 10000.0
EPS = 1e-6
NEG_BIG = -1e30
HGRN_CHUNK = 64
HGRN_LEVELS = (32, 16, 8, 4, 2, 1)
FFN_COLS = 256
VMEM_LIMIT = 56 * 1024 * 1024
Q_SCALE = (A_DH ** -0.5) * math.log2(math.e)

ATTN_ROW_GROUP = 256
NT_DIMS = (((1,), (1,)), ((), ()))
TN_DIMS = (((0,), (0,)), ((), ()))


def _const_spec(shape):
    nd = len(shape)
    return pl.BlockSpec(shape, lambda *_: (0,) * nd, pipeline_mode=pl.Buffered(1))


def _params(sem, vmem=VMEM_LIMIT):
    return pltpu.CompilerParams(dimension_semantics=sem, vmem_limit_bytes=vmem)


def _rms(x, g):
    ms = jnp.mean(x * x, axis=-1, keepdims=True)
    return (x * lax.rsqrt(ms + EPS)) * g


def _ada_kernel(c_ref, w_ref, b_ref, o_ref):
    a = jax.nn.silu(c_ref[...]).astype(BF16)
    o_ref[...] = jnp.dot(a, w_ref[...].astype(BF16), preferred_element_type=F32) + b_ref[...]


def _ada(c_all, w_ada, b_ada):
    m = c_all.shape[0]
    n = w_ada.shape[1]
    tn = 1536
    return pl.pallas_call(
        _ada_kernel,
        out_shape=jax.ShapeDtypeStruct((m, n), F32),
        grid=(n // tn,),
        in_specs=[
            _const_spec((m, D_MODEL)),
            pl.BlockSpec((D_MODEL, tn), lambda j: (0, j)),
            pl.BlockSpec((1, tn), lambda j: (0, j)),
        ],
        out_specs=pl.BlockSpec((m, tn), lambda j: (0, j)),
        compiler_params=_params(("arbitrary",)),
        name="ada_mod",
    )(c_all, w_ada, b_ada)


def _qk_norm_rope(z, g128, cos, sin, lane):
    sq = z * z
    first = lane < A_DH
    lo = jnp.sum(jnp.where(first, sq, 0.0), axis=-1, keepdims=True)
    al = jnp.sum(sq, axis=-1, keepdims=True)
    ms = jnp.where(first, lo, al - lo) * (1.0 / A_DH)
    zn = (z * lax.rsqrt(ms + EPS)) * g128
    rot = jnp.where((lane % A_DH) < A_DH // 2, pltpu.roll(zn, HEAD_W - A_DH // 2, 1), pltpu.roll(zn, A_DH // 2, 1))
    return zn * cos + rot * sin


def _lower_bound(lbl_ref):
    lbl = lbl_ref[...]
    mx = jnp.max(lbl, axis=0, keepdims=True)
    e = jnp.exp(lbl - mx)
    return e[0:1, :] / jnp.sum(e, axis=0, keepdims=True)


def _inproj_common(x, sh, sc, gmix_ref, w_ref):
    xn = _rms(x, gmix_ref[...]) * (1.0 + sc) + sh
    xb = xn.astype(BF16)

    def seg(i):
        return jnp.dot(xb, w_ref[:, i * D_MODEL:(i + 1) * D_MODEL], preferred_element_type=F32)

    return seg


def _inproj_prompt_kernel(x_ref, sh_ref, sc_ref, gmix_ref, w_ref, qg_ref, kg_ref, cos_ref, sin_ref, lbl_ref,
                          qhm_ref, ktok_ref, khm_ref, vtok_ref, vhm_ref,
                          qh_ref, kh_ref, gh_ref, vh_ref, og_ref, ga_ref, gb_ref):
    seg = _inproj_common(x_ref[0], sh_ref[0], sc_ref[0], gmix_ref, w_ref)
    cos = cos_ref[...]
    sin = sin_ref[...]
    lane = lax.broadcasted_iota(jnp.int32, (1, HEAD_W), 1)
    zq = seg(0)
    zk = seg(1)
    for h in range(N_HEADS):
        hs = slice(h * HEAD_W, (h + 1) * HEAD_W)
        q = _qk_norm_rope(zq[:, hs], qg_ref[...], cos, sin, lane) * Q_SCALE
        qhm_ref[0, h] = q.astype(BF16)
        k = _qk_norm_rope(zk[:, hs], kg_ref[...], cos, sin, lane)
        ktok_ref[0, :, hs] = k
        khm_ref[0, h] = k.astype(BF16)
    zv = seg(2)
    vtok_ref[0] = zv
    ones = jnp.ones((zv.shape[0], HEAD_W), BF16)
    for h in range(N_HEADS):
        vhm_ref[0, h, :, 0:HEAD_W] = zv[:, h * HEAD_W:(h + 1) * HEAD_W].astype(BF16)
        vhm_ref[0, h, :, HEAD_W:2 * HEAD_W] = ones
    qh_ref[0] = jax.nn.silu(seg(3))
    lb = _lower_bound(lbl_ref)
    f = lb + (1.0 - lb) * jax.nn.sigmoid(seg(4))
    kh_ref[0] = 1.0 - f
    gh_ref[0] = jnp.log(f)
    vh_ref[0] = seg(5)
    og_ref[0] = jax.nn.silu(seg(6))
    ga_ref[0] = jax.nn.sigmoid(seg(7))
    gb_ref[0] = jax.nn.sigmoid(seg(8))


def _inproj_prompt(x, sh, sc, gmix, w_in, qg, kg, cos, sin, lbl, tm=256):
    b, t, _ = x.shape
    tok = jax.ShapeDtypeStruct((b, t, D_MODEL), F32)
    hm = jax.ShapeDtypeStruct((b, N_HEADS, t, HEAD_W), BF16)
    hm2 = jax.ShapeDtypeStruct((b, N_HEADS, t, 2 * HEAD_W), BF16)
    tok_spec = pl.BlockSpec((1, tm, D_MODEL), lambda i, j: (i, j, 0))
    hm_spec = pl.BlockSpec((1, N_HEADS, tm, HEAD_W), lambda i, j: (i, 0, j, 0))
    hm2_spec = pl.BlockSpec((1, N_HEADS, tm, 2 * HEAD_W), lambda i, j: (i, 0, j, 0))
    mod_spec = pl.BlockSpec((1, 1, D_MODEL), lambda i, j: (i, 0, 0))
    tab_spec = pl.BlockSpec((tm, HEAD_W), lambda i, j: (j, 0))
    return pl.pallas_call(
        _inproj_prompt_kernel,
        out_shape=(hm, tok, hm, tok, hm2, tok, tok, tok, tok, tok, tok, tok),
        grid=(b, t // tm),
        in_specs=[tok_spec, mod_spec, mod_spec, _const_spec((1, D_MODEL)), _const_spec(w_in.shape),
                  _const_spec((1, HEAD_W)), _const_spec((1, HEAD_W)), tab_spec, tab_spec, _const_spec(lbl.shape)],
        out_specs=(hm_spec, tok_spec, hm_spec, tok_spec, hm2_spec,
                   tok_spec, tok_spec, tok_spec, tok_spec, tok_spec, tok_spec, tok_spec),
        compiler_params=_params(("parallel", "parallel")),
        name="inproj_prompt",
    )(x, sh, sc, gmix, w_in, qg, kg, cos, sin, lbl)


def _inproj_sample_kernel(x_ref, sh_ref, sc_ref, gmix_ref, w_ref, qg_ref, kg_ref, cos_ref, sin_ref, lbl_ref,
                          q_ref, k_ref, v_ref, qt_ref, ft_ref, vh_ref, og_ref, ga_ref, gb_ref):
    seg = _inproj_common(x_ref[...], sh_ref[...], sc_ref[...], gmix_ref, w_ref)
    cos = cos_ref[...]
    sin = sin_ref[...]
    lane = lax.broadcasted_iota(jnp.int32, (1, HEAD_W), 1)
    zq = seg(0)
    zk = seg(1)
    for h in range(N_HEADS):
        hs = slice(h * HEAD_W, (h + 1) * HEAD_W)
        q_ref[:, hs] = _qk_norm_rope(zq[:, hs], qg_ref[...], cos, sin, lane) * Q_SCALE
        k_ref[:, hs] = _qk_norm_rope(zk[:, hs], kg_ref[...], cos, sin, lane)
    v_ref[...] = seg(2)
    qh = jax.nn.silu(seg(3))
    lb = _lower_bound(lbl_ref)
    f = lb + (1.0 - lb) * jax.nn.sigmoid(seg(4))
    for h in range(N_HEADS):
        hs = slice(h * HEAD_W, (h + 1) * HEAD_W)
        qt_ref[hs, :] = qh[:, hs].T
        ft_ref[hs, :] = f[:, hs].T
    vh_ref[...] = seg(5)
    og_ref[...] = jax.nn.silu(seg(6))
    ga_ref[...] = jax.nn.sigmoid(seg(7))
    gb_ref[...] = jax.nn.sigmoid(seg(8))


def _inproj_sample(x, sh, sc, gmix, w_in, qg, kg, cos, sin, lbl):
    m = x.shape[0]
    tok = jax.ShapeDtypeStruct((m, D_MODEL), F32)
    chan = jax.ShapeDtypeStruct((D_MODEL, m), F32)
    ts = _const_spec((m, D_MODEL))
    cs = _const_spec((D_MODEL, m))
    return pl.pallas_call(
        _inproj_sample_kernel,
        out_shape=(tok, tok, tok, chan, chan, tok, tok, tok, tok),
        grid=(1,),
        in_specs=[ts, ts, ts, _const_spec((1, D_MODEL)), _const_spec(w_in.shape),
                  _const_spec((1, HEAD_W)), _const_spec((1, HEAD_W)), _const_spec((1, HEAD_W)),
                  _const_spec((1, HEAD_W)), _const_spec(lbl.shape)],
        out_specs=(ts, ts, ts, cs, cs, ts, ts, ts, ts),
        compiler_params=_params(("arbitrary",)),
        name="inproj_sample",
    )(x, sh, sc, gmix, w_in, qg, kg, cos, sin, lbl)


def _hgrn_boundary_rows(cum, h):
    c, d = cum.shape
    if 2 * h >= 8:
        grp = 2 * h
        return jnp.concatenate(
            [jnp.broadcast_to(cum[i * grp + h - 1:i * grp + h, :], (grp, d)) for i in range(c // grp)], axis=0)
    cum3 = cum.reshape(c // 8, 8, d)
    sub = lax.broadcasted_iota(jnp.int32, (c // 8, 8, d), 1)
    out = None
    for first in range(0, 8, 2 * h):
        row = jnp.broadcast_to(cum3[:, first + h - 1:first + h, :], cum3.shape)
        out = row if out is None else jnp.where(sub >= first, row, out)
    return out.reshape(c, d)


def _hgrn_prompt_kernel(ltri_ref, q_ref, k_ref, g_ref, v_ref, og_ref, gn_ref, ob_ref, sout_ref, st_ref, *, ct):
    c = HGRN_CHUNK
    tstep = pl.program_id(1)

    @pl.when(tstep == 0)
    def _():
        st_ref[...] = jnp.zeros_like(st_ref)

    tt = lax.broadcasted_iota(jnp.int32, (c, c), 0)
    ss = lax.broadcasted_iota(jnp.int32, (c, c), 1)
    meet = [((tt // (2 * h)) == (ss // (2 * h))) & (((tt // h) % 2) == 1) & (((ss // h) % 2) == 0)
            for h in HGRN_LEVELS]
    eye = tt == ss
    ti = lax.broadcasted_iota(jnp.int32, (c, 1), 0)
    side = [jnp.where(((ti // h) % 2) == 1, 1.0, -1.0) for h in HGRN_LEVELS]
    ltri = ltri_ref[...]
    gn = gn_ref[...]

    def chunk(ci, carry):
        rows = pl.ds(pl.multiple_of(ci * c, c), c)
        g = g_ref[0, rows, :]
        g_hi = g.astype(BF16)
        r1 = g - g_hi.astype(F32)
        g_mid = r1.astype(BF16)
        g_lo = (r1 - g_mid.astype(F32)).astype(BF16)
        cum = (jnp.dot(ltri, g_hi, preferred_element_type=F32)
               + jnp.dot(ltri, g_mid, preferred_element_type=F32)
               + jnp.dot(ltri, g_lo, preferred_element_type=F32))
        q = q_ref[0, rows, :]
        k = k_ref[0, rows, :]
        v = v_ref[0, rows, :]
        og = og_ref[0, rows, :]
        q_in = q * jnp.exp(cum)
        k_out = k * jnp.exp(cum[c - 1:c, :] - cum)
        d_all = jnp.exp(cum[c - 1:c, :])
        ex = [jnp.exp((cum - _hgrn_boundary_rows(cum, h)) * side[li]).astype(BF16)
              for li, h in enumerate(HGRN_LEVELS)]
        qb = q.astype(BF16)
        kb = k.astype(BF16)
        for h in range(N_HEADS):
            hs = slice(h * HEAD_W, (h + 1) * HEAD_W)
            a = jnp.where(eye, jnp.sum(q[:, hs] * k[:, hs], axis=-1, keepdims=True), 0.0)
            for li in range(len(HGRN_LEVELS)):
                e = ex[li][:, hs]
                al = lax.dot_general(qb[:, hs] * e, kb[:, hs] * e, NT_DIMS, preferred_element_type=F32)
                a = jnp.where(meet[li], al, a)
            vb = v[:, hs].astype(BF16)
            st = st_ref[h]
            o = (jnp.dot(a.astype(BF16), vb, preferred_element_type=F32)
                 + lax.dot_general(q_in[:, hs].astype(BF16), st.astype(BF16), NT_DIMS, preferred_element_type=F32))
            st_ref[h] = d_all[:, hs] * st + lax.dot_general(vb, k_out[:, hs].astype(BF16), TN_DIMS,
                                                            preferred_element_type=F32)
            ob_ref[0, rows, hs] = (_rms(o, gn) * og[:, hs]).astype(BF16)
        return carry

    lax.fori_loop(0, ct // c, chunk, 0, unroll=True)

    @pl.when(tstep == pl.num_programs(1) - 1)
    def _():
        for h in range(N_HEADS):
            sout_ref[0, h] = st_ref[h].T


def _hgrn_prompt(qh, kh, gh, vh, og, gn, ct=256):
    b, t, _ = qh.shape
    dall = jnp.asarray(np.tril(np.ones((HGRN_CHUNK, HGRN_CHUNK), np.float32)), dtype=BF16)
    tok_spec = pl.BlockSpec((1, ct, D_MODEL), lambda i, j: (i, j, 0))
    return pl.pallas_call(
        functools.partial(_hgrn_prompt_kernel, ct=ct),
        out_shape=(jax.ShapeDtypeStruct((b, t, D_MODEL), BF16),
                   jax.ShapeDtypeStruct((b, N_HEADS, HEAD_W, HEAD_W), F32)),
        grid=(b, t // ct),
        in_specs=[_const_spec(dall.shape), tok_spec, tok_spec, tok_spec, tok_spec, tok_spec, _const_spec((1, HEAD_W))],
        out_specs=(tok_spec, pl.BlockSpec((1, N_HEADS, HEAD_W, HEAD_W), lambda i, j: (i, 0, 0, 0))),
        scratch_shapes=[pltpu.VMEM((N_HEADS, HEAD_W, HEAD_W), F32)],
        compiler_params=_params(("parallel", "arbitrary")),
        name="hgrn_prompt",
    )(dall, qh, kh, gh, vh, og, gn)


def _lambda(lam_ref, lam_init):
    lv = lam_ref[...]
    s1 = jnp.sum(lv[0:1] * lv[1:2], axis=-1, keepdims=True)
    s2 = jnp.sum(lv[2:3] * lv[3:4], axis=-1, keepdims=True)
    return jnp.exp(s1) - jnp.exp(s2) + lam_init


def _attn_prompt_step(lam_ref, g_ref, q_ref, k_ref, v_ref, o_ref, m_ref, acc_ref, *, tq, tk, lam_init):
    qi = pl.program_id(2)
    q = q_ref[0, 0]
    lane = lax.broadcasted_iota(jnp.int32, (1, HEAD_W), 1)
    zero = jnp.zeros_like(q)
    qs = jnp.concatenate([jnp.where(lane < A_DH, q, zero), jnp.where(lane < A_DH, zero, q)], axis=0)
    rg = ATTN_ROW_GROUP
    ng = 2 * tq // rg

    def block(k0, masked):
        for g in range(ng):
            r0 = (g * rg) % tq
            gr = slice(g * rg, (g + 1) * rg)
            width = min(tk, r0 + rg) if masked else tk
            rows = pl.ds(pl.multiple_of(k0, tk), width)
            s = lax.dot_general(qs[gr], k_ref[0, 0, rows, :], NT_DIMS, preferred_element_type=F32)
            if masked:
                qpos = r0 + lax.broadcasted_iota(jnp.int32, (rg, width), 0)
                kpos = lax.broadcasted_iota(jnp.int32, (rg, width), 1)
                s = jnp.where(kpos <= qpos, s, NEG_BIG)
            m_old = m_ref[gr, :]
            m_new = jnp.maximum(m_old, jnp.max(s, axis=-1, keepdims=True))
            alpha = jnp.exp2(m_old - m_new)
            p = jnp.exp2(s - jnp.tile(m_new, (1, width // HEAD_W)))
            pv = jnp.dot(p.astype(BF16), v_ref[0, 0, rows, :], preferred_element_type=F32)
            m_ref[gr, :] = m_new
            acc_ref[gr, :] = jnp.tile(alpha, (1, 2)) * acc_ref[gr, :] + pv

    m_ref[...] = jnp.full(m_ref.shape, NEG_BIG, F32)
    acc_ref[...] = jnp.zeros(acc_ref.shape, F32)

    def full_block(i, carry):
        block(i * tk, False)
        return carry

    lax.fori_loop(0, qi, full_block, 0)
    block(qi * tq, True)
    acc = acc_ref[...]
    o = acc[:, 0:HEAD_W] / acc[:, HEAD_W:2 * HEAD_W]
    lam = _lambda(lam_ref, lam_init)
    od = o[0:tq] - lam * o[tq:2 * tq]
    o_ref[0, 0] = _rms(od, g_ref[...]).astype(BF16)


def _mix_kernel(x_ref, gm_ref, oa_ref, ob_ref, ga_ref, gb_ref, wa_ref, wb_ref, wo_ref, o_ref):
    oa = jnp.concatenate([oa_ref[0, h] for h in range(N_HEADS)], axis=-1)
    ya = jnp.dot(oa, wa_ref[...], preferred_element_type=F32)
    yb = jnp.dot(ob_ref[0].astype(BF16), wb_ref[...], preferred_element_type=F32)
    merged = ga_ref[0] * ya + gb_ref[0] * yb
    mix = jnp.dot(merged.astype(BF16), wo_ref[...], preferred_element_type=F32)
    o_ref[0] = x_ref[0] + gm_ref[0] * mix


def _mix(x, gm, oa_hm, ob, ga, gb, wa, wb, wo, tm):
    b, t, _ = x.shape
    rows_mod = gm.shape[1]
    tok_spec = pl.BlockSpec((1, tm, D_MODEL), lambda i, j: (i, j, 0))
    mod_spec = (pl.BlockSpec((1, 1, D_MODEL), lambda i, j: (i, 0, 0)) if rows_mod == 1 else tok_spec)
    return pl.pallas_call(
        _mix_kernel,
        out_shape=jax.ShapeDtypeStruct((b, t, D_MODEL), F32),
        grid=(b, t // tm),
        in_specs=[tok_spec, mod_spec, pl.BlockSpec((1, N_HEADS, tm, HEAD_W), lambda i, j: (i, 0, j, 0)),
                  tok_spec, tok_spec, tok_spec,
                  _const_spec(wa.shape), _const_spec(wb.shape), _const_spec(wo.shape)],
        out_specs=tok_spec,
        compiler_params=_params(("parallel", "parallel")),
        name="mix_out",
    )(x, gm, oa_hm, ob, ga, gb, wa, wb, wo)


def _ffn_kernel(*refs, tm, carried):
    if carried:
        (x_ref, sh_ref, sc_ref, gt_ref, g_ref, wup_ref, cw_ref, cb_ref, wdn_ref, y_ref, tail_ref, prev_ref) = refs
    else:
        (x_ref, sh_ref, sc_ref, gt_ref, g_ref, wup_ref, cw_ref, cb_ref, wdn_ref, b0_ref, b1_ref, y_ref, u_ref) = refs
    x = x_ref[0]
    hn = (_rms(x, g_ref[...]) * (1.0 + sc_ref[0]) + sh_ref[0]).astype(BF16)
    if carried:
        @pl.when(pl.program_id(1) == 0)
        def _():
            prev_ref[...] = jnp.zeros_like(prev_ref)
        row = lax.broadcasted_iota(jnp.int32, (tm, 1), 0)

    def conv(u, cols):
        if carried:
            p6 = prev_ref[6:7, cols]
            p7 = prev_ref[7:8, cols]
            u1 = jnp.where(row == 0, p7, pltpu.roll(u, 1, 0))
            u2 = jnp.where(row == 0, p6, jnp.where(row == 1, p7, pltpu.roll(u, 2, 0)))
            prev_ref[:, cols] = u[tm - 8:tm, :]
            tail_ref[0, :, cols] = u[tm - 8:tm, :]
        else:
            u2 = b0_ref[0, :, cols]
            u1 = b1_ref[0, :, cols]
            u_ref[0, :, cols] = u
        return cb_ref[:, cols] + cw_ref[0:1, cols] * u2 + cw_ref[1:2, cols] * u1 + cw_ref[2:3, cols] * u

    nchunk = D_FF // FFN_COLS

    def cols(c):
        return slice(c * FFN_COLS, (c + 1) * FFN_COLS), slice(D_FF + c * FFN_COLS, D_FF + (c + 1) * FFN_COLS)

    def up(c):
        ca, cg = cols(c)
        return (jnp.dot(hn, wup_ref[:, ca], preferred_element_type=F32),
                jnp.dot(hn, wup_ref[:, cg], preferred_element_type=F32))

    acc = jnp.zeros((tm, D_MODEL), F32)
    ups = [up(0), up(1)]
    acts = []
    for c in range(nchunk):
        ua, ug = ups[c]
        if c + 2 < nchunk:
            ups.append(up(c + 2))
        ca, cg = cols(c)
        acts.append((jax.nn.silu(conv(ua, ca)) * conv(ug, cg)).astype(BF16))
        if c >= 1:
            acc = acc + jnp.dot(acts[c - 1], wdn_ref[cols(c - 1)[0], :], preferred_element_type=F32)
    acc = acc + jnp.dot(acts[nchunk - 1], wdn_ref[cols(nchunk - 1)[0], :], preferred_element_type=F32)
    y_ref[0] = x + gt_ref[0] * acc


def _ffn(x1, sh, sc, gt, g, wup, cw, cb, wdn, tm, bufs=None):
    b, t, _ = x1.shape
    carried = bufs is None
    tok_spec = pl.BlockSpec((1, tm, D_MODEL), lambda i, j: (i, j, 0))
    mod_spec = (pl.BlockSpec((1, 1, D_MODEL), lambda i, j: (i, 0, 0)) if sh.shape[1] == 1 else tok_spec)
    wide_spec = pl.BlockSpec((1, tm, 2 * D_FF), lambda i, j: (i, j, 0))
    in_specs = [tok_spec, mod_spec, mod_spec, mod_spec, _const_spec((1, D_MODEL)), _const_spec(wup.shape),
                _const_spec(cw.shape), _const_spec(cb.shape), _const_spec(wdn.shape)]
    args = [x1, sh, sc, gt, g, wup, cw, cb, wdn]
    if carried:
        out_shape = (jax.ShapeDtypeStruct((b, t, D_MODEL), F32), jax.ShapeDtypeStruct((b, 8, 2 * D_FF), F32))
        out_specs = (tok_spec, pl.BlockSpec((1, 8, 2 * D_FF), lambda i, j: (i, 0, 0)))
        scratch = [pltpu.VMEM((8, 2 * D_FF), F32)]
        sem = ("parallel", "arbitrary")
    else:
        in_specs += [wide_spec, wide_spec]
        args += list(bufs)
        out_shape = (jax.ShapeDtypeStruct((b, t, D_MODEL), F32), jax.ShapeDtypeStruct((b, t, 2 * D_FF), F32))
        out_specs = (tok_spec, wide_spec)
        scratch = []
        sem = ("parallel", "parallel")
    return pl.pallas_call(
        functools.partial(_ffn_kernel, tm=tm, carried=carried),
        out_shape=out_shape,
        grid=(b, t // tm),
        in_specs=in_specs,
        out_specs=out_specs,
        scratch_shapes=scratch,
        compiler_params=_params(sem),
        name="conv_ffn",
    )(*args)


def _hgrn_sample_kernel(ft_ref, qt_ref, v_ref, og_ref, gn_ref, s_ref, so_ref, o_ref, *, nb):
    kc = pl.program_id(1)

    @pl.when(kc == 0)
    def _():
        o_ref[...] = jnp.zeros_like(o_ref)

    ft = ft_ref[...]
    qt = qt_ref[...]
    for b in range(nb):
        f = ft[:, b:b + 1]
        s_new = f * s_ref[b, 0] + (1.0 - f) * v_ref[b:b + 1, :]
        so_ref[b, 0] = s_new
        o_ref[b:b + 1, :] += jnp.sum(qt[:, b:b + 1] * s_new, axis=0, keepdims=True)

    @pl.when(kc == pl.num_programs(1) - 1)
    def _():
        o_ref[...] = _rms(o_ref[...], gn_ref[...]) * og_ref[...]


def _hgrn_sample(ft, qt, vh, og, gn, state, rows=16):
    nb = vh.shape[0]
    nkc = HEAD_W // rows
    st_spec = pl.BlockSpec((nb, 1, rows, HEAD_W), lambda h, c: (0, h, c, 0))
    ch_spec = pl.BlockSpec((rows, nb), lambda h, c: (h * nkc + c, 0))
    hd_spec = pl.BlockSpec((nb, HEAD_W), lambda h, c: (0, h))
    return pl.pallas_call(
        functools.partial(_hgrn_sample_kernel, nb=nb),
        out_shape=(jax.ShapeDtypeStruct(state.shape, F32), jax.ShapeDtypeStruct((nb, D_MODEL), F32)),
        grid=(N_HEADS, nkc),
        in_specs=[ch_spec, ch_spec, hd_spec, hd_spec, _const_spec((1, HEAD_W)), st_spec],
        out_specs=(st_spec, hd_spec),
        compiler_params=_params(("parallel", "arbitrary")),
        name="hgrn_sample",
    )(ft, qt, vh, og, gn, state)


def _page_copies(pt_ref, ck_hbm, cv_hbm, kbuf, vbuf, sem, b, n_pages):
    sl = b % 2
    out = []
    for p in range(n_pages):
        pg = pt_ref[b, p]
        out.append(pltpu.make_async_copy(ck_hbm.at[pg], kbuf.at[sl, p], sem.at[0, sl]))
        out.append(pltpu.make_async_copy(cv_hbm.at[pg], vbuf.at[sl, p], sem.at[1, sl]))
    return out


def _attn_sample_one(b, lam_ref, g_ref, q_ref, kn_ref, vn_ref, o_ref, kbuf, vbuf, *, n_pages, page, lam_init):
    slot = b % 2
    q = q_ref[pl.ds(b, 1), :]
    r16 = lax.broadcasted_iota(jnp.int32, (2 * N_HEADS, D_MODEL), 0)
    ln = lax.broadcasted_iota(jnp.int32, (2 * N_HEADS, D_MODEL), 1)
    sel = ((ln // HEAD_W) == (r16 % N_HEADS)) & (((ln // A_DH) % 2) == (r16 // N_HEADS))
    qsel = jnp.where(sel, q, 0.0)
    qblk = qsel.astype(BF16)
    s = jnp.concatenate([jnp.dot(qblk, kbuf[slot, p].astype(BF16), preferred_element_type=F32)
                         for p in range(n_pages)], axis=1)
    s_new = jnp.sum(qsel * kn_ref[pl.ds(b, 1), :], axis=-1, keepdims=True)
    m = jnp.maximum(jnp.max(s, axis=-1, keepdims=True), s_new)
    e = jnp.exp2(s - m)
    e_new = jnp.exp2(s_new - m)
    inv_l = 1.0 / (jnp.sum(e, axis=-1, keepdims=True) + e_new)
    lam = _lambda(lam_ref, lam_init)
    pr = e * inv_l
    pr_new = e_new * inv_l
    pd = pr[0:N_HEADS] - lam * pr[N_HEADS:2 * N_HEADS]
    pd_new = pr_new[0:N_HEADS] - lam * pr_new[N_HEADS:2 * N_HEADS]

    per = HEAD_W // N_HEADS
    lane = lax.broadcasted_iota(jnp.int32, (N_HEADS, HEAD_W), 1)
    own = (lane % N_HEADS) == lax.broadcasted_iota(jnp.int32, (N_HEADS, HEAD_W), 0)
    pad = jnp.zeros((N_HEADS, HEAD_W), F32)

    acc = jnp.zeros((2 * N_HEADS, HEAD_W), F32)
    for p in range(n_pages):
        w = pd[:, p * page:(p + 1) * page]
        for g in range(page // per):
            wx = jnp.take_along_axis(w, g * per + lane // N_HEADS, axis=1)
            lhs = jnp.concatenate([jnp.where(own, wx, 0.0), pad], axis=0).astype(BF16)
            rhs = vbuf[slot, p, pl.ds(g * per, per)].reshape(HEAD_W, HEAD_W).astype(BF16)
            acc = acc + jnp.dot(lhs, rhs, preferred_element_type=F32)
    o_ref[b] = _rms(acc[0:N_HEADS] + pd_new * vn_ref[b], g_ref[...])


def _attn_kernel(pt_ref, lam_ref, g_ref, q_ref, k_ref, v_ref, qs_ref, kn_ref, vn_ref, ck_hbm, cv_hbm,
                 o_ref, os_ref, kbuf, vbuf, m_ref, acc_ref, sem, *, tq, lam_init, n_pages, page, per_step):
    step = (pl.program_id(0) * pl.num_programs(1) + pl.program_id(1)) * pl.num_programs(2) + pl.program_id(2)
    n_steps = pl.num_programs(0) * pl.num_programs(1) * pl.num_programs(2)
    first = step * per_step
    copies = functools.partial(_page_copies, pt_ref, ck_hbm, cv_hbm, kbuf, vbuf, sem, n_pages=n_pages)

    @pl.when(step == 0)
    def _():
        for cp in copies(0):
            cp.start()

    if per_step == 1:
        @pl.when(step + 1 < n_steps)
        def _():
            for cp in copies(step + 1):
                cp.start()

    _attn_prompt_step(lam_ref, g_ref, q_ref, k_ref, v_ref, o_ref, m_ref, acc_ref, tq=tq, tk=tq, lam_init=lam_init)

    for j in range(per_step):
        b = first + j
        if per_step > 1:
            @pl.when(b + 1 < n_steps * per_step)
            def _():
                for cp in copies(b + 1):
                    cp.start()
        for cp in copies(b):
            cp.wait()
        _attn_sample_one(b, lam_ref, g_ref, qs_ref, kn_ref, vn_ref, os_ref, kbuf, vbuf,
                         n_pages=n_pages, page=page, lam_init=lam_init)


def _attention(page_table, lam4, g_sub, q_hm, k_hm, v_hm, q_s, k_new, v_new, cache_kt, cache_v, lam_init, tq=1024):
    b, nh, t, _ = q_hm.shape
    nb, n_pages = page_table.shape
    page = cache_v.shape[1]
    n_steps = b * nh * (t // tq)
    assert nb % n_steps == 0, (nb, n_steps)
    full = lambda shape: pl.BlockSpec(shape, lambda i, h, j, pt: (0,) * len(shape))
    any_spec = pl.BlockSpec(memory_space=pl.ANY)
    k_spec = pl.BlockSpec((1, 1, t, HEAD_W), lambda i, h, j, pt: (i, h, 0, 0))
    v_spec = pl.BlockSpec((1, 1, t, 2 * HEAD_W), lambda i, h, j, pt: (i, h, 0, 0))
    q_spec = pl.BlockSpec((1, 1, tq, HEAD_W), lambda i, h, j, pt: (i, h, j, 0))
    grid_spec = pltpu.PrefetchScalarGridSpec(
        num_scalar_prefetch=1,
        grid=(b, nh, t // tq),
        in_specs=[full(lam4.shape), full((1, HEAD_W)), q_spec, k_spec, v_spec,
                  full(q_s.shape), full(k_new.shape), full(v_new.shape), any_spec, any_spec],
        out_specs=(q_spec, full((nb, N_HEADS, HEAD_W))),
        scratch_shapes=[pltpu.VMEM((2, n_pages, D_MODEL, page), F32),
                        pltpu.VMEM((2, n_pages, page, N_HEADS, HEAD_W), F32),
                        pltpu.VMEM((2 * tq, HEAD_W), F32), pltpu.VMEM((2 * tq, 2 * HEAD_W), F32),
                        pltpu.SemaphoreType.DMA((2, 2))],
    )
    return pl.pallas_call(
        functools.partial(_attn_kernel, tq=tq, lam_init=lam_init, n_pages=n_pages, page=page,
                          per_step=nb // n_steps),
        out_shape=(jax.ShapeDtypeStruct((b, nh, t, HEAD_W), BF16), jax.ShapeDtypeStruct((nb, N_HEADS, HEAD_W), F32)),
        grid_spec=grid_spec,
        compiler_params=_params(("arbitrary", "arbitrary", "arbitrary")),
        name="attention",
    )(page_table, lam4, g_sub, q_hm, k_hm, v_hm, q_s, k_new, v_new, cache_kt, cache_v)


def _rope_tables(pos):
    half = A_DH // 2
    inv = ROPE_THETA ** (-jnp.arange(half, dtype=F32) / half)
    ang = pos.astype(F32)[:, None] * inv[None, :]
    cos = jnp.cos(ang)
    sin = jnp.sin(ang)
    return jnp.tile(cos, (1, 4)), jnp.concatenate([-sin, sin, -sin, sin], axis=-1)


def kernel(x_prompt, x_sample, cache_k, cache_v, state_hgrn, state_conv, page_table, c_prompt, c_sample, w_ada, b_ada,
           rms_mix_g, w_in, q_norm_g, k_norm_g, lambda_q1, lambda_k1, lambda_q2, lambda_k2, subln_g, lb_logits,
           hgrn_norm_g, w_branch_a, w_branch_b, w_out, rms_ffn_g, w_up, conv_w, conv_b, w_down):
    bp, tp, d = x_prompt.shape
    bs, ts, _ = x_sample.shape
    assert d == D_MODEL and ts == 1 and w_in.shape[0] == 1
    n_pool, page = cache_k.shape[1], cache_k.shape[2]
    past_len = page_table.shape[1] * page
    lam_init = 0.8 - 0.6 * math.exp(-0.3 * 0)

    pad = (-(bp + bs)) % 8
    c_all = jnp.concatenate([c_prompt, c_sample, jnp.zeros((pad, d), F32)], axis=0)
    mod = _ada(c_all, w_ada[0], b_ada[0].reshape(1, -1))
    mp = [mod[:bp, i * d:(i + 1) * d].reshape(bp, 1, d) for i in range(6)]
    msamp = [mod[bp:bp + bs, i * d:(i + 1) * d] for i in range(6)]

    w_in_b = w_in[0].astype(BF16)
    wa_b = w_branch_a[0].astype(BF16)
    wb_b = w_branch_b[0].astype(BF16)
    wo_b = w_out[0].astype(BF16)
    wup_b = w_up[0].astype(BF16)
    wdn_b = w_down[0].astype(BF16)
    gmix = rms_mix_g[0].reshape(1, d)
    gffn = rms_ffn_g[0].reshape(1, d)
    qg = jnp.tile(q_norm_g[0], 2).reshape(1, HEAD_W)
    kg = jnp.tile(k_norm_g[0], 2).reshape(1, HEAD_W)
    lam4 = jnp.stack([lambda_q1[0], lambda_k1[0], lambda_q2[0], lambda_k2[0]], axis=0)
    g_sub = (subln_g[0] * (1.0 - lam_init)).reshape(1, HEAD_W)
    gn = hgrn_norm_g[0].reshape(1, HEAD_W)
    cw = conv_w[0]
    cb = conv_b[0].reshape(1, -1)

    cos_p, sin_p = _rope_tables(jnp.arange(tp))
    (q_hm, k_tok, k_hm, v_tok, v_hm, qh, kh, gh, vh, og, ga, gb) = _inproj_prompt(
        x_prompt, mp[0], mp[1], gmix, w_in_b, qg, kg, cos_p, sin_p, lb_logits)
    ob, s_prompt = _hgrn_prompt(qh, kh, gh, vh, og, gn)
    cos_s, sin_s = _rope_tables(jnp.full((1,), past_len))
    xs = x_sample.reshape(bs, d)
    (q_s, k_s, v_s, qt_s, ft_s, vh_s, og_s, ga_s, gb_s) = _inproj_sample(
        xs, msamp[0], msamp[1], gmix, w_in_b, qg, kg, cos_s, sin_s, lb_logits)
    s_sample, ob_s = _hgrn_sample(ft_s, qt_s, vh_s, og_s, gn, state_hgrn[0])

    cache_kt = jnp.transpose(cache_k[0], (0, 2, 3, 4, 1)).reshape(n_pool, d, page)
    oa_hm, oa_s = _attention(page_table, lam4, g_sub, q_hm, k_hm, v_hm, q_s, k_s, v_s.reshape(bs, N_HEADS, HEAD_W),
                             cache_kt, cache_v[0], lam_init)

    x1 = _mix(x_prompt, mp[2], oa_hm, ob, ga, gb, wa_b, wb_b, wo_b, tm=512)
    y_prompt, tail = _ffn(x1, mp[3], mp[4], mp[5], gffn, wup_b, cw, cb, wdn_b, tm=512)
    oa_s = jnp.transpose(oa_s, (1, 0, 2))
    x1_s = _mix(xs[None], msamp[2][None], oa_s[None].astype(BF16), ob_s[None], ga_s[None], gb_s[None],
                wa_b, wb_b, wo_b, tm=bs)
    y_s, u_s = _ffn(x1_s, msamp[3][None], msamp[4][None], msamp[5][None], gffn, wup_b, cw, cb, wdn_b, tm=bs,
                    bufs=(state_conv[0][:, 0][None], state_conv[0][:, 1][None]))

    new_conv_s = jnp.stack([state_conv[0][:, 1], u_s[0]], axis=1)
    return (y_prompt, y_s.reshape(bs, 1, d),
            k_tok.reshape(1, bp, tp, N_HEADS, 2, A_DH), v_tok.reshape(1, bp, tp, N_HEADS, HEAD_W),
            s_prompt[None], tail[:, 6:8][None],
            k_s.reshape(1, bs, 1, N_HEADS, 2, A_DH), v_s.reshape(1, bs, 1, N_HEADS, HEAD_W),
            s_sample[None], new_conv_s[None])
```

```python
import functools
import math

import numpy as np
import jax
import jax.numpy as jnp
from jax import lax
from jax.experimental import pallas as pl
from jax.experimental.pallas import tpu as pltpu

F32 = jnp.float32
BF16 = jnp.bfloat16

D_MODEL = 1024
N_HEADS = 8
HEAD_W = 128
A_DH = 64
D_FF = 2816
ROPE_THETA = 10000.0
EPS = 1e-6
NEG_BIG = -1e30
HGRN_CHUNK = 64
HGRN_LEVELS = (32, 16, 8, 4, 2, 1)
FFN_COLS = 256
VMEM_LIMIT = 56 * 1024 * 1024
Q_SCALE = (A_DH ** -0.5) * math.log2(math.e)

ATTN_ROW_GROUP = 256
NT_DIMS = (((1,), (1,)), ((), ()))
TN_DIMS = (((0,), (0,)), ((), ()))


def _const_spec(shape):
    nd = len(shape)
    return pl.BlockSpec(shape, lambda *_: (0,) * nd, pipeline_mode=pl.Buffered(1))


def _params(sem, vmem=VMEM_LIMIT):
    return pltpu.CompilerParams(dimension_semantics=sem, vmem_limit_bytes=vmem)


def _rms(x, g):
    ms = jnp.mean(x * x, axis=-1, keepdims=True)
    return (x * lax.rsqrt(ms + EPS)) * g


def _ada_kernel(c_ref, w_ref, b_ref, o_ref):
    a = jax.nn.silu(c_ref[...]).astype(BF16)
    o_ref[...] = jnp.dot(a, w_ref[...].astype(BF16), preferred_element_type=F32) + b_ref[...]


def _ada(c_all, w_ada, b_ada):
    m = c_all.shape[0]
    n = w_ada.shape[1]
    tn = 1536
    return pl.pallas_call(
        _ada_kernel,
        out_shape=jax.ShapeDtypeStruct((m, n), F32),
        grid=(n // tn,),
        in_specs=[
            _const_spec((m, D_MODEL)),
            pl.BlockSpec((D_MODEL, tn), lambda j: (0, j)),
            pl.BlockSpec((1, tn), lambda j: (0, j)),
        ],
        out_specs=pl.BlockSpec((m, tn), lambda j: (0, j)),
        compiler_params=_params(("arbitrary",)),
        name="ada_mod",
    )(c_all, w_ada, b_ada)


def _qk_norm_rope(z, g128, cos, sin, lane):
    sq = z * z
    first = lane < A_DH
    lo = jnp.sum(jnp.where(first, sq, 0.0), axis=-1, keepdims=True)
    al = jnp.sum(sq, axis=-1, keepdims=True)
    ms = jnp.where(first, lo, al - lo) * (1.0 / A_DH)
    zn = (z * lax.rsqrt(ms + EPS)) * g128
    rot = jnp.where((lane % A_DH) < A_DH // 2, pltpu.roll(zn, HEAD_W - A_DH // 2, 1), pltpu.roll(zn, A_DH // 2, 1))
    return zn * cos + rot * sin


def _lower_bound(lbl_ref):
    lbl = lbl_ref[...]
    mx = jnp.max(lbl, axis=0, keepdims=True)
    e = jnp.exp(lbl - mx)
    return e[0:1, :] / jnp.sum(e, axis=0, keepdims=True)


def _inproj_common(x, sh, sc, gmix_ref, w_ref):
    xn = _rms(x, gmix_ref[...]) * (1.0 + sc) + sh
    xb = xn.astype(BF16)

    def seg(i):
        return jnp.dot(xb, w_ref[:, i * D_MODEL:(i + 1) * D_MODEL], preferred_element_type=F32)

    return seg


def _inproj_prompt_kernel(x_ref, sh_ref, sc_ref, gmix_ref, w_ref, qg_ref, kg_ref, cos_ref, sin_ref, lbl_ref,
                          qhm_ref, ktok_ref, khm_ref, vtok_ref, vhm_ref,
                          qh_ref, kh_ref, gh_ref, vh_ref, og_ref, ga_ref, gb_ref):
    seg = _inproj_common(x_ref[0], sh_ref[0], sc_ref[0], gmix_ref, w_ref)
    cos = cos_ref[...]
    sin = sin_ref[...]
    lane = lax.broadcasted_iota(jnp.int32, (1, HEAD_W), 1)
    zq = seg(0)
    zk = seg(1)
    for h in range(N_HEADS):
        hs = slice(h * HEAD_W, (h + 1) * HEAD_W)
        q = _qk_norm_rope(zq[:, hs], qg_ref[...], cos, sin, lane) * Q_SCALE
        qhm_ref[0, h] = q.astype(BF16)
        k = _qk_norm_rope(zk[:, hs], kg_ref[...], cos, sin, lane)
        ktok_ref[0, :, hs] = k
        khm_ref[0, h] = k.astype(BF16)
    zv = seg(2)
    vtok_ref[0] = zv
    ones = jnp.ones((zv.shape[0], HEAD_W), BF16)
    for h in range(N_HEADS):
        vhm_ref[0, h, :, 0:HEAD_W] = zv[:, h * HEAD_W:(h + 1) * HEAD_W].astype(BF16)
        vhm_ref[0, h, :, HEAD_W:2 * HEAD_W] = ones
    qh_ref[0] = jax.nn.silu(seg(3))
    lb = _lower_bound(lbl_ref)
    f = lb + (1.0 - lb) * jax.nn.sigmoid(seg(4))
    kh_ref[0] = 1.0 - f
    gh_ref[0] = jnp.log(f)
    vh_ref[0] = seg(5)
    og_ref[0] = jax.nn.silu(seg(6))
    ga_ref[0] = jax.nn.sigmoid(seg(7))
    gb_ref[0] = jax.nn.sigmoid(seg(8))


def _inproj_prompt(x, sh, sc, gmix, w_in, qg, kg, cos, sin, lbl, tm=256):
    b, t, _ = x.shape
    tok = jax.ShapeDtypeStruct((b, t, D_MODEL), F32)
    hm = jax.ShapeDtypeStruct((b, N_HEADS, t, HEAD_W), BF16)
    hm2 = jax.ShapeDtypeStruct((b, N_HEADS, t, 2 * HEAD_W), BF16)
    tok_spec = pl.BlockSpec((1, tm, D_MODEL), lambda i, j: (i, j, 0))
    hm_spec = pl.BlockSpec((1, N_HEADS, tm, HEAD_W), lambda i, j: (i, 0, j, 0))
    hm2_spec = pl.BlockSpec((1, N_HEADS, tm, 2 * HEAD_W), lambda i, j: (i, 0, j, 0))
    mod_spec = pl.BlockSpec((1, 1, D_MODEL), lambda i, j: (i, 0, 0))
    tab_spec = pl.BlockSpec((tm, HEAD_W), lambda i, j: (j, 0))
    return pl.pallas_call(
        _inproj_prompt_kernel,
        out_shape=(hm, tok, hm, tok, hm2, tok, tok, tok, tok, tok, tok, tok),
        grid=(b, t // tm),
        in_specs=[tok_spec, mod_spec, mod_spec, _const_spec((1, D_MODEL)), _const_spec(w_in.shape),
                  _const_spec((1, HEAD_W)), _const_spec((1, HEAD_W)), tab_spec, tab_spec, _const_spec(lbl.shape)],
        out_specs=(hm_spec, tok_spec, hm_spec, tok_spec, hm2_spec,
                   tok_spec, tok_spec, tok_spec, tok_spec, tok_spec, tok_spec, tok_spec),
        compiler_params=_params(("parallel", "parallel")),
        name="inproj_prompt",
    )(x, sh, sc, gmix, w_in, qg, kg, cos, sin, lbl)


def _inproj_sample_kernel(x_ref, sh_ref, sc_ref, gmix_ref, w_ref, qg_ref, kg_ref, cos_ref, sin_ref, lbl_ref,
                          q_ref, k_ref, v_ref, qt_ref, ft_ref, vh_ref, og_ref, ga_ref, gb_ref):
    seg = _inproj_common(x_ref[...], sh_ref[...], sc_ref[...], gmix_ref, w_ref)
    cos = cos_ref[...]
    sin = sin_ref[...]
    lane = lax.broadcasted_iota(jnp.int32, (1, HEAD_W), 1)
    zq = seg(0)
    zk = seg(1)
    for h in range(N_HEADS):
        hs = slice(h * HEAD_W, (h + 1) * HEAD_W)
        q_ref[:, hs] = _qk_norm_rope(zq[:, hs], qg_ref[...], cos, sin, lane) * Q_SCALE
        k_ref[:, hs] = _qk_norm_rope(zk[:, hs], kg_ref[...], cos, sin, lane)
    v_ref[...] = seg(2)
    qh = jax.nn.silu(seg(3))
    lb = _lower_bound(lbl_ref)
    f = lb + (1.0 - lb) * jax.nn.sigmoid(seg(4))
    for h in range(N_HEADS):
        hs = slice(h * HEAD_W, (h + 1) * HEAD_W)
        qt_ref[hs, :] = qh[:, hs].T
        ft_ref[hs, :] = f[:, hs].T
    vh_ref[...] = seg(5)
    og_ref[...] = jax.nn.silu(seg(6))
    ga_ref[...] = jax.nn.sigmoid(seg(7))
    gb_ref[...] = jax.nn.sigmoid(seg(8))


def _inproj_sample(x, sh, sc, gmix, w_in, qg, kg, cos, sin, lbl):
    m = x.shape[0]
    tok = jax.ShapeDtypeStruct((m, D_MODEL), F32)
    chan = jax.ShapeDtypeStruct((D_MODEL, m), F32)
    ts = _const_spec((m, D_MODEL))
    cs = _const_spec((D_MODEL, m))
    return pl.pallas_call(
        _inproj_sample_kernel,
        out_shape=(tok, tok, tok, chan, chan, tok, tok, tok, tok),
        grid=(1,),
        in_specs=[ts, ts, ts, _const_spec((1, D_MODEL)), _const_spec(w_in.shape),
                  _const_spec((1, HEAD_W)), _const_spec((1, HEAD_W)), _const_spec((1, HEAD_W)),
                  _const_spec((1, HEAD_W)), _const_spec(lbl.shape)],
        out_specs=(ts, ts, ts, cs, cs, ts, ts, ts, ts),
        compiler_params=_params(("arbitrary",)),
        name="inproj_sample",
    )(x, sh, sc, gmix, w_in, qg, kg, cos, sin, lbl)


def _hgrn_boundary_rows(cum, h):
    c, d = cum.shape
    if 2 * h >= 8:
        grp = 2 * h
        return jnp.concatenate(
            [jnp.broadcast_to(cum[i * grp + h - 1:i * grp + h, :], (grp, d)) for i in range(c // grp)], axis=0)
    cum3 = cum.reshape(c // 8, 8, d)
    sub = lax.broadcasted_iota(jnp.int32, (c // 8, 8, d), 1)
    out = None
    for first in range(0, 8, 2 * h):
        row = jnp.broadcast_to(cum3[:, first + h - 1:first + h, :], cum3.shape)
        out = row if out is None else jnp.where(sub >= first, row, out)
    return out.reshape(c, d)


def _hgrn_prompt_kernel(ltri_ref, q_ref, k_ref, g_ref, v_ref, og_ref, gn_ref, ob_ref, sout_ref, st_ref, *, ct):
    c = HGRN_CHUNK
    tstep = pl.program_id(1)

    @pl.when(tstep == 0)
    def _():
        st_ref[...] = jnp.zeros_like(st_ref)

    tt = lax.broadcasted_iota(jnp.int32, (c, c), 0)
    ss = lax.broadcasted_iota(jnp.int32, (c, c), 1)
    meet = [((tt // (2 * h)) == (ss // (2 * h))) & (((tt // h) % 2) == 1) & (((ss // h) % 2) == 0)
            for h in HGRN_LEVELS]
    eye = tt == ss
    ti = lax.broadcasted_iota(jnp.int32, (c, 1), 0)
    side = [jnp.where(((ti // h) % 2) == 1, 1.0, -1.0) for h in HGRN_LEVELS]
    ltri = ltri_ref[...]
    gn = gn_ref[...]

    def chunk(ci, carry):
        rows = pl.ds(pl.multiple_of(ci * c, c), c)
        g = g_ref[0, rows, :]
        g_hi = g.astype(BF16)
        r1 = g - g_hi.astype(F32)
        g_mid = r1.astype(BF16)
        g_lo = (r1 - g_mid.astype(F32)).astype(BF16)
        cum = (jnp.dot(ltri, g_hi, preferred_element_type=F32)
               + jnp.dot(ltri, g_mid, preferred_element_type=F32)
               + jnp.dot(ltri, g_lo, preferred_element_type=F32))
        q = q_ref[0, rows, :]
        k = k_ref[0, rows, :]
        v = v_ref[0, rows, :]
        og = og_ref[0, rows, :]
        q_in = q * jnp.exp(cum)
        k_out = k * jnp.exp(cum[c - 1:c, :] - cum)
        d_all = jnp.exp(cum[c - 1:c, :])
        ex = [jnp.exp((cum - _hgrn_boundary_rows(cum, h)) * side[li]).astype(BF16)
              for li, h in enumerate(HGRN_LEVELS)]
        qb = q.astype(BF16)
        kb = k.astype(BF16)
        for h in range(N_HEADS):
            hs = slice(h * HEAD_W, (h + 1) * HEAD_W)
            a = jnp.where(eye, jnp.sum(q[:, hs] * k[:, hs], axis=-1, keepdims=True), 0.0)
            for li in range(len(HGRN_LEVELS)):
                e = ex[li][:, hs]
                al = lax.dot_general(qb[:, hs] * e, kb[:, hs] * e, NT_DIMS, preferred_element_type=F32)
                a = jnp.where(meet[li], al, a)
            vb = v[:, hs].astype(BF16)
            st = st_ref[h]
            o = (jnp.dot(a.astype(BF16), vb, preferred_element_type=F32)
                 + lax.dot_general(q_in[:, hs].astype(BF16), st.astype(BF16), NT_DIMS, preferred_element_type=F32))
            st_ref[h] = d_all[:, hs] * st + lax.dot_general(vb, k_out[:, hs].astype(BF16), TN_DIMS,
                                                            preferred_element_type=F32)
            ob_ref[0, rows, hs] = (_rms(o, gn) * og[:, hs]).astype(BF16)
        return carry

    lax.fori_loop(0, ct // c, chunk, 0, unroll=True)

    @pl.when(tstep == pl.num_programs(1) - 1)
    def _():
        for h in range(N_HEADS):
            sout_ref[0, h] = st_ref[h].T


def _hgrn_prompt(qh, kh, gh, vh, og, gn, ct=256):
    b, t, _ = qh.shape
    dall = jnp.asarray(np.tril(np.ones((HGRN_CHUNK, HGRN_CHUNK), np.float32)), dtype=BF16)
    tok_spec = pl.BlockSpec((1, ct, D_MODEL), lambda i, j: (i, j, 0))
    return pl.pallas_call(
        functools.partial(_hgrn_prompt_kernel, ct=ct),
        out_shape=(jax.ShapeDtypeStruct((b, t, D_MODEL), BF16),
                   jax.ShapeDtypeStruct((b, N_HEADS, HEAD_W, HEAD_W), F32)),
        grid=(b, t // ct),
        in_specs=[_const_spec(dall.shape), tok_spec, tok_spec, tok_spec, tok_spec, tok_spec, _const_spec((1, HEAD_W))],
        out_specs=(tok_spec, pl.BlockSpec((1, N_HEADS, HEAD_W, HEAD_W), lambda i, j: (i, 0, 0, 0))),
        scratch_shapes=[pltpu.VMEM((N_HEADS, HEAD_W, HEAD_W), F32)],
        compiler_params=_params(("parallel", "arbitrary")),
        name="hgrn_prompt",
    )(dall, qh, kh, gh, vh, og, gn)


def _lambda(lam_ref, lam_init):
    lv = lam_ref[...]
    s1 = jnp.sum(lv[0:1] * lv[1:2], axis=-1, keepdims=True)
    s2 = jnp.sum(lv[2:3] * lv[3:4], axis=-1, keepdims=True)
    return jnp.exp(s1) - jnp.exp(s2) + lam_init


def _attn_prompt_step(lam_ref, g_ref, q_ref, k_ref, v_ref, o_ref, m_ref, acc_ref, *, tq, tk, lam_init,
                      before_diagonal=None):
    qi = pl.program_id(2)
    q = q_ref[0, 0]
    lane = lax.broadcasted_iota(jnp.int32, (1, HEAD_W), 1)
    zero = jnp.zeros_like(q)
    qs = jnp.concatenate([jnp.where(lane < A_DH, q, zero), jnp.where(lane < A_DH, zero, q)], axis=0)
    rg = ATTN_ROW_GROUP
    ng = 2 * tq // rg

    def block(k0, masked):
        for g in range(ng):
            r0 = (g * rg) % tq
            gr = slice(g * rg, (g + 1) * rg)
            width = min(tk, r0 + rg) if masked else tk
            rows = pl.ds(pl.multiple_of(k0, tk), width)
            s = lax.dot_general(qs[gr], k_ref[0, 0, rows, :], NT_DIMS, preferred_element_type=F32)
            if masked:
                qpos = r0 + lax.broadcasted_iota(jnp.int32, (rg, width), 0)
                kpos = lax.broadcasted_iota(jnp.int32, (rg, width), 1)
                s = jnp.where(kpos <= qpos, s, NEG_BIG)
            m_old = m_ref[gr, :]
            m_new = jnp.maximum(m_old, jnp.max(s, axis=-1, keepdims=True))
            alpha = jnp.exp2(m_old - m_new)
            p = jnp.exp2(s - jnp.tile(m_new, (1, width // HEAD_W)))
            pv = jnp.dot(p.astype(BF16), v_ref[0, 0, rows, :], preferred_element_type=F32)
            m_ref[gr, :] = m_new
            acc_ref[gr, :] = jnp.tile(alpha, (1, 2)) * acc_ref[gr, :] + pv

    m_ref[...] = jnp.full(m_ref.shape, NEG_BIG, F32)
    acc_ref[...] = jnp.zeros(acc_ref.shape, F32)

    def full_block(i, carry):
        block(i * tk, False)
        return carry

    lax.fori_loop(0, qi, full_block, 0)
    if before_diagonal is not None:
        before_diagonal()
    block(qi * tq, True)
    acc = acc_ref[...]
    o = acc[:, 0:HEAD_W] / acc[:, HEAD_W:2 * HEAD_W]
    lam = _lambda(lam_ref, lam_init)
    od = o[0:tq] - lam * o[tq:2 * tq]
    o_ref[0, 0] = _rms(od, g_ref[...]).astype(BF16)


def _mix_kernel(x_ref, gm_ref, oa_ref, ob_ref, ga_ref, gb_ref, wa_ref, wb_ref, wo_ref, o_ref):
    oa = jnp.concatenate([oa_ref[0, h] for h in range(N_HEADS)], axis=-1)
    ya = jnp.dot(oa, wa_ref[...], preferred_element_type=F32)
    yb = jnp.dot(ob_ref[0].astype(BF16), wb_ref[...], preferred_element_type=F32)
    merged = ga_ref[0] * ya + gb_ref[0] * yb
    mix = jnp.dot(merged.astype(BF16), wo_ref[...], preferred_element_type=F32)
    o_ref[0] = x_ref[0] + gm_ref[0] * mix


def _mix(x, gm, oa_hm, ob, ga, gb, wa, wb, wo, tm):
    b, t, _ = x.shape
    rows_mod = gm.shape[1]
    tok_spec = pl.BlockSpec((1, tm, D_MODEL), lambda i, j: (i, j, 0))
    mod_spec = (pl.BlockSpec((1, 1, D_MODEL), lambda i, j: (i, 0, 0)) if rows_mod == 1 else tok_spec)
    return pl.pallas_call(
        _mix_kernel,
        out_shape=jax.ShapeDtypeStruct((b, t, D_MODEL), F32),
        grid=(b, t // tm),
        in_specs=[tok_spec, mod_spec, pl.BlockSpec((1, N_HEADS, tm, HEAD_W), lambda i, j: (i, 0, j, 0)),
                  tok_spec, tok_spec, tok_spec,
                  _const_spec(wa.shape), _const_spec(wb.shape), _const_spec(wo.shape)],
        out_specs=tok_spec,
        compiler_params=_params(("parallel", "parallel")),
        name="mix_out",
    )(x, gm, oa_hm, ob, ga, gb, wa, wb, wo)


def _ffn_kernel(*refs, tm, carried):
    if carried:
        (x_ref, sh_ref, sc_ref, gt_ref, g_ref, wup_ref, cw_ref, cb_ref, wdn_ref, y_ref, tail_ref, prev_ref) = refs
    else:
        (x_ref, sh_ref, sc_ref, gt_ref, g_ref, wup_ref, cw_ref, cb_ref, wdn_ref, b0_ref, b1_ref, y_ref, u_ref) = refs
    x = x_ref[0]
    hn = (_rms(x, g_ref[...]) * (1.0 + sc_ref[0]) + sh_ref[0]).astype(BF16)
    if carried:
        @pl.when(pl.program_id(1) == 0)
        def _():
            prev_ref[...] = jnp.zeros_like(prev_ref)
        row = lax.broadcasted_iota(jnp.int32, (tm, 1), 0)

    def conv(u, cols):
        if carried:
            p6 = prev_ref[6:7, cols]
            p7 = prev_ref[7:8, cols]
            u1 = jnp.where(row == 0, p7, pltpu.roll(u, 1, 0))
            u2 = jnp.where(row == 0, p6, jnp.where(row == 1, p7, pltpu.roll(u, 2, 0)))
            prev_ref[:, cols] = u[tm - 8:tm, :]
            tail_ref[0, :, cols] = u[tm - 8:tm, :]
        else:
            u2 = b0_ref[0, :, cols]
            u1 = b1_ref[0, :, cols]
            u_ref[0, :, cols] = u
        return cb_ref[:, cols] + cw_ref[0:1, cols] * u2 + cw_ref[1:2, cols] * u1 + cw_ref[2:3, cols] * u

    nchunk = D_FF // FFN_COLS

    def cols(c):
        return slice(c * FFN_COLS, (c + 1) * FFN_COLS), slice(D_FF + c * FFN_COLS, D_FF + (c + 1) * FFN_COLS)

    def up(c):
        ca, cg = cols(c)
        return (jnp.dot(hn, wup_ref[:, ca], preferred_element_type=F32),
                jnp.dot(hn, wup_ref[:, cg], preferred_element_type=F32))

    acc = jnp.zeros((tm, D_MODEL), F32)
    ups = [up(0), up(1)]
    acts = []
    for c in range(nchunk):
        ua, ug = ups[c]
        if c + 2 < nchunk:
            ups.append(up(c + 2))
        ca, cg = cols(c)
        acts.append((jax.nn.silu(conv(ua, ca)) * conv(ug, cg)).astype(BF16))
        if c >= 1:
            acc = acc + jnp.dot(acts[c - 1], wdn_ref[cols(c - 1)[0], :], preferred_element_type=F32)
    acc = acc + jnp.dot(acts[nchunk - 1], wdn_ref[cols(nchunk - 1)[0], :], preferred_element_type=F32)
    y_ref[0] = x + gt_ref[0] * acc


def _ffn(x1, sh, sc, gt, g, wup, cw, cb, wdn, tm, bufs=None):
    b, t, _ = x1.shape
    carried = bufs is None
    tok_spec = pl.BlockSpec((1, tm, D_MODEL), lambda i, j: (i, j, 0))
    mod_spec = (pl.BlockSpec((1, 1, D_MODEL), lambda i, j: (i, 0, 0)) if sh.shape[1] == 1 else tok_spec)
    wide_spec = pl.BlockSpec((1, tm, 2 * D_FF), lambda i, j: (i, j, 0))
    in_specs = [tok_spec, mod_spec, mod_spec, mod_spec, _const_spec((1, D_MODEL)), _const_spec(wup.shape),
                _const_spec(cw.shape), _const_spec(cb.shape), _const_spec(wdn.shape)]
    args = [x1, sh, sc, gt, g, wup, cw, cb, wdn]
    if carried:
        out_shape = (jax.ShapeDtypeStruct((b, t, D_MODEL), F32), jax.ShapeDtypeStruct((b, 8, 2 * D_FF), F32))
        out_specs = (tok_spec, pl.BlockSpec((1, 8, 2 * D_FF), lambda i, j: (i, 0, 0)))
        scratch = [pltpu.VMEM((8, 2 * D_FF), F32)]
        sem = ("parallel", "arbitrary")
    else:
        in_specs += [wide_spec, wide_spec]
        args += list(bufs)
        out_shape = (jax.ShapeDtypeStruct((b, t, D_MODEL), F32), jax.ShapeDtypeStruct((b, t, 2 * D_FF), F32))
        out_specs = (tok_spec, wide_spec)
        scratch = []
        sem = ("parallel", "parallel")
    return pl.pallas_call(
        functools.partial(_ffn_kernel, tm=tm, carried=carried),
        out_shape=out_shape,
        grid=(b, t // tm),
        in_specs=in_specs,
        out_specs=out_specs,
        scratch_shapes=scratch,
        compiler_params=_params(sem),
        name="conv_ffn",
    )(*args)


def _hgrn_sample_kernel(ft_ref, qt_ref, v_ref, og_ref, gn_ref, s_ref, so_ref, o_ref, *, nb):
    kc = pl.program_id(1)

    @pl.when(kc == 0)
    def _():
        o_ref[...] = jnp.zeros_like(o_ref)

    ft = ft_ref[...]
    qt = qt_ref[...]
    for b in range(nb):
        f = ft[:, b:b + 1]
        s_new = f * s_ref[b, 0] + (1.0 - f) * v_ref[b:b + 1, :]
        so_ref[b, 0] = s_new
        o_ref[b:b + 1, :] += jnp.sum(qt[:, b:b + 1] * s_new, axis=0, keepdims=True)

    @pl.when(kc == pl.num_programs(1) - 1)
    def _():
        o_ref[...] = _rms(o_ref[...], gn_ref[...]) * og_ref[...]


def _hgrn_sample(ft, qt, vh, og, gn, state, rows=16):
    nb = vh.shape[0]
    nkc = HEAD_W // rows
    st_spec = pl.BlockSpec((nb, 1, rows, HEAD_W), lambda h, c: (0, h, c, 0))
    ch_spec = pl.BlockSpec((rows, nb), lambda h, c: (h * nkc + c, 0))
    hd_spec = pl.BlockSpec((nb, HEAD_W), lambda h, c: (0, h))
    return pl.pallas_call(
        functools.partial(_hgrn_sample_kernel, nb=nb),
        out_shape=(jax.ShapeDtypeStruct(state.shape, F32), jax.ShapeDtypeStruct((nb, D_MODEL), F32)),
        grid=(N_HEADS, nkc),
        in_specs=[ch_spec, ch_spec, hd_spec, hd_spec, _const_spec((1, HEAD_W)), st_spec],
        out_specs=(st_spec, hd_spec),
        compiler_params=_params(("parallel", "arbitrary")),
        name="hgrn_sample",
    )(ft, qt, vh, og, gn, state)


def _page_copies(pt_ref, ck_hbm, cv_hbm, kbuf, vbuf, sem, b, n_pages):
    sl = b % 2
    out = []
    for p in range(n_pages):
        pg = pt_ref[b, p]
        out.append(pltpu.make_async_copy(ck_hbm.at[pg], kbuf.at[sl, p], sem.at[0, sl]))
        out.append(pltpu.make_async_copy(cv_hbm.at[pg], vbuf.at[sl, p], sem.at[1, sl]))
    return out


def _attn_sample_one(b, lam_ref, g_ref, q_ref, kn_ref, vn_ref, o_ref, kbuf, vbuf, *, n_pages, page, lam_init):
    slot = b % 2
    q = q_ref[pl.ds(b, 1), :]
    r16 = lax.broadcasted_iota(jnp.int32, (2 * N_HEADS, D_MODEL), 0)
    ln = lax.broadcasted_iota(jnp.int32, (2 * N_HEADS, D_MODEL), 1)
    sel = ((ln // HEAD_W) == (r16 % N_HEADS)) & (((ln // A_DH) % 2) == (r16 // N_HEADS))
    qsel = jnp.where(sel, q, 0.0)
    qblk = qsel.astype(BF16)
    s = jnp.concatenate([jnp.dot(qblk, kbuf[slot, p].astype(BF16), preferred_element_type=F32)
                         for p in range(n_pages)], axis=1)
    s_new = jnp.sum(qsel * kn_ref[pl.ds(b, 1), :], axis=-1, keepdims=True)
    m = jnp.maximum(jnp.max(s, axis=-1, keepdims=True), s_new)
    e = jnp.exp2(s - m)
    e_new = jnp.exp2(s_new - m)
    inv_l = 1.0 / (jnp.sum(e, axis=-1, keepdims=True) + e_new)
    lam = _lambda(lam_ref, lam_init)
    pr = e * inv_l
    pr_new = e_new * inv_l
    pd = pr[0:N_HEADS] - lam * pr[N_HEADS:2 * N_HEADS]
    pd_new = pr_new[0:N_HEADS] - lam * pr_new[N_HEADS:2 * N_HEADS]

    per = HEAD_W // N_HEADS
    lane = lax.broadcasted_iota(jnp.int32, (N_HEADS, HEAD_W), 1)
    own = (lane % N_HEADS) == lax.broadcasted_iota(jnp.int32, (N_HEADS, HEAD_W), 0)
    pad = jnp.zeros((N_HEADS, HEAD_W), F32)

    acc = jnp.zeros((2 * N_HEADS, HEAD_W), F32)
    for p in range(n_pages):
        w = pd[:, p * page:(p + 1) * page]
        for g in range(page // per):
            wx = jnp.take_along_axis(w, g * per + lane // N_HEADS, axis=1)
            lhs = jnp.concatenate([jnp.where(own, wx, 0.0), pad], axis=0).astype(BF16)
            rhs = vbuf[slot, p, pl.ds(g * per, per)].reshape(HEAD_W, HEAD_W).astype(BF16)
            acc = acc + jnp.dot(lhs, rhs, preferred_element_type=F32)
    o_ref[b] = _rms(acc[0:N_HEADS] + pd_new * vn_ref[b], g_ref[...])


def _attn_kernel(pt_ref, lam_ref, g_ref, q_ref, k_ref, v_ref, qs_ref, kn_ref, vn_ref, ck_hbm, cv_hbm,
                 o_ref, os_ref, kbuf, vbuf, m_ref, acc_ref, sem, *, tq, lam_init, n_pages, page, per_step):
    step = (pl.program_id(0) * pl.num_programs(1) + pl.program_id(1)) * pl.num_programs(2) + pl.program_id(2)
    n_steps = pl.num_programs(0) * pl.num_programs(1) * pl.num_programs(2)
    first = step * per_step
    copies = functools.partial(_page_copies, pt_ref, ck_hbm, cv_hbm, kbuf, vbuf, sem, n_pages=n_pages)

    @pl.when(step == 0)
    def _():
        for cp in copies(0):
            cp.start()

    if per_step == 1:
        @pl.when(step + 1 < n_steps)
        def _():
            for cp in copies(step + 1):
                cp.start()

    def wait_first():
        for cp in copies(first):
            cp.wait()

    _attn_prompt_step(lam_ref, g_ref, q_ref, k_ref, v_ref, o_ref, m_ref, acc_ref, tq=tq, tk=tq, lam_init=lam_init,
                      before_diagonal=wait_first)

    for j in range(per_step):
        b = first + j
        if per_step > 1:
            @pl.when(b + 1 < n_steps * per_step)
            def _():
                for cp in copies(b + 1):
                    cp.start()
        if j > 0:
            for cp in copies(b):
                cp.wait()
        _attn_sample_one(b, lam_ref, g_ref, qs_ref, kn_ref, vn_ref, os_ref, kbuf, vbuf,
                         n_pages=n_pages, page=page, lam_init=lam_init)


def _attention(page_table, lam4, g_sub, q_hm, k_hm, v_hm, q_s, k_new, v_new, cache_kt, cache_v, lam_init, tq=1024):
    b, nh, t, _ = q_hm.shape
    nb, n_pages = page_table.shape
    page = cache_v.shape[1]
    n_steps = b * nh * (t // tq)
    assert nb % n_steps == 0, (nb, n_steps)
    full = lambda shape: pl.BlockSpec(shape, lambda i, h, j, pt: (0,) * len(shape))
    any_spec = pl.BlockSpec(memory_space=pl.ANY)
    k_spec = pl.BlockSpec((1, 1, t, HEAD_W), lambda i, h, j, pt: (i, h, 0, 0))
    v_spec = pl.BlockSpec((1, 1, t, 2 * HEAD_W), lambda i, h, j, pt: (i, h, 0, 0))
    q_spec = pl.BlockSpec((1, 1, tq, HEAD_W), lambda i, h, j, pt: (i, h, j, 0))
    grid_spec = pltpu.PrefetchScalarGridSpec(
        num_scalar_prefetch=1,
        grid=(b, nh, t // tq),
        in_specs=[full(lam4.shape), full((1, HEAD_W)), q_spec, k_spec, v_spec,
                  full(q_s.shape), full(k_new.shape), full(v_new.shape), any_spec, any_spec],
        out_specs=(q_spec, full((nb, N_HEADS, HEAD_W))),
        scratch_shapes=[pltpu.VMEM((2, n_pages, D_MODEL, page), F32),
                        pltpu.VMEM((2, n_pages, page, N_HEADS, HEAD_W), F32),
                        pltpu.VMEM((2 * tq, HEAD_W), F32), pltpu.VMEM((2 * tq, 2 * HEAD_W), F32),
                        pltpu.SemaphoreType.DMA((2, 2))],
    )
    return pl.pallas_call(
        functools.partial(_attn_kernel, tq=tq, lam_init=lam_init, n_pages=n_pages, page=page,
                          per_step=nb // n_steps),
        out_shape=(jax.ShapeDtypeStruct((b, nh, t, HEAD_W), BF16), jax.ShapeDtypeStruct((nb, N_HEADS, HEAD_W), F32)),
        grid_spec=grid_spec,
        compiler_params=_params(("arbitrary", "arbitrary", "arbitrary")),
        name="attention",
    )(page_table, lam4, g_sub, q_hm, k_hm, v_hm, q_s, k_new, v_new, cache_kt, cache_v)


def _rope_tables(pos):
    half = A_DH // 2
    inv = ROPE_THETA ** (-jnp.arange(half, dtype=F32) / half)
    ang = pos.astype(F32)[:, None] * inv[None, :]
    cos = jnp.cos(ang)
    sin = jnp.sin(ang)
    return jnp.tile(cos, (1, 4)), jnp.concatenate([-sin, sin, -sin, sin], axis=-1)


def kernel(x_prompt, x_sample, cache_k, cache_v, state_hgrn, state_conv, page_table, c_prompt, c_sample, w_ada, b_ada,
           rms_mix_g, w_in, q_norm_g, k_norm_g, lambda_q1, lambda_k1, lambda_q2, lambda_k2, subln_g, lb_logits,
           hgrn_norm_g, w_branch_a, w_branch_b, w_out, rms_ffn_g, w_up, conv_w, conv_b, w_down):
    bp, tp, d = x_prompt.shape
    bs, ts, _ = x_sample.shape
    assert d == D_MODEL and ts == 1 and w_in.shape[0] == 1
    n_pool, page = cache_k.shape[1], cache_k.shape[2]
    past_len = page_table.shape[1] * page
    lam_init = 0.8 - 0.6 * math.exp(-0.3 * 0)

    pad = (-(bp + bs)) % 8
    c_all = jnp.concatenate([c_prompt, c_sample, jnp.zeros((pad, d), F32)], axis=0)
    mod = _ada(c_all, w_ada[0], b_ada[0].reshape(1, -1))
    mp = [mod[:bp, i * d:(i + 1) * d].reshape(bp, 1, d) for i in range(6)]
    msamp = [mod[bp:bp + bs, i * d:(i + 1) * d] for i in range(6)]

    w_in_b = w_in[0].astype(BF16)
    wa_b = w_branch_a[0].astype(BF16)
    wb_b = w_branch_b[0].astype(BF16)
    wo_b = w_out[0].astype(BF16)
    wup_b = w_up[0].astype(BF16)
    wdn_b = w_down[0].astype(BF16)
    gmix = rms_mix_g[0].reshape(1, d)
    gffn = rms_ffn_g[0].reshape(1, d)
    qg = jnp.tile(q_norm_g[0], 2).reshape(1, HEAD_W)
    kg = jnp.tile(k_norm_g[0], 2).reshape(1, HEAD_W)
    lam4 = jnp.stack([lambda_q1[0], lambda_k1[0], lambda_q2[0], lambda_k2[0]], axis=0)
    g_sub = (subln_g[0] * (1.0 - lam_init)).reshape(1, HEAD_W)
    gn = hgrn_norm_g[0].reshape(1, HEAD_W)
    cw = conv_w[0]
    cb = conv_b[0].reshape(1, -1)

    cos_p, sin_p = _rope_tables(jnp.arange(tp))
    (q_hm, k_tok, k_hm, v_tok, v_hm, qh, kh, gh, vh, og, ga, gb) = _inproj_prompt(
        x_prompt, mp[0], mp[1], gmix, w_in_b, qg, kg, cos_p, sin_p, lb_logits)
    ob, s_prompt = _hgrn_prompt(qh, kh, gh, vh, og, gn)
    cos_s, sin_s = _rope_tables(jnp.full((1,), past_len))
    xs = x_sample.reshape(bs, d)
    (q_s, k_s, v_s, qt_s, ft_s, vh_s, og_s, ga_s, gb_s) = _inproj_sample(
        xs, msamp[0], msamp[1], gmix, w_in_b, qg, kg, cos_s, sin_s, lb_logits)
    s_sample, ob_s = _hgrn_sample(ft_s, qt_s, vh_s, og_s, gn, state_hgrn[0])

    cache_kt = jnp.transpose(cache_k[0], (0, 2, 3, 4, 1)).reshape(n_pool, d, page)
    oa_hm, oa_s = _attention(page_table, lam4, g_sub, q_hm, k_hm, v_hm, q_s, k_s, v_s.reshape(bs, N_HEADS, HEAD_W),
                             cache_kt, cache_v[0], lam_init)

    x1 = _mix(x_prompt, mp[2], oa_hm, ob, ga, gb, wa_b, wb_b, wo_b, tm=512)
    y_prompt, tail = _ffn(x1, mp[3], mp[4], mp[5], gffn, wup_b, cw, cb, wdn_b, tm=512)
    oa_s = jnp.transpose(oa_s, (1, 0, 2))
    x1_s = _mix(xs[None], msamp[2][None], oa_s[None].astype(BF16), ob_s[None], ga_s[None], gb_s[None],
                wa_b, wb_b, wo_b, tm=bs)
    y_s, u_s = _ffn(x1_s, msamp[3][None], msamp[4][None], msamp[5][None], gffn, wup_b, cw, cb, wdn_b, tm=bs,
                    bufs=(state_conv[0][:, 0][None], state_conv[0][:, 1][None]))

    new_conv_s = jnp.stack([state_conv[0][:, 1], u_s[0]], axis=1)
    return (y_prompt, y_s.reshape(bs, 1, d),
            k_tok.reshape(1, bp, tp, N_HEADS, 2, A_DH), v_tok.reshape(1, bp, tp, N_HEADS, HEAD_W),
            s_prompt[None], tail[:, 6:8][None],
            k_s.reshape(1, bs, 1, N_HEADS, 2, A_DH), v_s.reshape(1, bs, 1, N_HEADS, HEAD_W),
            s_sample[None], new_conv_s[None])
```

```python
import functools
import math

import numpy as np
import jax
import jax.numpy as jnp
from jax import lax
from jax.experimental import pallas as pl
from jax.experimental.pallas import tpu as pltpu

F32 = jnp.float32
BF16 = jnp.bfloat16

D_MODEL = 1024
N_HEADS = 8
HEAD_W = 128
A_DH = 64
D_FF = 2816
ROPE_THETA = 10000.0
EPS = 1e-6
NEG_BIG = -1e30
HGRN_CHUNK = 64
HGRN_LEVELS = (32, 16, 8, 4, 2, 1)
FFN_COLS = 256
VMEM_LIMIT = 56 * 1024 * 1024
Q_SCALE = (A_DH ** -0.5) * math.log2(math.e)

ATTN_ROW_GROUP = 256
NT_DIMS = (((1,), (1,)), ((), ()))
TN_DIMS = (((0,), (0,)), ((), ()))


def _const_spec(shape):
    nd = len(shape)
    return pl.BlockSpec(shape, lambda *_: (0,) * nd, pipeline_mode=pl.Buffered(1))


def _params(sem, vmem=VMEM_LIMIT):
    return pltpu.CompilerParams(dimension_semantics=sem, vmem_limit_bytes=vmem)


def _rms(x, g):
    ms = jnp.mean(x * x, axis=-1, keepdims=True)
    return (x * lax.rsqrt(ms + EPS)) * g


def _ada_kernel(c_ref, w_ref, b_ref, o_ref):
    a = jax.nn.silu(c_ref[...]).astype(BF16)
    o_ref[...] = jnp.dot(a, w_ref[...].astype(BF16), preferred_element_type=F32) + b_ref[...]


def _ada(c_all, w_ada, b_ada):
    m = c_all.shape[0]
    n = w_ada.shape[1]
    tn = 1536
    return pl.pallas_call(
        _ada_kernel,
        out_shape=jax.ShapeDtypeStruct((m, n), F32),
        grid=(n // tn,),
        in_specs=[
            _const_spec((m, D_MODEL)),
            pl.BlockSpec((D_MODEL, tn), lambda j: (0, j)),
            pl.BlockSpec((1, tn), lambda j: (0, j)),
        ],
        out_specs=pl.BlockSpec((m, tn), lambda j: (0, j)),
        compiler_params=_params(("arbitrary",)),
        name="ada_mod",
    )(c_all, w_ada, b_ada)


def _qk_norm_rope(z, g128, cos, sin, lane):
    sq = z * z
    first = lane < A_DH
    lo = jnp.sum(jnp.where(first, sq, 0.0), axis=-1, keepdims=True)
    al = jnp.sum(sq, axis=-1, keepdims=True)
    ms = jnp.where(first, lo, al - lo) * (1.0 / A_DH)
    zn = (z * lax.rsqrt(ms + EPS)) * g128
    rot = jnp.where((lane % A_DH) < A_DH // 2, pltpu.roll(zn, HEAD_W - A_DH // 2, 1), pltpu.roll(zn, A_DH // 2, 1))
    return zn * cos + rot * sin


def _lower_bound(lbl_ref):
    lbl = lbl_ref[...]
    mx = jnp.max(lbl, axis=0, keepdims=True)
    e = jnp.exp(lbl - mx)
    return e[0:1, :] / jnp.sum(e, axis=0, keepdims=True)


def _inproj_common(x, sh, sc, gmix_ref, w_ref):
    xn = _rms(x, gmix_ref[...]) * (1.0 + sc) + sh
    xb = xn.astype(BF16)

    def seg(i):
        return jnp.dot(xb, w_ref[:, i * D_MODEL:(i + 1) * D_MODEL], preferred_element_type=F32)

    return seg


def _inproj_prompt_kernel(x_ref, sh_ref, sc_ref, gmix_ref, w_ref, qg_ref, kg_ref, cos_ref, sin_ref, lbl_ref,
                          qhm_ref, ktok_ref, khm_ref, vtok_ref, vhm_ref,
                          qh_ref, kh_ref, gh_ref, vh_ref, og_ref, ga_ref, gb_ref):
    seg = _inproj_common(x_ref[0], sh_ref[0], sc_ref[0], gmix_ref, w_ref)
    cos = cos_ref[...]
    sin = sin_ref[...]
    lane = lax.broadcasted_iota(jnp.int32, (1, HEAD_W), 1)
    zq = seg(0)
    zk = seg(1)
    for h in range(N_HEADS):
        hs = slice(h * HEAD_W, (h + 1) * HEAD_W)
        q = _qk_norm_rope(zq[:, hs], qg_ref[...], cos, sin, lane) * Q_SCALE
        qhm_ref[0, h] = q.astype(BF16)
        k = _qk_norm_rope(zk[:, hs], kg_ref[...], cos, sin, lane)
        ktok_ref[0, :, hs] = k
        khm_ref[0, h] = k.astype(BF16)
    zv = seg(2)
    vtok_ref[0] = zv
    ones = jnp.ones((zv.shape[0], HEAD_W), BF16)
    for h in range(N_HEADS):
        vhm_ref[0, h, :, 0:HEAD_W] = zv[:, h * HEAD_W:(h + 1) * HEAD_W].astype(BF16)
        vhm_ref[0, h, :, HEAD_W:2 * HEAD_W] = ones
    qh_ref[0] = jax.nn.silu(seg(3))
    lb = _lower_bound(lbl_ref)
    f = lb + (1.0 - lb) * jax.nn.sigmoid(seg(4))
    kh_ref[0] = 1.0 - f
    gh_ref[0] = jnp.log(f)
    vh_ref[0] = seg(5)
    og_ref[0] = jax.nn.silu(seg(6))
    ga_ref[0] = jax.nn.sigmoid(seg(7))
    gb_ref[0] = jax.nn.sigmoid(seg(8))


def _inproj_prompt(x, sh, sc, gmix, w_in, qg, kg, cos, sin, lbl, tm=256):
    b, t, _ = x.shape
    tok = jax.ShapeDtypeStruct((b, t, D_MODEL), F32)
    hm = jax.ShapeDtypeStruct((b, N_HEADS, t, HEAD_W), BF16)
    hm2 = jax.ShapeDtypeStruct((b, N_HEADS, t, 2 * HEAD_W), BF16)
    tok_spec = pl.BlockSpec((1, tm, D_MODEL), lambda i, j: (i, j, 0))
    hm_spec = pl.BlockSpec((1, N_HEADS, tm, HEAD_W), lambda i, j: (i, 0, j, 0))
    hm2_spec = pl.BlockSpec((1, N_HEADS, tm, 2 * HEAD_W), lambda i, j: (i, 0, j, 0))
    mod_spec = pl.BlockSpec((1, 1, D_MODEL), lambda i, j: (i, 0, 0))
    tab_spec = pl.BlockSpec((tm, HEAD_W), lambda i, j: (j, 0))
    return pl.pallas_call(
        _inproj_prompt_kernel,
        out_shape=(hm, tok, hm, tok, hm2, tok, tok, tok, tok, tok, tok, tok),
        grid=(b, t // tm),
        in_specs=[tok_spec, mod_spec, mod_spec, _const_spec((1, D_MODEL)), _const_spec(w_in.shape),
                  _const_spec((1, HEAD_W)), _const_spec((1, HEAD_W)), tab_spec, tab_spec, _const_spec(lbl.shape)],
        out_specs=(hm_spec, tok_spec, hm_spec, tok_spec, hm2_spec,
                   tok_spec, tok_spec, tok_spec, tok_spec, tok_spec, tok_spec, tok_spec),
        compiler_params=_params(("parallel", "parallel")),
        name="inproj_prompt",
    )(x, sh, sc, gmix, w_in, qg, kg, cos, sin, lbl)


def _inproj_sample_kernel(x_ref, sh_ref, sc_ref, gmix_ref, w_ref, qg_ref, kg_ref, cos_ref, sin_ref, lbl_ref,
                          q_ref, k_ref, v_ref, qh_ref, ft_ref, vh_ref, og_ref, ga_ref, gb_ref):
    seg = _inproj_common(x_ref[...], sh_ref[...], sc_ref[...], gmix_ref, w_ref)
    cos = cos_ref[...]
    sin = sin_ref[...]
    lane = lax.broadcasted_iota(jnp.int32, (1, HEAD_W), 1)
    zq = seg(0)
    zk = seg(1)
    for h in range(N_HEADS):
        hs = slice(h * HEAD_W, (h + 1) * HEAD_W)
        q_ref[:, hs] = _qk_norm_rope(zq[:, hs], qg_ref[...], cos, sin, lane) * Q_SCALE
        k_ref[:, hs] = _qk_norm_rope(zk[:, hs], kg_ref[...], cos, sin, lane)
    v_ref[...] = seg(2)
    qh_ref[...] = jax.nn.silu(seg(3))
    lb = _lower_bound(lbl_ref)
    f = lb + (1.0 - lb) * jax.nn.sigmoid(seg(4))
    for h in range(N_HEADS):
        hs = slice(h * HEAD_W, (h + 1) * HEAD_W)
        ft_ref[hs, :] = f[:, hs].T
    vh_ref[...] = seg(5)
    og_ref[...] = jax.nn.silu(seg(6))
    ga_ref[...] = jax.nn.sigmoid(seg(7))
    gb_ref[...] = jax.nn.sigmoid(seg(8))


def _inproj_sample(x, sh, sc, gmix, w_in, qg, kg, cos, sin, lbl):
    m = x.shape[0]
    tok = jax.ShapeDtypeStruct((m, D_MODEL), F32)
    chan = jax.ShapeDtypeStruct((D_MODEL, m), F32)
    ts = _const_spec((m, D_MODEL))
    cs = _const_spec((D_MODEL, m))
    return pl.pallas_call(
        _inproj_sample_kernel,
        out_shape=(tok, tok, tok, tok, chan, tok, tok, tok, tok),
        grid=(1,),
        in_specs=[ts, ts, ts, _const_spec((1, D_MODEL)), _const_spec(w_in.shape),
                  _const_spec((1, HEAD_W)), _const_spec((1, HEAD_W)), _const_spec((1, HEAD_W)),
                  _const_spec((1, HEAD_W)), _const_spec(lbl.shape)],
        out_specs=(ts, ts, ts, ts, cs, ts, ts, ts, ts),
        compiler_params=_params(("arbitrary",)),
        name="inproj_sample",
    )(x, sh, sc, gmix, w_in, qg, kg, cos, sin, lbl)


def _hgrn_boundary_rows(cum, h):
    c, d = cum.shape
    if 2 * h >= 8:
        grp = 2 * h
        return jnp.concatenate(
            [jnp.broadcast_to(cum[i * grp + h - 1:i * grp + h, :], (grp, d)) for i in range(c // grp)], axis=0)
    cum3 = cum.reshape(c // 8, 8, d)
    sub = lax.broadcasted_iota(jnp.int32, (c // 8, 8, d), 1)
    out = None
    for first in range(0, 8, 2 * h):
        row = jnp.broadcast_to(cum3[:, first + h - 1:first + h, :], cum3.shape)
        out = row if out is None else jnp.where(sub >= first, row, out)
    return out.reshape(c, d)


def _hgrn_prompt_kernel(ltri_ref, q_ref, k_ref, g_ref, v_ref, og_ref, gn_ref, ob_ref, sout_ref, st_ref, *, ct):
    c = HGRN_CHUNK
    tstep = pl.program_id(1)

    @pl.when(tstep == 0)
    def _():
        st_ref[...] = jnp.zeros_like(st_ref)

    tt = lax.broadcasted_iota(jnp.int32, (c, c), 0)
    ss = lax.broadcasted_iota(jnp.int32, (c, c), 1)
    meet = [((tt // (2 * h)) == (ss // (2 * h))) & (((tt // h) % 2) == 1) & (((ss // h) % 2) == 0)
            for h in HGRN_LEVELS]
    eye = tt == ss
    ti = lax.broadcasted_iota(jnp.int32, (c, 1), 0)
    side = [jnp.where(((ti // h) % 2) == 1, 1.0, -1.0) for h in HGRN_LEVELS]
    ltri = ltri_ref[...]
    gn = gn_ref[...]

    def chunk(ci, carry):
        rows = pl.ds(pl.multiple_of(ci * c, c), c)
        g = g_ref[0, rows, :]
        g_hi = g.astype(BF16)
        r1 = g - g_hi.astype(F32)
        g_mid = r1.astype(BF16)
        g_lo = (r1 - g_mid.astype(F32)).astype(BF16)
        cum = (jnp.dot(ltri, g_hi, preferred_element_type=F32)
               + jnp.dot(ltri, g_mid, preferred_element_type=F32)
               + jnp.dot(ltri, g_lo, preferred_element_type=F32))
        q = q_ref[0, rows, :]
        k = k_ref[0, rows, :]
        v = v_ref[0, rows, :]
        og = og_ref[0, rows, :]
        q_in = q * jnp.exp(cum)
        k_out = k * jnp.exp(cum[c - 1:c, :] - cum)
        d_all = jnp.exp(cum[c - 1:c, :])
        ex = [jnp.exp((cum - _hgrn_boundary_rows(cum, h)) * side[li]).astype(BF16)
              for li, h in enumerate(HGRN_LEVELS)]
        qb = q.astype(BF16)
        kb = k.astype(BF16)
        for h in range(N_HEADS):
            hs = slice(h * HEAD_W, (h + 1) * HEAD_W)
            a = jnp.where(eye, jnp.sum(q[:, hs] * k[:, hs], axis=-1, keepdims=True), 0.0)
            for li in range(len(HGRN_LEVELS)):
                e = ex[li][:, hs]
                al = lax.dot_general(qb[:, hs] * e, kb[:, hs] * e, NT_DIMS, preferred_element_type=F32)
                a = jnp.where(meet[li], al, a)
            vb = v[:, hs].astype(BF16)
            st = st_ref[h]
            o = (jnp.dot(a.astype(BF16), vb, preferred_element_type=F32)
                 + lax.dot_general(q_in[:, hs].astype(BF16), st.astype(BF16), NT_DIMS, preferred_element_type=F32))
            st_ref[h] = d_all[:, hs] * st + lax.dot_general(vb, k_out[:, hs].astype(BF16), TN_DIMS,
                                                            preferred_element_type=F32)
            ob_ref[0, rows, hs] = (_rms(o, gn) * og[:, hs]).astype(BF16)
        return carry

    lax.fori_loop(0, ct // c, chunk, 0, unroll=True)

    @pl.when(tstep == pl.num_programs(1) - 1)
    def _():
        for h in range(N_HEADS):
            sout_ref[0, h] = st_ref[h].T


def _hgrn_prompt(qh, kh, gh, vh, og, gn, ct=256):
    b, t, _ = qh.shape
    dall = jnp.asarray(np.tril(np.ones((HGRN_CHUNK, HGRN_CHUNK), np.float32)), dtype=BF16)
    tok_spec = pl.BlockSpec((1, ct, D_MODEL), lambda i, j: (i, j, 0))
    return pl.pallas_call(
        functools.partial(_hgrn_prompt_kernel, ct=ct),
        out_shape=(jax.ShapeDtypeStruct((b, t, D_MODEL), BF16),
                   jax.ShapeDtypeStruct((b, N_HEADS, HEAD_W, HEAD_W), F32)),
        grid=(b, t // ct),
        in_specs=[_const_spec(dall.shape), tok_spec, tok_spec, tok_spec, tok_spec, tok_spec, _const_spec((1, HEAD_W))],
        out_specs=(tok_spec, pl.BlockSpec((1, N_HEADS, HEAD_W, HEAD_W), lambda i, j: (i, 0, 0, 0))),
        scratch_shapes=[pltpu.VMEM((N_HEADS, HEAD_W, HEAD_W), F32)],
        compiler_params=_params(("parallel", "arbitrary")),
        name="hgrn_prompt",
    )(dall, qh, kh, gh, vh, og, gn)


def _lambda(lam_ref, lam_init):
    lv = lam_ref[...]
    s1 = jnp.sum(lv[0:1] * lv[1:2], axis=-1, keepdims=True)
    s2 = jnp.sum(lv[2:3] * lv[3:4], axis=-1, keepdims=True)
    return jnp.exp(s1) - jnp.exp(s2) + lam_init


def _attn_prompt_step(lam_ref, g_ref, q_ref, k_ref, v_ref, o_ref, m_ref, acc_ref, *, tq, tk, lam_init,
                      before_diagonal=None):
    qi = pl.program_id(2)
    q = q_ref[0, 0]
    lane = lax.broadcasted_iota(jnp.int32, (1, HEAD_W), 1)
    zero = jnp.zeros_like(q)
    qs = jnp.concatenate([jnp.where(lane < A_DH, q, zero), jnp.where(lane < A_DH, zero, q)], axis=0)
    rg = ATTN_ROW_GROUP
    ng = 2 * tq // rg

    def block(k0, masked):
        for g in range(ng):
            r0 = (g * rg) % tq
            gr = slice(g * rg, (g + 1) * rg)
            width = min(tk, r0 + rg) if masked else tk
            rows = pl.ds(pl.multiple_of(k0, tk), width)
            s = lax.dot_general(qs[gr], k_ref[0, 0, rows, :], NT_DIMS, preferred_element_type=F32)
            if masked:
                qpos = r0 + lax.broadcasted_iota(jnp.int32, (rg, width), 0)
                kpos = lax.broadcasted_iota(jnp.int32, (rg, width), 1)
                s = jnp.where(kpos <= qpos, s, NEG_BIG)
            m_old = m_ref[gr, :]
            m_new = jnp.maximum(m_old, jnp.max(s, axis=-1, keepdims=True))
            alpha = jnp.exp2(m_old - m_new)
            p = jnp.exp2(s - jnp.tile(m_new, (1, width // HEAD_W)))
            pv = jnp.dot(p.astype(BF16), v_ref[0, 0, rows, :], preferred_element_type=F32)
            m_ref[gr, :] = m_new
            acc_ref[gr, :] = jnp.tile(alpha, (1, 2)) * acc_ref[gr, :] + pv

    m_ref[...] = jnp.full(m_ref.shape, NEG_BIG, F32)
    acc_ref[...] = jnp.zeros(acc_ref.shape, F32)

    def two_blocks(i, carry):
        block(2 * i * tk, False)
        block((2 * i + 1) * tk, False)
        return carry

    lax.fori_loop(0, qi // 2, two_blocks, 0)

    @pl.when(qi % 2 == 1)
    def _():
        block((qi - 1) * tk, False)

    if before_diagonal is not None:
        before_diagonal()
    block(qi * tq, True)
    acc = acc_ref[...]
    o = acc[:, 0:HEAD_W] / acc[:, HEAD_W:2 * HEAD_W]
    lam = _lambda(lam_ref, lam_init)
    od = o[0:tq] - lam * o[tq:2 * tq]
    o_ref[0, 0] = _rms(od, g_ref[...]).astype(BF16)


def _mix_kernel(x_ref, gm_ref, oa_ref, ob_ref, ga_ref, gb_ref, wa_ref, wb_ref, wo_ref, o_ref):
    oa = jnp.concatenate([oa_ref[0, h] for h in range(N_HEADS)], axis=-1)
    ya = jnp.dot(oa, wa_ref[...], preferred_element_type=F32)
    yb = jnp.dot(ob_ref[0].astype(BF16), wb_ref[...], preferred_element_type=F32)
    merged = ga_ref[0] * ya + gb_ref[0] * yb
    mix = jnp.dot(merged.astype(BF16), wo_ref[...], preferred_element_type=F32)
    o_ref[0] = x_ref[0] + gm_ref[0] * mix


def _mix(x, gm, oa_hm, ob, ga, gb, wa, wb, wo, tm):
    b, t, _ = x.shape
    rows_mod = gm.shape[1]
    tok_spec = pl.BlockSpec((1, tm, D_MODEL), lambda i, j: (i, j, 0))
    mod_spec = (pl.BlockSpec((1, 1, D_MODEL), lambda i, j: (i, 0, 0)) if rows_mod == 1 else tok_spec)
    return pl.pallas_call(
        _mix_kernel,
        out_shape=jax.ShapeDtypeStruct((b, t, D_MODEL), F32),
        grid=(b, t // tm),
        in_specs=[tok_spec, mod_spec, pl.BlockSpec((1, N_HEADS, tm, HEAD_W), lambda i, j: (i, 0, j, 0)),
                  tok_spec, tok_spec, tok_spec,
                  _const_spec(wa.shape), _const_spec(wb.shape), _const_spec(wo.shape)],
        out_specs=tok_spec,
        compiler_params=_params(("parallel", "parallel")),
        name="mix_out",
    )(x, gm, oa_hm, ob, ga, gb, wa, wb, wo)


def _ffn_kernel(*refs, tm, carried):
    if carried:
        (x_ref, sh_ref, sc_ref, gt_ref, g_ref, wup_ref, cw_ref, cb_ref, wdn_ref, y_ref, tail_ref, prev_ref) = refs
    else:
        (x_ref, sh_ref, sc_ref, gt_ref, g_ref, wup_ref, cw_ref, cb_ref, wdn_ref, b0_ref, b1_ref, y_ref, u_ref) = refs
    x = x_ref[0]
    hn = (_rms(x, g_ref[...]) * (1.0 + sc_ref[0]) + sh_ref[0]).astype(BF16)
    if carried:
        @pl.when(pl.program_id(1) == 0)
        def _():
            prev_ref[...] = jnp.zeros_like(prev_ref)
        row = lax.broadcasted_iota(jnp.int32, (tm, 1), 0)

    def conv(u, cols):
        if carried:
            p6 = prev_ref[6:7, cols]
            p7 = prev_ref[7:8, cols]
            u1 = jnp.where(row == 0, p7, pltpu.roll(u, 1, 0))
            u2 = jnp.where(row == 0, p6, jnp.where(row == 1, p7, pltpu.roll(u, 2, 0)))
            prev_ref[:, cols] = u[tm - 8:tm, :]
            tail_ref[0, :, cols] = u[tm - 8:tm, :]
        else:
            u2 = b0_ref[0, :, cols]
            u1 = b1_ref[0, :, cols]
            u_ref[0, :, cols] = u
        return cb_ref[:, cols] + cw_ref[0:1, cols] * u2 + cw_ref[1:2, cols] * u1 + cw_ref[2:3, cols] * u

    nchunk = D_FF // FFN_COLS

    def cols(c):
        return slice(c * FFN_COLS, (c + 1) * FFN_COLS), slice(D_FF + c * FFN_COLS, D_FF + (c + 1) * FFN_COLS)

    def up(c):
        ca, cg = cols(c)
        return (jnp.dot(hn, wup_ref[:, ca], preferred_element_type=F32),
                jnp.dot(hn, wup_ref[:, cg], preferred_element_type=F32))

    acc = jnp.zeros((tm, D_MODEL), F32)
    ups = [up(0), up(1)]
    acts = []
    for c in range(nchunk):
        ua, ug = ups[c]
        if c + 2 < nchunk:
            ups.append(up(c + 2))
        ca, cg = cols(c)
        acts.append((jax.nn.silu(conv(ua, ca)) * conv(ug, cg)).astype(BF16))
        if c >= 1:
            acc = acc + jnp.dot(acts[c - 1], wdn_ref[cols(c - 1)[0], :], preferred_element_type=F32)
    acc = acc + jnp.dot(acts[nchunk - 1], wdn_ref[cols(nchunk - 1)[0], :], preferred_element_type=F32)
    y_ref[0] = x + gt_ref[0] * acc


def _ffn(x1, sh, sc, gt, g, wup, cw, cb, wdn, tm, bufs=None):
    b, t, _ = x1.shape
    carried = bufs is None
    tok_spec = pl.BlockSpec((1, tm, D_MODEL), lambda i, j: (i, j, 0))
    mod_spec = (pl.BlockSpec((1, 1, D_MODEL), lambda i, j: (i, 0, 0)) if sh.shape[1] == 1 else tok_spec)
    wide_spec = pl.BlockSpec((1, tm, 2 * D_FF), lambda i, j: (i, j, 0))
    in_specs = [tok_spec, mod_spec, mod_spec, mod_spec, _const_spec((1, D_MODEL)), _const_spec(wup.shape),
                _const_spec(cw.shape), _const_spec(cb.shape), _const_spec(wdn.shape)]
    args = [x1, sh, sc, gt, g, wup, cw, cb, wdn]
    if carried:
        out_shape = (jax.ShapeDtypeStruct((b, t, D_MODEL), F32), jax.ShapeDtypeStruct((b, 8, 2 * D_FF), F32))
        out_specs = (tok_spec, pl.BlockSpec((1, 8, 2 * D_FF), lambda i, j: (i, 0, 0)))
        scratch = [pltpu.VMEM((8, 2 * D_FF), F32)]
        sem = ("parallel", "arbitrary")
    else:
        in_specs += [wide_spec, wide_spec]
        args += list(bufs)
        out_shape = (jax.ShapeDtypeStruct((b, t, D_MODEL), F32), jax.ShapeDtypeStruct((b, t, 2 * D_FF), F32))
        out_specs = (tok_spec, wide_spec)
        scratch = []
        sem = ("parallel", "parallel")
    return pl.pallas_call(
        functools.partial(_ffn_kernel, tm=tm, carried=carried),
        out_shape=out_shape,
        grid=(b, t // tm),
        in_specs=in_specs,
        out_specs=out_specs,
        scratch_shapes=scratch,
        compiler_params=_params(sem),
        name="conv_ffn",
    )(*args)


def _hgrn_sample_kernel(ft_ref, q_ref, v_ref, og_ref, gn_ref, s_ref, so_ref, o_ref, *, nb):
    ft = ft_ref[...]
    first_row = lax.broadcasted_iota(jnp.int32, (2 * N_HEADS, HEAD_W), 0) == 0
    for b in range(nb):
        f = ft[:, b:b + 1]
        s_new = f * s_ref[b, 0] + (1.0 - f) * v_ref[b:b + 1, :]
        so_ref[b, 0] = s_new
        lhs = jnp.where(first_row, q_ref[b:b + 1, :], 0.0).astype(BF16)
        o_ref[b:b + 1, :] = jnp.dot(lhs, s_new.astype(BF16), preferred_element_type=F32)[0:1, :]
    o_ref[...] = _rms(o_ref[...], gn_ref[...]) * og_ref[...]


def _hgrn_sample(ft, qh, vh, og, gn, state):
    nb = vh.shape[0]
    st_spec = pl.BlockSpec((nb, 1, HEAD_W, HEAD_W), lambda h: (0, h, 0, 0))
    hd_spec = pl.BlockSpec((nb, HEAD_W), lambda h: (0, h))
    return pl.pallas_call(
        functools.partial(_hgrn_sample_kernel, nb=nb),
        out_shape=(jax.ShapeDtypeStruct(state.shape, F32), jax.ShapeDtypeStruct((nb, D_MODEL), F32)),
        grid=(N_HEADS,),
        in_specs=[pl.BlockSpec((HEAD_W, nb), lambda h: (h, 0)), hd_spec, hd_spec, hd_spec, _const_spec((1, HEAD_W)),
                  st_spec],
        out_specs=(st_spec, hd_spec),
        compiler_params=_params(("parallel",)),
        name="hgrn_sample",
    )(ft, qh, vh, og, gn, state)


def _page_copies(pt_ref, ck_hbm, cv_hbm, kbuf, vbuf, sem, b, n_pages):
    sl = b % 2
    out = []
    for p in range(n_pages):
        pg = pt_ref[b, p]
        out.append(pltpu.make_async_copy(ck_hbm.at[pg], kbuf.at[sl, p], sem.at[0, sl]))
        out.append(pltpu.make_async_copy(cv_hbm.at[pg], vbuf.at[sl, p], sem.at[1, sl]))
    return out


def _attn_sample_one(b, lam_ref, g_ref, q_ref, kn_ref, vn_ref, o_ref, kbuf, vbuf, *, n_pages, page, lam_init):
    slot = b % 2
    q = q_ref[pl.ds(b, 1), :]
    r16 = lax.broadcasted_iota(jnp.int32, (2 * N_HEADS, D_MODEL), 0)
    ln = lax.broadcasted_iota(jnp.int32, (2 * N_HEADS, D_MODEL), 1)
    sel = ((ln // HEAD_W) == (r16 % N_HEADS)) & (((ln // A_DH) % 2) == (r16 // N_HEADS))
    qsel = jnp.where(sel, q, 0.0)
    qblk = qsel.astype(BF16)
    s = jnp.concatenate([jnp.dot(qblk, kbuf[slot, p].astype(BF16), preferred_element_type=F32)
                         for p in range(n_pages)], axis=1)
    s_new = jnp.sum(qsel * kn_ref[pl.ds(b, 1), :], axis=-1, keepdims=True)
    m = jnp.maximum(jnp.max(s, axis=-1, keepdims=True), s_new)
    e = jnp.exp2(s - m)
    e_new = jnp.exp2(s_new - m)
    inv_l = 1.0 / (jnp.sum(e, axis=-1, keepdims=True) + e_new)
    lam = _lambda(lam_ref, lam_init)
    pr = e * inv_l
    pr_new = e_new * inv_l
    pd = pr[0:N_HEADS] - lam * pr[N_HEADS:2 * N_HEADS]
    pd_new = pr_new[0:N_HEADS] - lam * pr_new[N_HEADS:2 * N_HEADS]

    per = HEAD_W // N_HEADS
    lane = lax.broadcasted_iota(jnp.int32, (N_HEADS, HEAD_W), 1)
    own = (lane % N_HEADS) == lax.broadcasted_iota(jnp.int32, (N_HEADS, HEAD_W), 0)
    pad = jnp.zeros((N_HEADS, HEAD_W), F32)

    acc = jnp.zeros((2 * N_HEADS, HEAD_W), F32)
    for p in range(n_pages):
        w = pd[:, p * page:(p + 1) * page]
        for g in range(page // per):
            wx = jnp.take_along_axis(w, g * per + lane // N_HEADS, axis=1)
            lhs = jnp.concatenate([jnp.where(own, wx, 0.0), pad], axis=0).astype(BF16)
            rhs = vbuf[slot, p, pl.ds(g * per, per)].reshape(HEAD_W, HEAD_W).astype(BF16)
            acc = acc + jnp.dot(lhs, rhs, preferred_element_type=F32)
    o_ref[b] = _rms(acc[0:N_HEADS] + pd_new * vn_ref[b], g_ref[...])


def _attn_kernel(pt_ref, lam_ref, g_ref, q_ref, k_ref, v_ref, qs_ref, kn_ref, vn_ref, ck_hbm, cv_hbm,
                 o_ref, os_ref, kbuf, vbuf, m_ref, acc_ref, sem, *, tq, lam_init, n_pages, page, per_step):
    step = (pl.program_id(0) * pl.num_programs(1) + pl.program_id(1)) * pl.num_programs(2) + pl.program_id(2)
    n_steps = pl.num_programs(0) * pl.num_programs(1) * pl.num_programs(2)
    first = step * per_step
    copies = functools.partial(_page_copies, pt_ref, ck_hbm, cv_hbm, kbuf, vbuf, sem, n_pages=n_pages)

    @pl.when(step == 0)
    def _():
        for cp in copies(0):
            cp.start()

    if per_step == 1:
        @pl.when(step + 1 < n_steps)
        def _():
            for cp in copies(step + 1):
                cp.start()

    def wait_first():
        for cp in copies(first):
            cp.wait()

    _attn_prompt_step(lam_ref, g_ref, q_ref, k_ref, v_ref, o_ref, m_ref, acc_ref, tq=tq, tk=tq, lam_init=lam_init,
                      before_diagonal=wait_first)

    for j in range(per_step):
        b = first + j
        if per_step > 1:
            @pl.when(b + 1 < n_steps * per_step)
            def _():
                for cp in copies(b + 1):
                    cp.start()
        if j > 0:
            for cp in copies(b):
                cp.wait()
        _attn_sample_one(b, lam_ref, g_ref, qs_ref, kn_ref, vn_ref, os_ref, kbuf, vbuf,
                         n_pages=n_pages, page=page, lam_init=lam_init)


def _attention(page_table, lam4, g_sub, q_hm, k_hm, v_hm, q_s, k_new, v_new, cache_kt, cache_v, lam_init, tq=1024):
    b, nh, t, _ = q_hm.shape
    nb, n_pages = page_table.shape
    page = cache_v.shape[1]
    n_steps = b * nh * (t // tq)
    assert nb % n_steps == 0, (nb, n_steps)
    full = lambda shape: pl.BlockSpec(shape, lambda i, h, j, pt: (0,) * len(shape))
    any_spec = pl.BlockSpec(memory_space=pl.ANY)
    k_spec = pl.BlockSpec((1, 1, t, HEAD_W), lambda i, h, j, pt: (i, h, 0, 0))
    v_spec = pl.BlockSpec((1, 1, t, 2 * HEAD_W), lambda i, h, j, pt: (i, h, 0, 0))
    q_spec = pl.BlockSpec((1, 1, tq, HEAD_W), lambda i, h, j, pt: (i, h, j, 0))
    grid_spec = pltpu.PrefetchScalarGridSpec(
        num_scalar_prefetch=1,
        grid=(b, nh, t // tq),
        in_specs=[full(lam4.shape), full((1, HEAD_W)), q_spec, k_spec, v_spec,
                  full(q_s.shape), full(k_new.shape), full(v_new.shape), any_spec, any_spec],
        out_specs=(q_spec, full((nb, N_HEADS, HEAD_W))),
        scratch_shapes=[pltpu.VMEM((2, n_pages, D_MODEL, page), F32),
                        pltpu.VMEM((2, n_pages, page, N_HEADS, HEAD_W), F32),
                        pltpu.VMEM((2 * tq, HEAD_W), F32), pltpu.VMEM((2 * tq, 2 * HEAD_W), F32),
                        pltpu.SemaphoreType.DMA((2, 2))],
    )
    return pl.pallas_call(
        functools.partial(_attn_kernel, tq=tq, lam_init=lam_init, n_pages=n_pages, page=page,
                          per_step=nb // n_steps),
        out_shape=(jax.ShapeDtypeStruct((b, nh, t, HEAD_W), BF16), jax.ShapeDtypeStruct((nb, N_HEADS, HEAD_W), F32)),
        grid_spec=grid_spec,
        compiler_params=_params(("arbitrary", "arbitrary", "arbitrary")),
        name="attention",
    )(page_table, lam4, g_sub, q_hm, k_hm, v_hm, q_s, k_new, v_new, cache_kt, cache_v)


def _rope_tables(pos):
    half = A_DH // 2
    inv = ROPE_THETA ** (-np.arange(half, dtype=np.float64) / half)
    ang = np.asarray(pos, np.float64)[:, None] * inv[None, :]
    cos = np.cos(ang)
    sin = np.sin(ang)
    return (jnp.asarray(np.tile(cos, (1, 4)), F32),
            jnp.asarray(np.concatenate([-sin, sin, -sin, sin], axis=-1), F32))


def kernel(x_prompt, x_sample, cache_k, cache_v, state_hgrn, state_conv, page_table, c_prompt, c_sample, w_ada, b_ada,
           rms_mix_g, w_in, q_norm_g, k_norm_g, lambda_q1, lambda_k1, lambda_q2, lambda_k2, subln_g, lb_logits,
           hgrn_norm_g, w_branch_a, w_branch_b, w_out, rms_ffn_g, w_up, conv_w, conv_b, w_down):
    bp, tp, d = x_prompt.shape
    bs, ts, _ = x_sample.shape
    assert d == D_MODEL and ts == 1 and w_in.shape[0] == 1
    n_pool, page = cache_k.shape[1], cache_k.shape[2]
    past_len = page_table.shape[1] * page
    lam_init = 0.8 - 0.6 * math.exp(-0.3 * 0)

    pad = (-(bp + bs)) % 8
    c_all = jnp.concatenate([c_prompt, c_sample, jnp.zeros((pad, d), F32)], axis=0)
    mod = _ada(c_all, w_ada[0], b_ada[0].reshape(1, -1))
    mp = [mod[:bp, i * d:(i + 1) * d].reshape(bp, 1, d) for i in range(6)]
    msamp = [mod[bp:bp + bs, i * d:(i + 1) * d] for i in range(6)]

    w_in_b = w_in[0].astype(BF16)
    wa_b = w_branch_a[0].astype(BF16)
    wb_b = w_branch_b[0].astype(BF16)
    wo_b = w_out[0].astype(BF16)
    wup_b = w_up[0].astype(BF16)
    wdn_b = w_down[0].astype(BF16)
    gmix = rms_mix_g[0].reshape(1, d)
    gffn = rms_ffn_g[0].reshape(1, d)
    qg = jnp.tile(q_norm_g[0], 2).reshape(1, HEAD_W)
    kg = jnp.tile(k_norm_g[0], 2).reshape(1, HEAD_W)
    lam4 = jnp.stack([lambda_q1[0], lambda_k1[0], lambda_q2[0], lambda_k2[0]], axis=0)
    g_sub = (subln_g[0] * (1.0 - lam_init)).reshape(1, HEAD_W)
    gn = hgrn_norm_g[0].reshape(1, HEAD_W)
    cw = conv_w[0]
    cb = conv_b[0].reshape(1, -1)

    cos_p, sin_p = _rope_tables(np.arange(tp))
    (q_hm, k_tok, k_hm, v_tok, v_hm, qh, kh, gh, vh, og, ga, gb) = _inproj_prompt(
        x_prompt, mp[0], mp[1], gmix, w_in_b, qg, kg, cos_p, sin_p, lb_logits)
    ob, s_prompt = _hgrn_prompt(qh, kh, gh, vh, og, gn)
    cos_s, sin_s = _rope_tables(np.full((1,), past_len))
    xs = x_sample.reshape(bs, d)
    (q_s, k_s, v_s, qh_s, ft_s, vh_s, og_s, ga_s, gb_s) = _inproj_sample(
        xs, msamp[0], msamp[1], gmix, w_in_b, qg, kg, cos_s, sin_s, lb_logits)
    s_sample, ob_s = _hgrn_sample(ft_s, qh_s, vh_s, og_s, gn, state_hgrn[0])

    cache_kt = jnp.transpose(cache_k[0], (0, 2, 3, 4, 1)).reshape(n_pool, d, page)
    oa_hm, oa_s = _attention(page_table, lam4, g_sub, q_hm, k_hm, v_hm, q_s, k_s, v_s.reshape(bs, N_HEADS, HEAD_W),
                             cache_kt, cache_v[0], lam_init)

    x1 = _mix(x_prompt, mp[2], oa_hm, ob, ga, gb, wa_b, wb_b, wo_b, tm=512)
    y_prompt, tail = _ffn(x1, mp[3], mp[4], mp[5], gffn, wup_b, cw, cb, wdn_b, tm=512)
    oa_s = jnp.transpose(oa_s, (1, 0, 2))
    x1_s = _mix(xs[None], msamp[2][None], oa_s[None].astype(BF16), ob_s[None], ga_s[None], gb_s[None],
                wa_b, wb_b, wo_b, tm=bs)
    y_s, u_s = _ffn(x1_s, msamp[3][None], msamp[4][None], msamp[5][None], gffn, wup_b, cw, cb, wdn_b, tm=bs,
                    bufs=(state_conv[0][:, 0][None], state_conv[0][:, 1][None]))

    new_conv_s = jnp.stack([state_conv[0][:, 1], u_s[0]], axis=1)
    return (y_prompt, y_s.reshape(bs, 1, d),
            k_tok.reshape(1, bp, tp, N_HEADS, 2, A_DH), v_tok.reshape(1, bp, tp, N_HEADS, HEAD_W),
            s_prompt[None], tail[:, 6:8][None],
            k_s.reshape(1, bs, 1, N_HEADS, 2, A_DH), v_s.reshape(1, bs, 1, N_HEADS, HEAD_W),
            s_sample[None], new_conv_s[None])
```

```python
import functools
import math

import numpy as np
import jax
import jax.numpy as jnp
from jax import lax
from jax.experimental import pallas as pl
from jax.experimental.pallas import tpu as pltpu

F32 = jnp.float32
BF16 = jnp.bfloat16

D_MODEL = 1024
N_HEADS = 8
HEAD_W = 128
A_DH = 64
D_FF = 2816
ROPE_THETA = 10000.0
EPS = 1e-6
NEG_BIG = -1e30
HGRN_CHUNK = 64
HGRN_LEVELS = (32, 16, 8, 4, 2, 1)
FFN_COLS = 256
FFN_AHEAD = 6
FFN_BEHIND = 3
VMEM_LIMIT = 56 * 1024 * 1024
Q_SCALE = (A_DH ** -0.5) * math.log2(math.e)

ATTN_ROW_GROUP = 256
ATTN_AHEAD = 2
NT_DIMS = (((1,), (1,)), ((), ()))
TN_DIMS = (((0,), (0,)), ((), ()))


def _const_spec(shape):
    nd = len(shape)
    return pl.BlockSpec(shape, lambda *_: (0,) * nd, pipeline_mode=pl.Buffered(1))


def _params(sem, vmem=VMEM_LIMIT):
    return pltpu.CompilerParams(dimension_semantics=sem, vmem_limit_bytes=vmem)


def _rms(x, g):
    ms = jnp.mean(x * x, axis=-1, keepdims=True)
    return (x * lax.rsqrt(ms + EPS)) * g


def _ada_kernel(c_ref, w_ref, b_ref, o_ref):
    a = jax.nn.silu(c_ref[...]).astype(BF16)
    o_ref[...] = jnp.dot(a, w_ref[...].astype(BF16), preferred_element_type=F32) + b_ref[...]


def _ada(c_all, w_ada, b_ada):
    m = c_all.shape[0]
    n = w_ada.shape[1]
    tn = 1536
    return pl.pallas_call(
        _ada_kernel,
        out_shape=jax.ShapeDtypeStruct((m, n), F32),
        grid=(n // tn,),
        in_specs=[
            _const_spec((m, D_MODEL)),
            pl.BlockSpec((D_MODEL, tn), lambda j: (0, j)),
            pl.BlockSpec((1, tn), lambda j: (0, j)),
        ],
        out_specs=pl.BlockSpec((m, tn), lambda j: (0, j)),
        compiler_params=_params(("arbitrary",)),
        name="ada_mod",
    )(c_all, w_ada, b_ada)


def _qk_norm_rope(z, g128, cos, sin, lane):
    sq = z * z
    first = lane < A_DH
    lo = jnp.sum(jnp.where(first, sq, 0.0), axis=-1, keepdims=True)
    al = jnp.sum(sq, axis=-1, keepdims=True)
    ms = jnp.where(first, lo, al - lo) * (1.0 / A_DH)
    zn = (z * lax.rsqrt(ms + EPS)) * g128
    rot = jnp.where((lane % A_DH) < A_DH // 2, pltpu.roll(zn, HEAD_W - A_DH // 2, 1), pltpu.roll(zn, A_DH // 2, 1))
    return zn * cos + rot * sin


def _lower_bound(lbl_ref):
    lbl = lbl_ref[...]
    mx = jnp.max(lbl, axis=0, keepdims=True)
    e = jnp.exp(lbl - mx)
    return e[0:1, :] / jnp.sum(e, axis=0, keepdims=True)


def _inproj_common(x, sh, sc, gmix_ref, w_ref):
    xn = _rms(x, gmix_ref[...]) * (1.0 + sc) + sh
    xb = xn.astype(BF16)

    def seg(i):
        return jnp.dot(xb, w_ref[:, i * D_MODEL:(i + 1) * D_MODEL], preferred_element_type=F32)

    return seg


def _inproj_prompt_kernel(x_ref, sh_ref, sc_ref, gmix_ref, w_ref, qg_ref, kg_ref, cos_ref, sin_ref, lbl_ref,
                          qhm_ref, ktok_ref, khm_ref, vtok_ref, vhm_ref,
                          qh_ref, kh_ref, gh_ref, vh_ref, og_ref, ga_ref, gb_ref):
    seg = _inproj_common(x_ref[0], sh_ref[0], sc_ref[0], gmix_ref, w_ref)
    cos = cos_ref[...]
    sin = sin_ref[...]
    lane = lax.broadcasted_iota(jnp.int32, (1, HEAD_W), 1)
    zq = seg(0)
    zk = seg(1)
    for h in range(N_HEADS):
        hs = slice(h * HEAD_W, (h + 1) * HEAD_W)
        q = _qk_norm_rope(zq[:, hs], qg_ref[...], cos, sin, lane) * Q_SCALE
        qhm_ref[0, h] = q.astype(BF16)
        k = _qk_norm_rope(zk[:, hs], kg_ref[...], cos, sin, lane)
        ktok_ref[0, :, hs] = k
        khm_ref[0, h] = k.astype(BF16)
    zv = seg(2)
    vtok_ref[0] = zv
    ones = jnp.ones((zv.shape[0], HEAD_W), BF16)
    for h in range(N_HEADS):
        vhm_ref[0, h, :, 0:HEAD_W] = zv[:, h * HEAD_W:(h + 1) * HEAD_W].astype(BF16)
        vhm_ref[0, h, :, HEAD_W:2 * HEAD_W] = ones
    qh_ref[0] = jax.nn.silu(seg(3))
    lb = _lower_bound(lbl_ref)
    f = lb + (1.0 - lb) * jax.nn.sigmoid(seg(4))
    kh_ref[0] = 1.0 - f
    gh_ref[0] = jnp.log(f)
    vh_ref[0] = seg(5)
    og_ref[0] = jax.nn.silu(seg(6))
    ga_ref[0] = jax.nn.sigmoid(seg(7))
    gb_ref[0] = jax.nn.sigmoid(seg(8))


def _inproj_prompt(x, sh, sc, gmix, w_in, qg, kg, cos, sin, lbl, tm=256):
    b, t, _ = x.shape
    tok = jax.ShapeDtypeStruct((b, t, D_MODEL), F32)
    hm = jax.ShapeDtypeStruct((b, N_HEADS, t, HEAD_W), BF16)
    hm2 = jax.ShapeDtypeStruct((b, N_HEADS, t, 2 * HEAD_W), BF16)
    tok_spec = pl.BlockSpec((1, tm, D_MODEL), lambda i, j: (i, j, 0))
    hm_spec = pl.BlockSpec((1, N_HEADS, tm, HEAD_W), lambda i, j: (i, 0, j, 0))
    hm2_spec = pl.BlockSpec((1, N_HEADS, tm, 2 * HEAD_W), lambda i, j: (i, 0, j, 0))
    mod_spec = pl.BlockSpec((1, 1, D_MODEL), lambda i, j: (i, 0, 0))
    tab_spec = pl.BlockSpec((tm, HEAD_W), lambda i, j: (j, 0))
    return pl.pallas_call(
        _inproj_prompt_kernel,
        out_shape=(hm, tok, hm, tok, hm2, tok, tok, tok, tok, tok, tok, tok),
        grid=(b, t // tm),
        in_specs=[tok_spec, mod_spec, mod_spec, _const_spec((1, D_MODEL)), _const_spec(w_in.shape),
                  _const_spec((1, HEAD_W)), _const_spec((1, HEAD_W)), tab_spec, tab_spec, _const_spec(lbl.shape)],
        out_specs=(hm_spec, tok_spec, hm_spec, tok_spec, hm2_spec,
                   tok_spec, tok_spec, tok_spec, tok_spec, tok_spec, tok_spec, tok_spec),
        compiler_params=_params(("parallel", "parallel")),
        name="inproj_prompt",
    )(x, sh, sc, gmix, w_in, qg, kg, cos, sin, lbl)


def _inproj_sample_kernel(x_ref, sh_ref, sc_ref, gmix_ref, w_ref, qg_ref, kg_ref, cos_ref, sin_ref, lbl_ref,
                          q_ref, k_ref, v_ref, qh_ref, ft_ref, vh_ref, og_ref, ga_ref, gb_ref):
    seg = _inproj_common(x_ref[...], sh_ref[...], sc_ref[...], gmix_ref, w_ref)
    cos = cos_ref[...]
    sin = sin_ref[...]
    lane = lax.broadcasted_iota(jnp.int32, (1, HEAD_W), 1)
    zq = seg(0)
    zk = seg(1)
    for h in range(N_HEADS):
        hs = slice(h * HEAD_W, (h + 1) * HEAD_W)
        q_ref[:, hs] = _qk_norm_rope(zq[:, hs], qg_ref[...], cos, sin, lane) * Q_SCALE
        k_ref[:, hs] = _qk_norm_rope(zk[:, hs], kg_ref[...], cos, sin, lane)
    v_ref[...] = seg(2)
    qh_ref[...] = jax.nn.silu(seg(3))
    lb = _lower_bound(lbl_ref)
    f = lb + (1.0 - lb) * jax.nn.sigmoid(seg(4))
    for h in range(N_HEADS):
        hs = slice(h * HEAD_W, (h + 1) * HEAD_W)
        ft_ref[hs, :] = f[:, hs].T
    vh_ref[...] = seg(5)
    og_ref[...] = jax.nn.silu(seg(6))
    ga_ref[...] = jax.nn.sigmoid(seg(7))
    gb_ref[...] = jax.nn.sigmoid(seg(8))


def _inproj_sample(x, sh, sc, gmix, w_in, qg, kg, cos, sin, lbl):
    m = x.shape[0]
    tok = jax.ShapeDtypeStruct((m, D_MODEL), F32)
    chan = jax.ShapeDtypeStruct((D_MODEL, m), F32)
    ts = _const_spec((m, D_MODEL))
    cs = _const_spec((D_MODEL, m))
    return pl.pallas_call(
        _inproj_sample_kernel,
        out_shape=(tok, tok, tok, tok, chan, tok, tok, tok, tok),
        grid=(1,),
        in_specs=[ts, ts, ts, _const_spec((1, D_MODEL)), _const_spec(w_in.shape),
                  _const_spec((1, HEAD_W)), _const_spec((1, HEAD_W)), _const_spec((1, HEAD_W)),
                  _const_spec((1, HEAD_W)), _const_spec(lbl.shape)],
        out_specs=(ts, ts, ts, ts, cs, ts, ts, ts, ts),
        compiler_params=_params(("arbitrary",)),
        name="inproj_sample",
    )(x, sh, sc, gmix, w_in, qg, kg, cos, sin, lbl)


def _hgrn_boundary_rows(cum, h):
    c, d = cum.shape
    if 2 * h >= 8:
        grp = 2 * h
        return jnp.concatenate(
            [jnp.broadcast_to(cum[i * grp + h - 1:i * grp + h, :], (grp, d)) for i in range(c // grp)], axis=0)
    cum3 = cum.reshape(c // 8, 8, d)
    sub = lax.broadcasted_iota(jnp.int32, (c // 8, 8, d), 1)
    out = None
    for first in range(0, 8, 2 * h):
        row = jnp.broadcast_to(cum3[:, first + h - 1:first + h, :], cum3.shape)
        out = row if out is None else jnp.where(sub >= first, row, out)
    return out.reshape(c, d)


def _hgrn_prompt_kernel(ltri_ref, q_ref, k_ref, g_ref, v_ref, og_ref, gn_ref, ob_ref, sout_ref, st_ref, *, ct):
    c = HGRN_CHUNK
    tstep = pl.program_id(1)

    @pl.when(tstep == 0)
    def _():
        st_ref[...] = jnp.zeros_like(st_ref)

    tt = lax.broadcasted_iota(jnp.int32, (c, c), 0)
    ss = lax.broadcasted_iota(jnp.int32, (c, c), 1)
    meet = [((tt // (2 * h)) == (ss // (2 * h))) & (((tt // h) % 2) == 1) & (((ss // h) % 2) == 0)
            for h in HGRN_LEVELS]
    eye = tt == ss
    ti = lax.broadcasted_iota(jnp.int32, (c, 1), 0)
    side = [jnp.where(((ti // h) % 2) == 1, 1.0, -1.0) for h in HGRN_LEVELS]
    ltri = ltri_ref[...]
    gn = gn_ref[...]

    def chunk(ci, carry):
        rows = pl.ds(pl.multiple_of(ci * c, c), c)
        g = g_ref[0, rows, :]
        g_hi = g.astype(BF16)
        r1 = g - g_hi.astype(F32)
        g_mid = r1.astype(BF16)
        g_lo = (r1 - g_mid.astype(F32)).astype(BF16)
        cum = (jnp.dot(ltri, g_hi, preferred_element_type=F32)
               + jnp.dot(ltri, g_mid, preferred_element_type=F32)
               + jnp.dot(ltri, g_lo, preferred_element_type=F32))
        q = q_ref[0, rows, :]
        k = k_ref[0, rows, :]
        v = v_ref[0, rows, :]
        og = og_ref[0, rows, :]
        q_in = q * jnp.exp(cum)
        k_out = k * jnp.exp(cum[c - 1:c, :] - cum)
        d_all = jnp.exp(cum[c - 1:c, :])
        ex = [jnp.exp((cum - _hgrn_boundary_rows(cum, h)) * side[li]).astype(BF16)
              for li, h in enumerate(HGRN_LEVELS)]
        qb = q.astype(BF16)
        kb = k.astype(BF16)
        hsl = [slice(h * HEAD_W, (h + 1) * HEAD_W) for h in range(N_HEADS)]
        als = [[lax.dot_general(qb[:, hs] * ex[li][:, hs], kb[:, hs] * ex[li][:, hs], NT_DIMS,
                                preferred_element_type=F32) for li in range(len(HGRN_LEVELS))] for hs in hsl]
        for h in range(N_HEADS):
            hs = hsl[h]
            a = jnp.where(eye, jnp.sum(q[:, hs] * k[:, hs], axis=-1, keepdims=True), 0.0)
            for li in range(len(HGRN_LEVELS)):
                a = jnp.where(meet[li], als[h][li], a)
            vb = v[:, hs].astype(BF16)
            st = st_ref[h]
            o = (jnp.dot(a.astype(BF16), vb, preferred_element_type=F32)
                 + lax.dot_general(q_in[:, hs].astype(BF16), st.astype(BF16), NT_DIMS, preferred_element_type=F32))
            st_ref[h] = d_all[:, hs] * st + lax.dot_general(vb, k_out[:, hs].astype(BF16), TN_DIMS,
                                                            preferred_element_type=F32)
            ob_ref[0, rows, hs] = (_rms(o, gn) * og[:, hs]).astype(BF16)
        return carry

    lax.fori_loop(0, ct // c, chunk, 0, unroll=True)

    @pl.when(tstep == pl.num_programs(1) - 1)
    def _():
        for h in range(N_HEADS):
            sout_ref[0, h] = st_ref[h].T


def _hgrn_prompt(qh, kh, gh, vh, og, gn, ct=256):
    b, t, _ = qh.shape
    dall = jnp.asarray(np.tril(np.ones((HGRN_CHUNK, HGRN_CHUNK), np.float32)), dtype=BF16)
    tok_spec = pl.BlockSpec((1, ct, D_MODEL), lambda i, j: (i, j, 0))
    return pl.pallas_call(
        functools.partial(_hgrn_prompt_kernel, ct=ct),
        out_shape=(jax.ShapeDtypeStruct((b, t, D_MODEL), BF16),
                   jax.ShapeDtypeStruct((b, N_HEADS, HEAD_W, HEAD_W), F32)),
        grid=(b, t // ct),
        in_specs=[_const_spec(dall.shape), tok_spec, tok_spec, tok_spec, tok_spec, tok_spec, _const_spec((1, HEAD_W))],
        out_specs=(tok_spec, pl.BlockSpec((1, N_HEADS, HEAD_W, HEAD_W), lambda i, j: (i, 0, 0, 0))),
        scratch_shapes=[pltpu.VMEM((N_HEADS, HEAD_W, HEAD_W), F32)],
        compiler_params=_params(("parallel", "arbitrary")),
        name="hgrn_prompt",
    )(dall, qh, kh, gh, vh, og, gn)


def _lambda(lam_ref, lam_init):
    lv = lam_ref[...]
    s1 = jnp.sum(lv[0:1] * lv[1:2], axis=-1, keepdims=True)
    s2 = jnp.sum(lv[2:3] * lv[3:4], axis=-1, keepdims=True)
    return jnp.exp(s1) - jnp.exp(s2) + lam_init


def _attn_prompt_step(lam_ref, g_ref, q_ref, k_ref, v_ref, o_ref, m_ref, acc_ref, *, tq, tk, lam_init,
                      before_diagonal=None):
    qi = pl.program_id(2)
    q = q_ref[0, 0]
    lane = lax.broadcasted_iota(jnp.int32, (1, HEAD_W), 1)
    zero = jnp.zeros_like(q)
    qs = jnp.concatenate([jnp.where(lane < A_DH, q, zero), jnp.where(lane < A_DH, zero, q)], axis=0)
    rg = ATTN_ROW_GROUP
    ng = 2 * tq // rg

    def block(k0, masked):
        geo = []
        for g in range(ng):
            r0 = (g * rg) % tq
            width = min(tk, r0 + rg) if masked else tk
            geo.append((r0, slice(g * rg, (g + 1) * rg), width, pl.ds(pl.multiple_of(k0, tk), width)))

        def scores(g):
            _, gr, _, rows = geo[g]
            return lax.dot_general(qs[gr], k_ref[0, 0, rows, :], NT_DIMS, preferred_element_type=F32)

        ss = [scores(g) for g in range(ATTN_AHEAD)]
        for g, (r0, gr, width, rows) in enumerate(geo):
            s = ss[g]
            if g + ATTN_AHEAD < ng:
                ss.append(scores(g + ATTN_AHEAD))
            if masked:
                qpos = r0 + lax.broadcasted_iota(jnp.int32, (rg, width), 0)
                kpos = lax.broadcasted_iota(jnp.int32, (rg, width), 1)
                s = jnp.where(kpos <= qpos, s, NEG_BIG)
            m_old = m_ref[gr, :]
            m_new = jnp.maximum(m_old, jnp.max(s, axis=-1, keepdims=True))
            alpha = jnp.exp2(m_old - m_new)
            p = jnp.exp2(s - jnp.tile(m_new, (1, width // HEAD_W)))
            pv = jnp.dot(p.astype(BF16), v_ref[0, 0, rows, :], preferred_element_type=F32)
            m_ref[gr, :] = m_new
            acc_ref[gr, :] = jnp.tile(alpha, (1, 2)) * acc_ref[gr, :] + pv

    m_ref[...] = jnp.full(m_ref.shape, NEG_BIG, F32)
    acc_ref[...] = jnp.zeros(acc_ref.shape, F32)

    def two_blocks(i, carry):
        block(2 * i * tk, False)
        block((2 * i + 1) * tk, False)
        return carry

    lax.fori_loop(0, qi // 2, two_blocks, 0)

    @pl.when(qi % 2 == 1)
    def _():
        block((qi - 1) * tk, False)

    if before_diagonal is not None:
        before_diagonal()
    block(qi * tq, True)
    acc = acc_ref[...]
    o = acc[:, 0:HEAD_W] / acc[:, HEAD_W:2 * HEAD_W]
    lam = _lambda(lam_ref, lam_init)
    od = o[0:tq] - lam * o[tq:2 * tq]
    o_ref[0, 0] = _rms(od, g_ref[...]).astype(BF16)


def _mix_kernel(x_ref, gm_ref, oa_ref, ob_ref, ga_ref, gb_ref, wa_ref, wb_ref, wo_ref, o_ref):
    oa = jnp.concatenate([oa_ref[0, h] for h in range(N_HEADS)], axis=-1)
    ya = jnp.dot(oa, wa_ref[...], preferred_element_type=F32)
    yb = jnp.dot(ob_ref[0].astype(BF16), wb_ref[...], preferred_element_type=F32)
    merged = ga_ref[0] * ya + gb_ref[0] * yb
    mix = jnp.dot(merged.astype(BF16), wo_ref[...], preferred_element_type=F32)
    o_ref[0] = x_ref[0] + gm_ref[0] * mix


def _mix(x, gm, oa_hm, ob, ga, gb, wa, wb, wo, tm):
    b, t, _ = x.shape
    rows_mod = gm.shape[1]
    tok_spec = pl.BlockSpec((1, tm, D_MODEL), lambda i, j: (i, j, 0))
    mod_spec = (pl.BlockSpec((1, 1, D_MODEL), lambda i, j: (i, 0, 0)) if rows_mod == 1 else tok_spec)
    return pl.pallas_call(
        _mix_kernel,
        out_shape=jax.ShapeDtypeStruct((b, t, D_MODEL), F32),
        grid=(b, t // tm),
        in_specs=[tok_spec, mod_spec, pl.BlockSpec((1, N_HEADS, tm, HEAD_W), lambda i, j: (i, 0, j, 0)),
                  tok_spec, tok_spec, tok_spec,
                  _const_spec(wa.shape), _const_spec(wb.shape), _const_spec(wo.shape)],
        out_specs=tok_spec,
        compiler_params=_params(("parallel", "parallel")),
        name="mix_out",
    )(x, gm, oa_hm, ob, ga, gb, wa, wb, wo)


def _ffn_kernel(*refs, tm, carried):
    if carried:
        (x_ref, sh_ref, sc_ref, gt_ref, g_ref, wup_ref, cw_ref, cb_ref, wdn_ref, y_ref, tail_ref, prev_ref) = refs
    else:
        (x_ref, sh_ref, sc_ref, gt_ref, g_ref, wup_ref, cw_ref, cb_ref, wdn_ref, b0_ref, b1_ref, y_ref, u_ref) = refs
    x = x_ref[0]
    hn = (_rms(x, g_ref[...]) * (1.0 + sc_ref[0]) + sh_ref[0]).astype(BF16)
    if carried:
        @pl.when(pl.program_id(1) == 0)
        def _():
            prev_ref[...] = jnp.zeros_like(prev_ref)
        row = lax.broadcasted_iota(jnp.int32, (tm, 1), 0)

    def conv(u, cols):
        if carried:
            p6 = prev_ref[6:7, cols]
            p7 = prev_ref[7:8, cols]
            u1 = jnp.where(row == 0, p7, pltpu.roll(u, 1, 0))
            u2 = jnp.where(row == 0, p6, jnp.where(row == 1, p7, pltpu.roll(u, 2, 0)))
            prev_ref[:, cols] = u[tm - 8:tm, :]
            tail_ref[0, :, cols] = u[tm - 8:tm, :]
        else:
            u2 = b0_ref[0, :, cols]
            u1 = b1_ref[0, :, cols]
            u_ref[0, :, cols] = u
        return cb_ref[:, cols] + cw_ref[0:1, cols] * u2 + cw_ref[1:2, cols] * u1 + cw_ref[2:3, cols] * u

    nchunk = D_FF // FFN_COLS

    def cols(c):
        return slice(c * FFN_COLS, (c + 1) * FFN_COLS), slice(D_FF + c * FFN_COLS, D_FF + (c + 1) * FFN_COLS)

    def up(c):
        ca, cg = cols(c)
        return (jnp.dot(hn, wup_ref[:, ca], preferred_element_type=F32),
                jnp.dot(hn, wup_ref[:, cg], preferred_element_type=F32))

    acc = jnp.zeros((tm, D_MODEL), F32)
    ups = [up(c) for c in range(FFN_AHEAD)]
    acts = []
    for c in range(nchunk):
        ua, ug = ups[c]
        if c + FFN_AHEAD < nchunk:
            ups.append(up(c + FFN_AHEAD))
        ca, cg = cols(c)
        acts.append((jax.nn.silu(conv(ua, ca)) * conv(ug, cg)).astype(BF16))
        if c >= FFN_BEHIND:
            acc = acc + jnp.dot(acts[c - FFN_BEHIND], wdn_ref[cols(c - FFN_BEHIND)[0], :],
                                preferred_element_type=F32)
    for c in range(nchunk - FFN_BEHIND, nchunk):
        acc = acc + jnp.dot(acts[c], wdn_ref[cols(c)[0], :], preferred_element_type=F32)
    y_ref[0] = x + gt_ref[0] * acc


def _ffn(x1, sh, sc, gt, g, wup, cw, cb, wdn, tm, bufs=None):
    b, t, _ = x1.shape
    carried = bufs is None
    tok_spec = pl.BlockSpec((1, tm, D_MODEL), lambda i, j: (i, j, 0))
    mod_spec = (pl.BlockSpec((1, 1, D_MODEL), lambda i, j: (i, 0, 0)) if sh.shape[1] == 1 else tok_spec)
    wide_spec = pl.BlockSpec((1, tm, 2 * D_FF), lambda i, j: (i, j, 0))
    in_specs = [tok_spec, mod_spec, mod_spec, mod_spec, _const_spec((1, D_MODEL)), _const_spec(wup.shape),
                _const_spec(cw.shape), _const_spec(cb.shape), _const_spec(wdn.shape)]
    args = [x1, sh, sc, gt, g, wup, cw, cb, wdn]
    if carried:
        out_shape = (jax.ShapeDtypeStruct((b, t, D_MODEL), F32), jax.ShapeDtypeStruct((b, 8, 2 * D_FF), F32))
        out_specs = (tok_spec, pl.BlockSpec((1, 8, 2 * D_FF), lambda i, j: (i, 0, 0)))
        scratch = [pltpu.VMEM((8, 2 * D_FF), F32)]
        sem = ("parallel", "arbitrary")
    else:
        in_specs += [wide_spec, wide_spec]
        args += list(bufs)
        out_shape = (jax.ShapeDtypeStruct((b, t, D_MODEL), F32), jax.ShapeDtypeStruct((b, t, 2 * D_FF), F32))
        out_specs = (tok_spec, wide_spec)
        scratch = []
        sem = ("parallel", "parallel")
    return pl.pallas_call(
        functools.partial(_ffn_kernel, tm=tm, carried=carried),
        out_shape=out_shape,
        grid=(b, t // tm),
        in_specs=in_specs,
        out_specs=out_specs,
        scratch_shapes=scratch,
        compiler_params=_params(sem),
        name="conv_ffn",
    )(*args)


def _hgrn_sample_kernel(ft_ref, q_ref, v_ref, og_ref, gn_ref, s_ref, so_ref, o_ref, *, nb):
    ft = ft_ref[...]
    first_row = lax.broadcasted_iota(jnp.int32, (2 * N_HEADS, HEAD_W), 0) == 0
    for b in range(nb):
        f = ft[:, b:b + 1]
        s_new = f * s_ref[b, 0] + (1.0 - f) * v_ref[b:b + 1, :]
        so_ref[b, 0] = s_new
        lhs = jnp.where(first_row, q_ref[b:b + 1, :], 0.0).astype(BF16)
        o_ref[b:b + 1, :] = jnp.dot(lhs, s_new.astype(BF16), preferred_element_type=F32)[0:1, :]
    o_ref[...] = _rms(o_ref[...], gn_ref[...]) * og_ref[...]


def _hgrn_sample(ft, qh, vh, og, gn, state):
    nb = vh.shape[0]
    st_spec = pl.BlockSpec((nb, 1, HEAD_W, HEAD_W), lambda h: (0, h, 0, 0))
    hd_spec = pl.BlockSpec((nb, HEAD_W), lambda h: (0, h))
    return pl.pallas_call(
        functools.partial(_hgrn_sample_kernel, nb=nb),
        out_shape=(jax.ShapeDtypeStruct(state.shape, F32), jax.ShapeDtypeStruct((nb, D_MODEL), F32)),
        grid=(N_HEADS,),
        in_specs=[pl.BlockSpec((HEAD_W, nb), lambda h: (h, 0)), hd_spec, hd_spec, hd_spec, _const_spec((1, HEAD_W)),
                  st_spec],
        out_specs=(st_spec, hd_spec),
        compiler_params=_params(("parallel",)),
        name="hgrn_sample",
    )(ft, qh, vh, og, gn, state)


def _page_copies(pt_ref, ck_hbm, cv_hbm, kbuf, vbuf, sem, b, n_pages):
    sl = b % 2
    out = []
    for p in range(n_pages):
        pg = pt_ref[b, p]
        out.append(pltpu.make_async_copy(ck_hbm.at[pg], kbuf.at[sl, p], sem.at[0, sl]))
        out.append(pltpu.make_async_copy(cv_hbm.at[pg], vbuf.at[sl, p], sem.at[1, sl]))
    return out


def _attn_sample_one(b, lam_ref, g_ref, q_ref, kn_ref, vn_ref, o_ref, kbuf, vbuf, *, n_pages, page, lam_init):
    slot = b % 2
    q = q_ref[pl.ds(b, 1), :]
    r16 = lax.broadcasted_iota(jnp.int32, (2 * N_HEADS, D_MODEL), 0)
    ln = lax.broadcasted_iota(jnp.int32, (2 * N_HEADS, D_MODEL), 1)
    sel = ((ln // HEAD_W) == (r16 % N_HEADS)) & (((ln // A_DH) % 2) == (r16 // N_HEADS))
    qsel = jnp.where(sel, q, 0.0)
    qblk = qsel.astype(BF16)
    s = jnp.concatenate([jnp.dot(qblk, kbuf[slot, p].astype(BF16), preferred_element_type=F32)
                         for p in range(n_pages)], axis=1)
    s_new = jnp.sum(qsel * kn_ref[pl.ds(b, 1), :], axis=-1, keepdims=True)
    m = jnp.maximum(jnp.max(s, axis=-1, keepdims=True), s_new)
    e = jnp.exp2(s - m)
    e_new = jnp.exp2(s_new - m)
    inv_l = 1.0 / (jnp.sum(e, axis=-1, keepdims=True) + e_new)
    lam = _lambda(lam_ref, lam_init)
    pr = e * inv_l
    pr_new = e_new * inv_l
    pd = pr[0:N_HEADS] - lam * pr[N_HEADS:2 * N_HEADS]
    pd_new = pr_new[0:N_HEADS] - lam * pr_new[N_HEADS:2 * N_HEADS]

    per = HEAD_W // N_HEADS
    lane = lax.broadcasted_iota(jnp.int32, (N_HEADS, HEAD_W), 1)
    own = (lane % N_HEADS) == lax.broadcasted_iota(jnp.int32, (N_HEADS, HEAD_W), 0)
    pad = jnp.zeros((N_HEADS, HEAD_W), F32)

    acc = jnp.zeros((2 * N_HEADS, HEAD_W), F32)
    for p in range(n_pages):
        w = pd[:, p * page:(p + 1) * page]
        for g in range(page // per):
            wx = jnp.take_along_axis(w, g * per + lane // N_HEADS, axis=1)
            lhs = jnp.concatenate([jnp.where(own, wx, 0.0), pad], axis=0).astype(BF16)
            rhs = vbuf[slot, p, pl.ds(g * per, per)].reshape(HEAD_W, HEAD_W).astype(BF16)
            acc = acc + jnp.dot(lhs, rhs, preferred_element_type=F32)
    o_ref[b] = _rms(acc[0:N_HEADS] + pd_new * vn_ref[b], g_ref[...])


def _attn_kernel(pt_ref, lam_ref, g_ref, q_ref, k_ref, v_ref, qs_ref, kn_ref, vn_ref, ck_hbm, cv_hbm,
                 o_ref, os_ref, kbuf, vbuf, m_ref, acc_ref, sem, *, tq, lam_init, n_pages, page, per_step):
    step = (pl.program_id(0) * pl.num_programs(1) + pl.program_id(1)) * pl.num_programs(2) + pl.program_id(2)
    n_steps = pl.num_programs(0) * pl.num_programs(1) * pl.num_programs(2)
    first = step * per_step
    copies = functools.partial(_page_copies, pt_ref, ck_hbm, cv_hbm, kbuf, vbuf, sem, n_pages=n_pages)

    @pl.when(step == 0)
    def _():
        for cp in copies(0):
            cp.start()

    if per_step == 1:
        @pl.when(step + 1 < n_steps)
        def _():
            for cp in copies(step + 1):
                cp.start()

    def wait_first():
        for cp in copies(first):
            cp.wait()

    _attn_prompt_step(lam_ref, g_ref, q_ref, k_ref, v_ref, o_ref, m_ref, acc_ref, tq=tq, tk=tq, lam_init=lam_init,
                      before_diagonal=wait_first)

    for j in range(per_step):
        b = first + j
        if per_step > 1:
            @pl.when(b + 1 < n_steps * per_step)
            def _():
                for cp in copies(b + 1):
                    cp.start()
        if j > 0:
            for cp in copies(b):
                cp.wait()
        _attn_sample_one(b, lam_ref, g_ref, qs_ref, kn_ref, vn_ref, os_ref, kbuf, vbuf,
                         n_pages=n_pages, page=page, lam_init=lam_init)


def _attention(page_table, lam4, g_sub, q_hm, k_hm, v_hm, q_s, k_new, v_new, cache_kt, cache_v, lam_init, tq=1024):
    b, nh, t, _ = q_hm.shape
    nb, n_pages = page_table.shape
    page = cache_v.shape[1]
    n_steps = b * nh * (t // tq)
    assert nb % n_steps == 0, (nb, n_steps)
    full = lambda shape: pl.BlockSpec(shape, lambda i, h, j, pt: (0,) * len(shape))
    any_spec = pl.BlockSpec(memory_space=pl.ANY)
    k_spec = pl.BlockSpec((1, 1, t, HEAD_W), lambda i, h, j, pt: (i, h, 0, 0))
    v_spec = pl.BlockSpec((1, 1, t, 2 * HEAD_W), lambda i, h, j, pt: (i, h, 0, 0))
    q_spec = pl.BlockSpec((1, 1, tq, HEAD_W), lambda i, h, j, pt: (i, h, j, 0))
    grid_spec = pltpu.PrefetchScalarGridSpec(
        num_scalar_prefetch=1,
        grid=(b, nh, t // tq),
        in_specs=[full(lam4.shape), full((1, HEAD_W)), q_spec, k_spec, v_spec,
                  full(q_s.shape), full(k_new.shape), full(v_new.shape), any_spec, any_spec],
        out_specs=(q_spec, full((nb, N_HEADS, HEAD_W))),
        scratch_shapes=[pltpu.VMEM((2, n_pages, D_MODEL, page), F32),
                        pltpu.VMEM((2, n_pages, page, N_HEADS, HEAD_W), F32),
                        pltpu.VMEM((2 * tq, HEAD_W), F32), pltpu.VMEM((2 * tq, 2 * HEAD_W), F32),
                        pltpu.SemaphoreType.DMA((2, 2))],
    )
    return pl.pallas_call(
        functools.partial(_attn_kernel, tq=tq, lam_init=lam_init, n_pages=n_pages, page=page,
                          per_step=nb // n_steps),
        out_shape=(jax.ShapeDtypeStruct((b, nh, t, HEAD_W), BF16), jax.ShapeDtypeStruct((nb, N_HEADS, HEAD_W), F32)),
        grid_spec=grid_spec,
        compiler_params=_params(("arbitrary", "arbitrary", "arbitrary")),
        name="attention",
    )(page_table, lam4, g_sub, q_hm, k_hm, v_hm, q_s, k_new, v_new, cache_kt, cache_v)


def _rope_tables(pos):
    half = A_DH // 2
    inv = ROPE_THETA ** (-np.arange(half, dtype=np.float64) / half)
    ang = np.asarray(pos, np.float64)[:, None] * inv[None, :]
    cos = np.cos(ang)
    sin = np.sin(ang)
    return (jnp.asarray(np.tile(cos, (1, 4)), F32),
            jnp.asarray(np.concatenate([-sin, sin, -sin, sin], axis=-1), F32))


def kernel(x_prompt, x_sample, cache_k, cache_v, state_hgrn, state_conv, page_table, c_prompt, c_sample, w_ada, b_ada,
           rms_mix_g, w_in, q_norm_g, k_norm_g, lambda_q1, lambda_k1, lambda_q2, lambda_k2, subln_g, lb_logits,
           hgrn_norm_g, w_branch_a, w_branch_b, w_out, rms_ffn_g, w_up, conv_w, conv_b, w_down):
    bp, tp, d = x_prompt.shape
    bs, ts, _ = x_sample.shape
    assert d == D_MODEL and ts == 1 and w_in.shape[0] == 1
    n_pool, page = cache_k.shape[1], cache_k.shape[2]
    past_len = page_table.shape[1] * page
    lam_init = 0.8 - 0.6 * math.exp(-0.3 * 0)

    pad = (-(bp + bs)) % 8
    c_all = jnp.concatenate([c_prompt, c_sample, jnp.zeros((pad, d), F32)], axis=0)
    mod = _ada(c_all, w_ada[0], b_ada[0].reshape(1, -1))
    mp = [mod[:bp, i * d:(i + 1) * d].reshape(bp, 1, d) for i in range(6)]
    msamp = [mod[bp:bp + bs, i * d:(i + 1) * d] for i in range(6)]

    w_in_b = w_in[0].astype(BF16)
    wa_b = w_branch_a[0].astype(BF16)
    wb_b = w_branch_b[0].astype(BF16)
    wo_b = w_out[0].astype(BF16)
    wup_b = w_up[0].astype(BF16)
    wdn_b = w_down[0].astype(BF16)
    gmix = rms_mix_g[0].reshape(1, d)
    gffn = rms_ffn_g[0].reshape(1, d)
    qg = jnp.tile(q_norm_g[0], 2).reshape(1, HEAD_W)
    kg = jnp.tile(k_norm_g[0], 2).reshape(1, HEAD_W)
    lam4 = jnp.stack([lambda_q1[0], lambda_k1[0], lambda_q2[0], lambda_k2[0]], axis=0)
    g_sub = (subln_g[0] * (1.0 - lam_init)).reshape(1, HEAD_W)
    gn = hgrn_norm_g[0].reshape(1, HEAD_W)
    cw = conv_w[0]
    cb = conv_b[0].reshape(1, -1)

    cos_p, sin_p = _rope_tables(np.arange(tp))
    (q_hm, k_tok, k_hm, v_tok, v_hm, qh, kh, gh, vh, og, ga, gb) = _inproj_prompt(
        x_prompt, mp[0], mp[1], gmix, w_in_b, qg, kg, cos_p, sin_p, lb_logits)
    ob, s_prompt = _hgrn_prompt(qh, kh, gh, vh, og, gn)
    cos_s, sin_s = _rope_tables(np.full((1,), past_len))
    xs = x_sample.reshape(bs, d)
    (q_s, k_s, v_s, qh_s, ft_s, vh_s, og_s, ga_s, gb_s) = _inproj_sample(
        xs, msamp[0], msamp[1], gmix, w_in_b, qg, kg, cos_s, sin_s, lb_logits)
    s_sample, ob_s = _hgrn_sample(ft_s, qh_s, vh_s, og_s, gn, state_hgrn[0])

    cache_kt = jnp.transpose(cache_k[0], (0, 2, 3, 4, 1)).reshape(n_pool, d, page)
    oa_hm, oa_s = _attention(page_table, lam4, g_sub, q_hm, k_hm, v_hm, q_s, k_s, v_s.reshape(bs, N_HEADS, HEAD_W),
                             cache_kt, cache_v[0], lam_init)

    x1 = _mix(x_prompt, mp[2], oa_hm, ob, ga, gb, wa_b, wb_b, wo_b, tm=512)
    y_prompt, tail = _ffn(x1, mp[3], mp[4], mp[5], gffn, wup_b, cw, cb, wdn_b, tm=512)
    oa_s = jnp.transpose(oa_s, (1, 0, 2))
    x1_s = _mix(xs[None], msamp[2][None], oa_s[None].astype(BF16), ob_s[None], ga_s[None], gb_s[None],
                wa_b, wb_b, wo_b, tm=bs)
    y_s, u_s = _ffn(x1_s, msamp[3][None], msamp[4][None], msamp[5][None], gffn, wup_b, cw, cb, wdn_b, tm=bs,
                    bufs=(state_conv[0][:, 0][None], state_conv[0][:, 1][None]))

    new_conv_s = jnp.stack([state_conv[0][:, 1], u_s[0]], axis=1)
    return (y_prompt, y_s.reshape(bs, 1, d),
            k_tok.reshape(1, bp, tp, N_HEADS, 2, A_DH), v_tok.reshape(1, bp, tp, N_HEADS, HEAD_W),
            s_prompt[None], tail[:, 6:8][None],
            k_s.reshape(1, bs, 1, N_HEADS, 2, A_DH), v_s.reshape(1, bs, 1, N_HEADS, HEAD_W),
            s_sample[None], new_conv_s[None])
```

```python
import functools
import math

import numpy as np
import jax
import jax.numpy as jnp
from jax import lax
from jax.experimental import pallas as pl
from jax.experimental.pallas import tpu as pltpu

F32 = jnp.float32
BF16 = jnp.bfloat16

D_MODEL = 1024
N_HEADS = 8
HEAD_W = 128
A_DH = 64
D_FF = 2816
ROPE_THETA = 10000.0
EPS = 1e-6
NEG_BIG = -1e30
HGRN_CHUNK = 64
HGRN_LEVELS = (32, 16, 8, 4, 2, 1)
FFN_COLS = 256
FFN_AHEAD = 6
FFN_BEHIND = 3
VMEM_LIMIT = 56 * 1024 * 1024
Q_SCALE = (A_DH ** -0.5) * math.log2(math.e)

ATTN_ROW_GROUP = 256
ATTN_AHEAD = 2
NT_DIMS = (((1,), (1,)), ((), ()))
TN_DIMS = (((0,), (0,)), ((), ()))


def _const_spec(shape):
    nd = len(shape)
    return pl.BlockSpec(shape, lambda *_: (0,) * nd, pipeline_mode=pl.Buffered(1))


def _params(sem, vmem=VMEM_LIMIT):
    return pltpu.CompilerParams(dimension_semantics=sem, vmem_limit_bytes=vmem)


def _rms(x, g):
    ms = jnp.mean(x * x, axis=-1, keepdims=True)
    return (x * lax.rsqrt(ms + EPS)) * g


def _ada_kernel(c_ref, w_ref, b_ref, o_ref):
    a = jax.nn.silu(c_ref[...]).astype(BF16)
    o_ref[...] = jnp.dot(a, w_ref[...].astype(BF16), preferred_element_type=F32) + b_ref[...]


def _ada(c_all, w_ada, b_ada):
    m = c_all.shape[0]
    n = w_ada.shape[1]
    tn = 1536
    return pl.pallas_call(
        _ada_kernel,
        out_shape=jax.ShapeDtypeStruct((m, n), F32),
        grid=(n // tn,),
        in_specs=[
            _const_spec((m, D_MODEL)),
            pl.BlockSpec((D_MODEL, tn), lambda j: (0, j)),
            pl.BlockSpec((1, tn), lambda j: (0, j)),
        ],
        out_specs=pl.BlockSpec((m, tn), lambda j: (0, j)),
        compiler_params=_params(("arbitrary",)),
        name="ada_mod",
    )(c_all, w_ada, b_ada)


def _qk_norm_rope(z, g128, cos, sin, lane):
    sq = z * z
    first = lane < A_DH
    lo = jnp.sum(jnp.where(first, sq, 0.0), axis=-1, keepdims=True)
    al = jnp.sum(sq, axis=-1, keepdims=True)
    ms = jnp.where(first, lo, al - lo) * (1.0 / A_DH)
    zn = (z * lax.rsqrt(ms + EPS)) * g128
    rot = jnp.where((lane % A_DH) < A_DH // 2, pltpu.roll(zn, HEAD_W - A_DH // 2, 1), pltpu.roll(zn, A_DH // 2, 1))
    return zn * cos + rot * sin


def _lower_bound(lbl_ref):
    lbl = lbl_ref[...]
    mx = jnp.max(lbl, axis=0, keepdims=True)
    e = jnp.exp(lbl - mx)
    return e[0:1, :] / jnp.sum(e, axis=0, keepdims=True)


def _inproj_common(x, sh, sc, gmix_ref, w_ref):
    xn = _rms(x, gmix_ref[...]) * (1.0 + sc) + sh
    xb = xn.astype(BF16)

    def seg(i):
        return jnp.dot(xb, w_ref[:, i * D_MODEL:(i + 1) * D_MODEL], preferred_element_type=F32)

    return seg


def _inproj_prompt_kernel(x_ref, sh_ref, sc_ref, gmix_ref, w_ref, qg_ref, kg_ref, cos_ref, sin_ref, lbl_ref,
                          qhm_ref, ktok_ref, khm_ref, vtok_ref, vhm_ref,
                          qh_ref, kh_ref, gh_ref, vh_ref, og_ref, ga_ref, gb_ref):
    seg = _inproj_common(x_ref[0], sh_ref[0], sc_ref[0], gmix_ref, w_ref)
    cos = cos_ref[...]
    sin = sin_ref[...]
    lane = lax.broadcasted_iota(jnp.int32, (1, HEAD_W), 1)
    zq = seg(0)
    zk = seg(1)
    for h in range(N_HEADS):
        hs = slice(h * HEAD_W, (h + 1) * HEAD_W)
        q = _qk_norm_rope(zq[:, hs], qg_ref[...], cos, sin, lane) * Q_SCALE
        qhm_ref[0, h] = q.astype(BF16)
        k = _qk_norm_rope(zk[:, hs], kg_ref[...], cos, sin, lane)
        ktok_ref[0, :, hs] = k
        khm_ref[0, h] = k.astype(BF16)
    zv = seg(2)
    vtok_ref[0] = zv
    ones = jnp.ones((zv.shape[0], HEAD_W), BF16)
    for h in range(N_HEADS):
        vhm_ref[0, h, :, 0:HEAD_W] = zv[:, h * HEAD_W:(h + 1) * HEAD_W].astype(BF16)
        vhm_ref[0, h, :, HEAD_W:2 * HEAD_W] = ones
    qh_ref[0] = jax.nn.silu(seg(3))
    lb = _lower_bound(lbl_ref)
    f = lb + (1.0 - lb) * jax.nn.sigmoid(seg(4))
    kh_ref[0] = 1.0 - f
    gh_ref[0] = jnp.log(f)
    vh_ref[0] = seg(5)
    og_ref[0] = jax.nn.silu(seg(6))
    ga_ref[0] = jax.nn.sigmoid(seg(7))
    gb_ref[0] = jax.nn.sigmoid(seg(8))


def _inproj_prompt(x, sh, sc, gmix, w_in, qg, kg, cos, sin, lbl, tm=256):
    b, t, _ = x.shape
    tok = jax.ShapeDtypeStruct((b, t, D_MODEL), F32)
    hm = jax.ShapeDtypeStruct((b, N_HEADS, t, HEAD_W), BF16)
    hm2 = jax.ShapeDtypeStruct((b, N_HEADS, t, 2 * HEAD_W), BF16)
    tok_spec = pl.BlockSpec((1, tm, D_MODEL), lambda i, j: (i, j, 0))
    hm_spec = pl.BlockSpec((1, N_HEADS, tm, HEAD_W), lambda i, j: (i, 0, j, 0))
    hm2_spec = pl.BlockSpec((1, N_HEADS, tm, 2 * HEAD_W), lambda i, j: (i, 0, j, 0))
    mod_spec = pl.BlockSpec((1, 1, D_MODEL), lambda i, j: (i, 0, 0))
    tab_spec = pl.BlockSpec((tm, HEAD_W), lambda i, j: (j, 0))
    return pl.pallas_call(
        _inproj_prompt_kernel,
        out_shape=(hm, tok, hm, tok, hm2, tok, tok, tok, tok, tok, tok, tok),
        grid=(b, t // tm),
        in_specs=[tok_spec, mod_spec, mod_spec, _const_spec((1, D_MODEL)), _const_spec(w_in.shape),
                  _const_spec((1, HEAD_W)), _const_spec((1, HEAD_W)), tab_spec, tab_spec, _const_spec(lbl.shape)],
        out_specs=(hm_spec, tok_spec, hm_spec, tok_spec, hm2_spec,
                   tok_spec, tok_spec, tok_spec, tok_spec, tok_spec, tok_spec, tok_spec),
        compiler_params=_params(("parallel", "parallel")),
        name="inproj_prompt",
    )(x, sh, sc, gmix, w_in, qg, kg, cos, sin, lbl)


def _inproj_sample_kernel(x_ref, sh_ref, sc_ref, gmix_ref, w_ref, qg_ref, kg_ref, cos_ref, sin_ref, lbl_ref,
                          q_ref, k_ref, v_ref, qh_ref, ft_ref, vh_ref, og_ref, ga_ref, gb_ref):
    seg = _inproj_common(x_ref[...], sh_ref[...], sc_ref[...], gmix_ref, w_ref)
    cos = cos_ref[...]
    sin = sin_ref[...]
    lane = lax.broadcasted_iota(jnp.int32, (1, HEAD_W), 1)
    zq = seg(0)
    zk = seg(1)
    for h in range(N_HEADS):
        hs = slice(h * HEAD_W, (h + 1) * HEAD_W)
        q_ref[:, hs] = _qk_norm_rope(zq[:, hs], qg_ref[...], cos, sin, lane) * Q_SCALE
        k_ref[:, hs] = _qk_norm_rope(zk[:, hs], kg_ref[...], cos, sin, lane)
    v_ref[...] = seg(2)
    qh_ref[...] = jax.nn.silu(seg(3))
    lb = _lower_bound(lbl_ref)
    f = lb + (1.0 - lb) * jax.nn.sigmoid(seg(4))
    for h in range(N_HEADS):
        hs = slice(h * HEAD_W, (h + 1) * HEAD_W)
        ft_ref[hs, :] = f[:, hs].T
    vh_ref[...] = seg(5)
    og_ref[...] = jax.nn.silu(seg(6))
    ga_ref[...] = jax.nn.sigmoid(seg(7))
    gb_ref[...] = jax.nn.sigmoid(seg(8))


def _inproj_sample(x, sh, sc, gmix, w_in, qg, kg, cos, sin, lbl):
    m = x.shape[0]
    tok = jax.ShapeDtypeStruct((m, D_MODEL), F32)
    chan = jax.ShapeDtypeStruct((D_MODEL, m), F32)
    ts = _const_spec((m, D_MODEL))
    cs = _const_spec((D_MODEL, m))
    return pl.pallas_call(
        _inproj_sample_kernel,
        out_shape=(tok, tok, tok, tok, chan, tok, tok, tok, tok),
        grid=(1,),
        in_specs=[ts, ts, ts, _const_spec((1, D_MODEL)), _const_spec(w_in.shape),
                  _const_spec((1, HEAD_W)), _const_spec((1, HEAD_W)), _const_spec((1, HEAD_W)),
                  _const_spec((1, HEAD_W)), _const_spec(lbl.shape)],
        out_specs=(ts, ts, ts, ts, cs, ts, ts, ts, ts),
        compiler_params=_params(("arbitrary",)),
        name="inproj_sample",
    )(x, sh, sc, gmix, w_in, qg, kg, cos, sin, lbl)


def _hgrn_boundary_rows(cum, h):
    c, d = cum.shape
    if 2 * h >= 8:
        grp = 2 * h
        return jnp.concatenate(
            [jnp.broadcast_to(cum[i * grp + h - 1:i * grp + h, :], (grp, d)) for i in range(c // grp)], axis=0)
    cum3 = cum.reshape(c // 8, 8, d)
    sub = lax.broadcasted_iota(jnp.int32, (c // 8, 8, d), 1)
    out = None
    for first in range(0, 8, 2 * h):
        row = jnp.broadcast_to(cum3[:, first + h - 1:first + h, :], cum3.shape)
        out = row if out is None else jnp.where(sub >= first, row, out)
    return out.reshape(c, d)


def _hgrn_prompt_kernel(ltri_ref, q_ref, k_ref, g_ref, v_ref, og_ref, gn_ref, ob_ref, sout_ref, st_ref, *, ct):
    c = HGRN_CHUNK
    tstep = pl.program_id(1)

    @pl.when(tstep == 0)
    def _():
        st_ref[...] = jnp.zeros_like(st_ref)

    tt = lax.broadcasted_iota(jnp.int32, (c, c), 0)
    ss = lax.broadcasted_iota(jnp.int32, (c, c), 1)
    meet = [((tt // (2 * h)) == (ss // (2 * h))) & (((tt // h) % 2) == 1) & (((ss // h) % 2) == 0)
            for h in HGRN_LEVELS]
    eye = tt == ss
    ti = lax.broadcasted_iota(jnp.int32, (c, 1), 0)
    side = [jnp.where(((ti // h) % 2) == 1, 1.0, -1.0) for h in HGRN_LEVELS]
    ltri = ltri_ref[...]
    gn = gn_ref[...]

    def chunk(ci, carry):
        rows = pl.ds(pl.multiple_of(ci * c, c), c)
        g = g_ref[0, rows, :]
        g_hi = g.astype(BF16)
        r1 = g - g_hi.astype(F32)
        g_mid = r1.astype(BF16)
        g_lo = (r1 - g_mid.astype(F32)).astype(BF16)
        cum = (jnp.dot(ltri, g_hi, preferred_element_type=F32)
               + jnp.dot(ltri, g_mid, preferred_element_type=F32)
               + jnp.dot(ltri, g_lo, preferred_element_type=F32))
        q = q_ref[0, rows, :]
        k = k_ref[0, rows, :]
        v = v_ref[0, rows, :]
        og = og_ref[0, rows, :]
        q_in = q * jnp.exp(cum)
        k_out = k * jnp.exp(cum[c - 1:c, :] - cum)
        d_all = jnp.exp(cum[c - 1:c, :])
        ex = [jnp.exp((cum - _hgrn_boundary_rows(cum, h)) * side[li]).astype(BF16)
              for li, h in enumerate(HGRN_LEVELS)]
        qb = q.astype(BF16)
        kb = k.astype(BF16)
        hsl = [slice(h * HEAD_W, (h + 1) * HEAD_W) for h in range(N_HEADS)]
        als = [[lax.dot_general(qb[:, hs] * ex[li][:, hs], kb[:, hs] * ex[li][:, hs], NT_DIMS,
                                preferred_element_type=F32) for li in range(len(HGRN_LEVELS))] for hs in hsl]
        for h in range(N_HEADS):
            hs = hsl[h]
            a = jnp.where(eye, jnp.sum(q[:, hs] * k[:, hs], axis=-1, keepdims=True), 0.0)
            for li in range(len(HGRN_LEVELS)):
                a = jnp.where(meet[li], als[h][li], a)
            vb = v[:, hs].astype(BF16)
            st = st_ref[h]
            o = (jnp.dot(a.astype(BF16), vb, preferred_element_type=F32)
                 + lax.dot_general(q_in[:, hs].astype(BF16), st.astype(BF16), NT_DIMS, preferred_element_type=F32))
            st_ref[h] = d_all[:, hs] * st + lax.dot_general(vb, k_out[:, hs].astype(BF16), TN_DIMS,
                                                            preferred_element_type=F32)
            ob_ref[0, rows, hs] = (_rms(o, gn) * og[:, hs]).astype(BF16)
        return carry

    lax.fori_loop(0, ct // c, chunk, 0, unroll=True)

    @pl.when(tstep == pl.num_programs(1) - 1)
    def _():
        for h in range(N_HEADS):
            sout_ref[0, h] = st_ref[h].T


def _hgrn_prompt(qh, kh, gh, vh, og, gn, ct=512):
    b, t, _ = qh.shape
    dall = jnp.asarray(np.tril(np.ones((HGRN_CHUNK, HGRN_CHUNK), np.float32)), dtype=BF16)
    tok_spec = pl.BlockSpec((1, ct, D_MODEL), lambda i, j: (i, j, 0))
    return pl.pallas_call(
        functools.partial(_hgrn_prompt_kernel, ct=ct),
        out_shape=(jax.ShapeDtypeStruct((b, t, D_MODEL), BF16),
                   jax.ShapeDtypeStruct((b, N_HEADS, HEAD_W, HEAD_W), F32)),
        grid=(b, t // ct),
        in_specs=[_const_spec(dall.shape), tok_spec, tok_spec, tok_spec, tok_spec, tok_spec, _const_spec((1, HEAD_W))],
        out_specs=(tok_spec, pl.BlockSpec((1, N_HEADS, HEAD_W, HEAD_W), lambda i, j: (i, 0, 0, 0))),
        scratch_shapes=[pltpu.VMEM((N_HEADS, HEAD_W, HEAD_W), F32)],
        compiler_params=_params(("parallel", "arbitrary")),
        name="hgrn_prompt",
    )(dall, qh, kh, gh, vh, og, gn)


def _lambda(lam_ref, lam_init):
    lv = lam_ref[...]
    s1 = jnp.sum(lv[0:1] * lv[1:2], axis=-1, keepdims=True)
    s2 = jnp.sum(lv[2:3] * lv[3:4], axis=-1, keepdims=True)
    return jnp.exp(s1) - jnp.exp(s2) + lam_init


def _attn_prompt_step(lam_ref, g_ref, q_ref, k_ref, v_ref, o_ref, m_ref, acc_ref, *, tq, tk, lam_init,
                      before_diagonal=None):
    qi = pl.program_id(2)
    q = q_ref[0, 0]
    lane = lax.broadcasted_iota(jnp.int32, (1, HEAD_W), 1)
    zero = jnp.zeros_like(q)
    qs = jnp.concatenate([jnp.where(lane < A_DH, q, zero), jnp.where(lane < A_DH, zero, q)], axis=0)
    rg = ATTN_ROW_GROUP
    ng = 2 * tq // rg

    def block(k0, masked):
        geo = []
        for g in range(ng):
            r0 = (g * rg) % tq
            width = min(tk, r0 + rg) if masked else tk
            geo.append((r0, slice(g * rg, (g + 1) * rg), width, pl.ds(pl.multiple_of(k0, tk), width)))

        def scores(g):
            _, gr, _, rows = geo[g]
            return lax.dot_general(qs[gr], k_ref[0, 0, rows, :], NT_DIMS, preferred_element_type=F32)

        ss = [scores(g) for g in range(ATTN_AHEAD)]
        for g, (r0, gr, width, rows) in enumerate(geo):
            s = ss[g]
            if g + ATTN_AHEAD < ng:
                ss.append(scores(g + ATTN_AHEAD))
            if masked:
                qpos = r0 + lax.broadcasted_iota(jnp.int32, (rg, width), 0)
                kpos = lax.broadcasted_iota(jnp.int32, (rg, width), 1)
                s = jnp.where(kpos <= qpos, s, NEG_BIG)
            m_old = m_ref[gr, :]
            m_new = jnp.maximum(m_old, jnp.max(s, axis=-1, keepdims=True))
            alpha = jnp.exp2(m_old - m_new)
            p = jnp.exp2(s - jnp.tile(m_new, (1, width // HEAD_W)))
            pv = jnp.dot(p.astype(BF16), v_ref[0, 0, rows, :], preferred_element_type=F32)
            m_ref[gr, :] = m_new
            acc_ref[gr, :] = jnp.tile(alpha, (1, 2)) * acc_ref[gr, :] + pv

    m_ref[...] = jnp.full(m_ref.shape, NEG_BIG, F32)
    acc_ref[...] = jnp.zeros(acc_ref.shape, F32)

    def two_blocks(i, carry):
        block(2 * i * tk, False)
        block((2 * i + 1) * tk, False)
        return carry

    lax.fori_loop(0, qi // 2, two_blocks, 0)

    @pl.when(qi % 2 == 1)
    def _():
        block((qi - 1) * tk, False)

    if before_diagonal is not None:
        before_diagonal()
    block(qi * tq, True)
    acc = acc_ref[...]
    o = acc[:, 0:HEAD_W] / acc[:, HEAD_W:2 * HEAD_W]
    lam = _lambda(lam_ref, lam_init)
    od = o[0:tq] - lam * o[tq:2 * tq]
    o_ref[0, 0] = _rms(od, g_ref[...]).astype(BF16)


def _mix_kernel(x_ref, gm_ref, oa_ref, ob_ref, ga_ref, gb_ref, wa_ref, wb_ref, wo_ref, o_ref):
    oa = jnp.concatenate([oa_ref[0, h] for h in range(N_HEADS)], axis=-1)
    ya = jnp.dot(oa, wa_ref[...], preferred_element_type=F32)
    yb = jnp.dot(ob_ref[0].astype(BF16), wb_ref[...], preferred_element_type=F32)
    merged = ga_ref[0] * ya + gb_ref[0] * yb
    mix = jnp.dot(merged.astype(BF16), wo_ref[...], preferred_element_type=F32)
    o_ref[0] = x_ref[0] + gm_ref[0] * mix


def _mix(x, gm, oa_hm, ob, ga, gb, wa, wb, wo, tm):
    b, t, _ = x.shape
    rows_mod = gm.shape[1]
    tok_spec = pl.BlockSpec((1, tm, D_MODEL), lambda i, j: (i, j, 0))
    mod_spec = (pl.BlockSpec((1, 1, D_MODEL), lambda i, j: (i, 0, 0)) if rows_mod == 1 else tok_spec)
    return pl.pallas_call(
        _mix_kernel,
        out_shape=jax.ShapeDtypeStruct((b, t, D_MODEL), F32),
        grid=(b, t // tm),
        in_specs=[tok_spec, mod_spec, pl.BlockSpec((1, N_HEADS, tm, HEAD_W), lambda i, j: (i, 0, j, 0)),
                  tok_spec, tok_spec, tok_spec,
                  _const_spec(wa.shape), _const_spec(wb.shape), _const_spec(wo.shape)],
        out_specs=tok_spec,
        compiler_params=_params(("parallel", "parallel")),
        name="mix_out",
    )(x, gm, oa_hm, ob, ga, gb, wa, wb, wo)


def _ffn_kernel(*refs, tm, carried):
    if carried:
        (x_ref, sh_ref, sc_ref, gt_ref, g_ref, wup_ref, cw_ref, cb_ref, wdn_ref, y_ref, tail_ref, prev_ref) = refs
    else:
        (x_ref, sh_ref, sc_ref, gt_ref, g_ref, wup_ref, cw_ref, cb_ref, wdn_ref, b0_ref, b1_ref, y_ref, u_ref) = refs
    x = x_ref[0]
    hn = (_rms(x, g_ref[...]) * (1.0 + sc_ref[0]) + sh_ref[0]).astype(BF16)
    if carried:
        @pl.when(pl.program_id(1) == 0)
        def _():
            prev_ref[...] = jnp.zeros_like(prev_ref)
        row = lax.broadcasted_iota(jnp.int32, (tm, 1), 0)

    def conv(u, cols):
        if carried:
            p6 = prev_ref[6:7, cols]
            p7 = prev_ref[7:8, cols]
            u1 = jnp.where(row == 0, p7, pltpu.roll(u, 1, 0))
            u2 = jnp.where(row == 0, p6, jnp.where(row == 1, p7, pltpu.roll(u, 2, 0)))
            prev_ref[:, cols] = u[tm - 8:tm, :]
            tail_ref[0, :, cols] = u[tm - 8:tm, :]
        else:
            u2 = b0_ref[0, :, cols]
            u1 = b1_ref[0, :, cols]
            u_ref[0, :, cols] = u
        return cb_ref[:, cols] + cw_ref[0:1, cols] * u2 + cw_ref[1:2, cols] * u1 + cw_ref[2:3, cols] * u

    nchunk = D_FF // FFN_COLS

    def cols(c):
        return slice(c * FFN_COLS, (c + 1) * FFN_COLS), slice(D_FF + c * FFN_COLS, D_FF + (c + 1) * FFN_COLS)

    def up(c):
        ca, cg = cols(c)
        return (jnp.dot(hn, wup_ref[:, ca], preferred_element_type=F32),
                jnp.dot(hn, wup_ref[:, cg], preferred_element_type=F32))

    acc = jnp.zeros((tm, D_MODEL), F32)
    ups = [up(c) for c in range(FFN_AHEAD)]
    acts = []
    for c in range(nchunk):
        ua, ug = ups[c]
        if c + FFN_AHEAD < nchunk:
            ups.append(up(c + FFN_AHEAD))
        ca, cg = cols(c)
        acts.append((jax.nn.silu(conv(ua, ca)) * conv(ug, cg)).astype(BF16))
        if c >= FFN_BEHIND:
            acc = acc + jnp.dot(acts[c - FFN_BEHIND], wdn_ref[cols(c - FFN_BEHIND)[0], :],
                                preferred_element_type=F32)
    for c in range(nchunk - FFN_BEHIND, nchunk):
        acc = acc + jnp.dot(acts[c], wdn_ref[cols(c)[0], :], preferred_element_type=F32)
    y_ref[0] = x + gt_ref[0] * acc


def _ffn(x1, sh, sc, gt, g, wup, cw, cb, wdn, tm, bufs=None):
    b, t, _ = x1.shape
    carried = bufs is None
    tok_spec = pl.BlockSpec((1, tm, D_MODEL), lambda i, j: (i, j, 0))
    mod_spec = (pl.BlockSpec((1, 1, D_MODEL), lambda i, j: (i, 0, 0)) if sh.shape[1] == 1 else tok_spec)
    wide_spec = pl.BlockSpec((1, tm, 2 * D_FF), lambda i, j: (i, j, 0))
    in_specs = [tok_spec, mod_spec, mod_spec, mod_spec, _const_spec((1, D_MODEL)), _const_spec(wup.shape),
                _const_spec(cw.shape), _const_spec(cb.shape), _const_spec(wdn.shape)]
    args = [x1, sh, sc, gt, g, wup, cw, cb, wdn]
    if carried:
        out_shape = (jax.ShapeDtypeStruct((b, t, D_MODEL), F32), jax.ShapeDtypeStruct((b, 8, 2 * D_FF), F32))
        out_specs = (tok_spec, pl.BlockSpec((1, 8, 2 * D_FF), lambda i, j: (i, 0, 0)))
        scratch = [pltpu.VMEM((8, 2 * D_FF), F32)]
        sem = ("parallel", "arbitrary")
    else:
        in_specs += [wide_spec, wide_spec]
        args += list(bufs)
        out_shape = (jax.ShapeDtypeStruct((b, t, D_MODEL), F32), jax.ShapeDtypeStruct((b, t, 2 * D_FF), F32))
        out_specs = (tok_spec, wide_spec)
        scratch = []
        sem = ("parallel", "parallel")
    return pl.pallas_call(
        functools.partial(_ffn_kernel, tm=tm, carried=carried),
        out_shape=out_shape,
        grid=(b, t // tm),
        in_specs=in_specs,
        out_specs=out_specs,
        scratch_shapes=scratch,
        compiler_params=_params(sem),
        name="conv_ffn",
    )(*args)


def _hgrn_sample_kernel(ft_ref, q_ref, v_ref, og_ref, gn_ref, s_ref, so_ref, o_ref, *, nb):
    ft = ft_ref[...]
    first_row = lax.broadcasted_iota(jnp.int32, (2 * N_HEADS, HEAD_W), 0) == 0
    for b in range(nb):
        f = ft[:, b:b + 1]
        s_new = f * s_ref[b, 0] + (1.0 - f) * v_ref[b:b + 1, :]
        so_ref[b, 0] = s_new
        lhs = jnp.where(first_row, q_ref[b:b + 1, :], 0.0).astype(BF16)
        o_ref[b:b + 1, :] = jnp.dot(lhs, s_new.astype(BF16), preferred_element_type=F32)[0:1, :]
    o_ref[...] = _rms(o_ref[...], gn_ref[...]) * og_ref[...]


def _hgrn_sample(ft, qh, vh, og, gn, state):
    nb = vh.shape[0]
    st_spec = pl.BlockSpec((nb, 1, HEAD_W, HEAD_W), lambda h: (0, h, 0, 0))
    hd_spec = pl.BlockSpec((nb, HEAD_W), lambda h: (0, h))
    return pl.pallas_call(
        functools.partial(_hgrn_sample_kernel, nb=nb),
        out_shape=(jax.ShapeDtypeStruct(state.shape, F32), jax.ShapeDtypeStruct((nb, D_MODEL), F32)),
        grid=(N_HEADS,),
        in_specs=[pl.BlockSpec((HEAD_W, nb), lambda h: (h, 0)), hd_spec, hd_spec, hd_spec, _const_spec((1, HEAD_W)),
                  st_spec],
        out_specs=(st_spec, hd_spec),
        compiler_params=_params(("parallel",)),
        name="hgrn_sample",
    )(ft, qh, vh, og, gn, state)


def _page_copies(pt_ref, ck_hbm, cv_hbm, kbuf, vbuf, sem, b, n_pages):
    sl = b % 2
    out = []
    for p in range(n_pages):
        pg = pt_ref[b, p]
        out.append(pltpu.make_async_copy(ck_hbm.at[pg], kbuf.at[sl, p], sem.at[0, sl]))
        out.append(pltpu.make_async_copy(cv_hbm.at[pg], vbuf.at[sl, p], sem.at[1, sl]))
    return out


def _attn_sample_scores(b, q_ref, kbuf, *, n_pages):
    slot = b % 2
    q = q_ref[pl.ds(b, 1), :]
    r16 = lax.broadcasted_iota(jnp.int32, (2 * N_HEADS, D_MODEL), 0)
    ln = lax.broadcasted_iota(jnp.int32, (2 * N_HEADS, D_MODEL), 1)
    sel = ((ln // HEAD_W) == (r16 % N_HEADS)) & (((ln // A_DH) % 2) == (r16 // N_HEADS))
    qsel = jnp.where(sel, q, 0.0)
    qblk = qsel.astype(BF16)
    s = jnp.concatenate([jnp.dot(qblk, kbuf[slot, p].astype(BF16), preferred_element_type=F32)
                         for p in range(n_pages)], axis=1)
    return qsel, s


def _attn_sample_one(b, qsel, s, lam_ref, g_ref, kn_ref, vn_ref, o_ref, vbuf, *, n_pages, page, lam_init):
    slot = b % 2
    s_new = jnp.sum(qsel * kn_ref[pl.ds(b, 1), :], axis=-1, keepdims=True)
    m = jnp.maximum(jnp.max(s, axis=-1, keepdims=True), s_new)
    e = jnp.exp2(s - m)
    e_new = jnp.exp2(s_new - m)
    inv_l = 1.0 / (jnp.sum(e, axis=-1, keepdims=True) + e_new)
    lam = _lambda(lam_ref, lam_init)
    pr = e * inv_l
    pr_new = e_new * inv_l
    pd = pr[0:N_HEADS] - lam * pr[N_HEADS:2 * N_HEADS]
    pd_new = pr_new[0:N_HEADS] - lam * pr_new[N_HEADS:2 * N_HEADS]

    per = HEAD_W // N_HEADS
    lane = lax.broadcasted_iota(jnp.int32, (N_HEADS, HEAD_W), 1)
    own = (lane % N_HEADS) == lax.broadcasted_iota(jnp.int32, (N_HEADS, HEAD_W), 0)
    pad = jnp.zeros((N_HEADS, HEAD_W), F32)

    acc = jnp.zeros((2 * N_HEADS, HEAD_W), F32)
    for p in range(n_pages):
        w = pd[:, p * page:(p + 1) * page]
        for g in range(page // per):
            wx = jnp.take_along_axis(w, g * per + lane // N_HEADS, axis=1)
            lhs = jnp.concatenate([jnp.where(own, wx, 0.0), pad], axis=0).astype(BF16)
            rhs = vbuf[slot, p, pl.ds(g * per, per)].reshape(HEAD_W, HEAD_W).astype(BF16)
            acc = acc + jnp.dot(lhs, rhs, preferred_element_type=F32)
    o_ref[b] = _rms(acc[0:N_HEADS] + pd_new * vn_ref[b], g_ref[...])


def _attn_kernel(pt_ref, lam_ref, g_ref, q_ref, k_ref, v_ref, qs_ref, kn_ref, vn_ref, ck_hbm, cv_hbm,
                 o_ref, os_ref, kbuf, vbuf, m_ref, acc_ref, sem, *, tq, lam_init, n_pages, page, per_step):
    step = (pl.program_id(0) * pl.num_programs(1) + pl.program_id(1)) * pl.num_programs(2) + pl.program_id(2)
    n_steps = pl.num_programs(0) * pl.num_programs(1) * pl.num_programs(2)
    first = step * per_step
    copies = functools.partial(_page_copies, pt_ref, ck_hbm, cv_hbm, kbuf, vbuf, sem, n_pages=n_pages)

    @pl.when(step == 0)
    def _():
        for cp in copies(0):
            cp.start()

    if per_step == 1:
        @pl.when(step + 1 < n_steps)
        def _():
            for cp in copies(step + 1):
                cp.start()

    early = []

    def wait_first():
        for cp in copies(first):
            cp.wait()
        early.append(_attn_sample_scores(first, qs_ref, kbuf, n_pages=n_pages))

    _attn_prompt_step(lam_ref, g_ref, q_ref, k_ref, v_ref, o_ref, m_ref, acc_ref, tq=tq, tk=tq, lam_init=lam_init,
                      before_diagonal=wait_first)

    for j in range(per_step):
        b = first + j
        if per_step > 1:
            @pl.when(b + 1 < n_steps * per_step)
            def _():
                for cp in copies(b + 1):
                    cp.start()
        if j > 0:
            for cp in copies(b):
                cp.wait()
        qsel, s = early[0] if j == 0 else _attn_sample_scores(b, qs_ref, kbuf, n_pages=n_pages)
        _attn_sample_one(b, qsel, s, lam_ref, g_ref, kn_ref, vn_ref, os_ref, vbuf,
                         n_pages=n_pages, page=page, lam_init=lam_init)


def _attention(page_table, lam4, g_sub, q_hm, k_hm, v_hm, q_s, k_new, v_new, cache_kt, cache_v, lam_init, tq=1024):
    b, nh, t, _ = q_hm.shape
    nb, n_pages = page_table.shape
    page = cache_v.shape[1]
    n_steps = b * nh * (t // tq)
    assert nb % n_steps == 0, (nb, n_steps)
    full = lambda shape: pl.BlockSpec(shape, lambda i, h, j, pt: (0,) * len(shape))
    any_spec = pl.BlockSpec(memory_space=pl.ANY)
    k_spec = pl.BlockSpec((1, 1, t, HEAD_W), lambda i, h, j, pt: (i, h, 0, 0))
    v_spec = pl.BlockSpec((1, 1, t, 2 * HEAD_W), lambda i, h, j, pt: (i, h, 0, 0))
    q_spec = pl.BlockSpec((1, 1, tq, HEAD_W), lambda i, h, j, pt: (i, h, j, 0))
    grid_spec = pltpu.PrefetchScalarGridSpec(
        num_scalar_prefetch=1,
        grid=(b, nh, t // tq),
        in_specs=[full(lam4.shape), full((1, HEAD_W)), q_spec, k_spec, v_spec,
                  full(q_s.shape), full(k_new.shape), full(v_new.shape), any_spec, any_spec],
        out_specs=(q_spec, full((nb, N_HEADS, HEAD_W))),
        scratch_shapes=[pltpu.VMEM((2, n_pages, D_MODEL, page), F32),
                        pltpu.VMEM((2, n_pages, page, N_HEADS, HEAD_W), F32),
                        pltpu.VMEM((2 * tq, HEAD_W), F32), pltpu.VMEM((2 * tq, 2 * HEAD_W), F32),
                        pltpu.SemaphoreType.DMA((2, 2))],
    )
    return pl.pallas_call(
        functools.partial(_attn_kernel, tq=tq, lam_init=lam_init, n_pages=n_pages, page=page,
                          per_step=nb // n_steps),
        out_shape=(jax.ShapeDtypeStruct((b, nh, t, HEAD_W), BF16), jax.ShapeDtypeStruct((nb, N_HEADS, HEAD_W), F32)),
        grid_spec=grid_spec,
        compiler_params=_params(("arbitrary", "arbitrary", "arbitrary")),
        name="attention",
    )(page_table, lam4, g_sub, q_hm, k_hm, v_hm, q_s, k_new, v_new, cache_kt, cache_v)


def _rope_tables(pos):
    half = A_DH // 2
    inv = ROPE_THETA ** (-np.arange(half, dtype=np.float64) / half)
    ang = np.asarray(pos, np.float64)[:, None] * inv[None, :]
    cos = np.cos(ang)
    sin = np.sin(ang)
    return (jnp.asarray(np.tile(cos, (1, 4)), F32),
            jnp.asarray(np.concatenate([-sin, sin, -sin, sin], axis=-1), F32))


def kernel(x_prompt, x_sample, cache_k, cache_v, state_hgrn, state_conv, page_table, c_prompt, c_sample, w_ada, b_ada,
           rms_mix_g, w_in, q_norm_g, k_norm_g, lambda_q1, lambda_k1, lambda_q2, lambda_k2, subln_g, lb_logits,
           hgrn_norm_g, w_branch_a, w_branch_b, w_out, rms_ffn_g, w_up, conv_w, conv_b, w_down):
    bp, tp, d = x_prompt.shape
    bs, ts, _ = x_sample.shape
    assert d == D_MODEL and ts == 1 and w_in.shape[0] == 1
    n_pool, page = cache_k.shape[1], cache_k.shape[2]
    past_len = page_table.shape[1] * page
    lam_init = 0.8 - 0.6 * math.exp(-0.3 * 0)

    pad = (-(bp + bs)) % 8
    c_all = jnp.concatenate([c_prompt, c_sample, jnp.zeros((pad, d), F32)], axis=0)
    mod = _ada(c_all, w_ada[0], b_ada[0].reshape(1, -1))
    mp = [mod[:bp, i * d:(i + 1) * d].reshape(bp, 1, d) for i in range(6)]
    msamp = [mod[bp:bp + bs, i * d:(i + 1) * d] for i in range(6)]

    w_in_b = w_in[0].astype(BF16)
    wa_b = w_branch_a[0].astype(BF16)
    wb_b = w_branch_b[0].astype(BF16)
    wo_b = w_out[0].astype(BF16)
    wup_b = w_up[0].astype(BF16)
    wdn_b = w_down[0].astype(BF16)
    gmix = rms_mix_g[0].reshape(1, d)
    gffn = rms_ffn_g[0].reshape(1, d)
    qg = jnp.tile(q_norm_g[0], 2).reshape(1, HEAD_W)
    kg = jnp.tile(k_norm_g[0], 2).reshape(1, HEAD_W)
    lam4 = jnp.stack([lambda_q1[0], lambda_k1[0], lambda_q2[0], lambda_k2[0]], axis=0)
    g_sub = (subln_g[0] * (1.0 - lam_init)).reshape(1, HEAD_W)
    gn = hgrn_norm_g[0].reshape(1, HEAD_W)
    cw = conv_w[0]
    cb = conv_b[0].reshape(1, -1)

    cos_p, sin_p = _rope_tables(np.arange(tp))
    (q_hm, k_tok, k_hm, v_tok, v_hm, qh, kh, gh, vh, og, ga, gb) = _inproj_prompt(
        x_prompt, mp[0], mp[1], gmix, w_in_b, qg, kg, cos_p, sin_p, lb_logits)
    ob, s_prompt = _hgrn_prompt(qh, kh, gh, vh, og, gn)
    cos_s, sin_s = _rope_tables(np.full((1,), past_len))
    xs = x_sample.reshape(bs, d)
    (q_s, k_s, v_s, qh_s, ft_s, vh_s, og_s, ga_s, gb_s) = _inproj_sample(
        xs, msamp[0], msamp[1], gmix, w_in_b, qg, kg, cos_s, sin_s, lb_logits)
    s_sample, ob_s = _hgrn_sample(ft_s, qh_s, vh_s, og_s, gn, state_hgrn[0])

    cache_kt = jnp.transpose(cache_k[0], (0, 2, 3, 4, 1)).reshape(n_pool, d, page)
    oa_hm, oa_s = _attention(page_table, lam4, g_sub, q_hm, k_hm, v_hm, q_s, k_s, v_s.reshape(bs, N_HEADS, HEAD_W),
                             cache_kt, cache_v[0], lam_init)

    x1 = _mix(x_prompt, mp[2], oa_hm, ob, ga, gb, wa_b, wb_b, wo_b, tm=512)
    y_prompt, tail = _ffn(x1, mp[3], mp[4], mp[5], gffn, wup_b, cw, cb, wdn_b, tm=512)
    oa_s = jnp.transpose(oa_s, (1, 0, 2))
    x1_s = _mix(xs[None], msamp[2][None], oa_s[None].astype(BF16), ob_s[None], ga_s[None], gb_s[None],
                wa_b, wb_b, wo_b, tm=bs)
    y_s, u_s = _ffn(x1_s, msamp[3][None], msamp[4][None], msamp[5][None], gffn, wup_b, cw, cb, wdn_b, tm=bs,
                    bufs=(state_conv[0][:, 0][None], state_conv[0][:, 1][None]))

    new_conv_s = jnp.stack([state_conv[0][:, 1], u_s[0]], axis=1)
    return (y_prompt, y_s.reshape(bs, 1, d),
            k_tok.reshape(1, bp, tp, N_HEADS, 2, A_DH), v_tok.reshape(1, bp, tp, N_HEADS, HEAD_W),
            s_prompt[None], tail[:, 6:8][None],
            k_s.reshape(1, bs, 1, N_HEADS, 2, A_DH), v_s.reshape(1, bs, 1, N_HEADS, HEAD_W),
            s_sample[None], new_conv_s[None])
```

```python
import functools
import math

import numpy as np
import jax
import jax.numpy as jnp
from jax import lax
from jax.experimental import pallas as pl
from jax.experimental.pallas import tpu as pltpu

F32 = jnp.float32
BF16 = jnp.bfloat16

D_MODEL = 1024
N_HEADS = 8
HEAD_W = 128
A_DH = 64
D_FF = 2816
ROPE_THETA = 10000.0
EPS = 1e-6
NEG_BIG = -1e30
HGRN_CHUNK = 64
HGRN_LEVELS = (32, 16, 8, 4, 2, 1)
FFN_COLS = 256
FFN_AHEAD = 6
FFN_BEHIND = 3
VMEM_LIMIT = 56 * 1024 * 1024
Q_SCALE = (A_DH ** -0.5) * math.log2(math.e)

ATTN_ROW_GROUP = 256
ATTN_AHEAD = 2
NT_DIMS = (((1,), (1,)), ((), ()))
TN_DIMS = (((0,), (0,)), ((), ()))


def _const_spec(shape):
    nd = len(shape)
    return pl.BlockSpec(shape, lambda *_: (0,) * nd, pipeline_mode=pl.Buffered(1))


def _params(sem, vmem=VMEM_LIMIT):
    return pltpu.CompilerParams(dimension_semantics=sem, vmem_limit_bytes=vmem)


def _rms(x, g):
    ms = jnp.mean(x * x, axis=-1, keepdims=True)
    return (x * lax.rsqrt(ms + EPS)) * g


def _ada_kernel(c_ref, w_ref, b_ref, o_ref):
    a = jax.nn.silu(c_ref[...]).astype(BF16)
    o_ref[...] = jnp.dot(a, w_ref[...].astype(BF16), preferred_element_type=F32) + b_ref[...]


def _ada(c_all, w_ada, b_ada):
    m = c_all.shape[0]
    n = w_ada.shape[1]
    tn = 1536
    return pl.pallas_call(
        _ada_kernel,
        out_shape=jax.ShapeDtypeStruct((m, n), F32),
        grid=(n // tn,),
        in_specs=[
            _const_spec((m, D_MODEL)),
            pl.BlockSpec((D_MODEL, tn), lambda j: (0, j)),
            pl.BlockSpec((1, tn), lambda j: (0, j)),
        ],
        out_specs=pl.BlockSpec((m, tn), lambda j: (0, j)),
        compiler_params=_params(("arbitrary",)),
        name="ada_mod",
    )(c_all, w_ada, b_ada)


def _qk_norm_rope(z, g128, cos, sin, lane):
    sq = z * z
    first = lane < A_DH
    lo = jnp.sum(jnp.where(first, sq, 0.0), axis=-1, keepdims=True)
    al = jnp.sum(sq, axis=-1, keepdims=True)
    ms = jnp.where(first, lo, al - lo) * (1.0 / A_DH)
    zn = (z * lax.rsqrt(ms + EPS)) * g128
    rot = jnp.where((lane % A_DH) < A_DH // 2, pltpu.roll(zn, HEAD_W - A_DH // 2, 1), pltpu.roll(zn, A_DH // 2, 1))
    return zn * cos + rot * sin


def _lower_bound(lbl_ref):
    lbl = lbl_ref[...]
    mx = jnp.max(lbl, axis=0, keepdims=True)
    e = jnp.exp(lbl - mx)
    return e[0:1, :] / jnp.sum(e, axis=0, keepdims=True)


def _inproj_common(x, sh, sc, gmix_ref, w_ref):
    xn = _rms(x, gmix_ref[...]) * (1.0 + sc) + sh
    xb = xn.astype(BF16)

    def seg(i):
        return jnp.dot(xb, w_ref[:, i * D_MODEL:(i + 1) * D_MODEL], preferred_element_type=F32)

    return seg


def _inproj_prompt_kernel(x_ref, sh_ref, sc_ref, gmix_ref, w_ref, qg_ref, kg_ref, cos_ref, sin_ref, lbl_ref,
                          qhm_ref, ktok_ref, khm_ref, vtok_ref, vhm_ref,
                          qh_ref, kh_ref, gh_ref, vh_ref, og_ref, ga_ref, gb_ref):
    seg = _inproj_common(x_ref[0], sh_ref[0], sc_ref[0], gmix_ref, w_ref)
    cos = cos_ref[...]
    sin = sin_ref[...]
    lane = lax.broadcasted_iota(jnp.int32, (1, HEAD_W), 1)
    zq = seg(0)
    zk = seg(1)
    for h in range(N_HEADS):
        hs = slice(h * HEAD_W, (h + 1) * HEAD_W)
        q = _qk_norm_rope(zq[:, hs], qg_ref[...], cos, sin, lane) * Q_SCALE
        qhm_ref[0, h] = q.astype(BF16)
        k = _qk_norm_rope(zk[:, hs], kg_ref[...], cos, sin, lane)
        ktok_ref[0, :, hs] = k
        khm_ref[0, h] = k.astype(BF16)
    zv = seg(2)
    vtok_ref[0] = zv
    ones = jnp.ones((zv.shape[0], HEAD_W), BF16)
    for h in range(N_HEADS):
        vhm_ref[0, h, :, 0:HEAD_W] = zv[:, h * HEAD_W:(h + 1) * HEAD_W].astype(BF16)
        vhm_ref[0, h, :, HEAD_W:2 * HEAD_W] = ones
    qh_ref[0] = jax.nn.silu(seg(3))
    lb = _lower_bound(lbl_ref)
    f = lb + (1.0 - lb) * jax.nn.sigmoid(seg(4))
    kh_ref[0] = 1.0 - f
    gh_ref[0] = jnp.log(f)
    vh_ref[0] = seg(5)
    og_ref[0] = jax.nn.silu(seg(6))
    ga_ref[0] = jax.nn.sigmoid(seg(7))
    gb_ref[0] = jax.nn.sigmoid(seg(8))


def _inproj_prompt(x, sh, sc, gmix, w_in, qg, kg, cos, sin, lbl, tm=256):
    b, t, _ = x.shape
    tok = jax.ShapeDtypeStruct((b, t, D_MODEL), F32)
    hm = jax.ShapeDtypeStruct((b, N_HEADS, t, HEAD_W), BF16)
    hm2 = jax.ShapeDtypeStruct((b, N_HEADS, t, 2 * HEAD_W), BF16)
    tok_spec = pl.BlockSpec((1, tm, D_MODEL), lambda i, j: (i, j, 0))
    hm_spec = pl.BlockSpec((1, N_HEADS, tm, HEAD_W), lambda i, j: (i, 0, j, 0))
    hm2_spec = pl.BlockSpec((1, N_HEADS, tm, 2 * HEAD_W), lambda i, j: (i, 0, j, 0))
    mod_spec = pl.BlockSpec((1, 1, D_MODEL), lambda i, j: (i, 0, 0))
    tab_spec = pl.BlockSpec((tm, HEAD_W), lambda i, j: (j, 0))
    return pl.pallas_call(
        _inproj_prompt_kernel,
        out_shape=(hm, tok, hm, tok, hm2, tok, tok, tok, tok, tok, tok, tok),
        grid=(b, t // tm),
        in_specs=[tok_spec, mod_spec, mod_spec, _const_spec((1, D_MODEL)), _const_spec(w_in.shape),
                  _const_spec((1, HEAD_W)), _const_spec((1, HEAD_W)), tab_spec, tab_spec, _const_spec(lbl.shape)],
        out_specs=(hm_spec, tok_spec, hm_spec, tok_spec, hm2_spec,
                   tok_spec, tok_spec, tok_spec, tok_spec, tok_spec, tok_spec, tok_spec),
        compiler_params=_params(("parallel", "parallel")),
        name="inproj_prompt",
    )(x, sh, sc, gmix, w_in, qg, kg, cos, sin, lbl)


def _inproj_sample_kernel(x_ref, sh_ref, sc_ref, gmix_ref, w_ref, qg_ref, kg_ref, cos_ref, sin_ref, lbl_ref,
                          q_ref, k_ref, v_ref, qh_ref, ft_ref, vh_ref, og_ref, ga_ref, gb_ref):
    seg = _inproj_common(x_ref[...], sh_ref[...], sc_ref[...], gmix_ref, w_ref)
    cos = cos_ref[...]
    sin = sin_ref[...]
    lane = lax.broadcasted_iota(jnp.int32, (1, HEAD_W), 1)
    zq = seg(0)
    zk = seg(1)
    for h in range(N_HEADS):
        hs = slice(h * HEAD_W, (h + 1) * HEAD_W)
        q_ref[:, hs] = _qk_norm_rope(zq[:, hs], qg_ref[...], cos, sin, lane) * Q_SCALE
        k_ref[:, hs] = _qk_norm_rope(zk[:, hs], kg_ref[...], cos, sin, lane)
    v_ref[...] = seg(2)
    qh_ref[...] = jax.nn.silu(seg(3))
    lb = _lower_bound(lbl_ref)
    f = lb + (1.0 - lb) * jax.nn.sigmoid(seg(4))
    for h in range(N_HEADS):
        hs = slice(h * HEAD_W, (h + 1) * HEAD_W)
        ft_ref[hs, :] = f[:, hs].T
    vh_ref[...] = seg(5)
    og_ref[...] = jax.nn.silu(seg(6))
    ga_ref[...] = jax.nn.sigmoid(seg(7))
    gb_ref[...] = jax.nn.sigmoid(seg(8))


def _inproj_sample(x, sh, sc, gmix, w_in, qg, kg, cos, sin, lbl):
    m = x.shape[0]
    tok = jax.ShapeDtypeStruct((m, D_MODEL), F32)
    chan = jax.ShapeDtypeStruct((D_MODEL, m), F32)
    ts = _const_spec((m, D_MODEL))
    cs = _const_spec((D_MODEL, m))
    return pl.pallas_call(
        _inproj_sample_kernel,
        out_shape=(tok, tok, tok, tok, chan, tok, tok, tok, tok),
        grid=(1,),
        in_specs=[ts, ts, ts, _const_spec((1, D_MODEL)), _const_spec(w_in.shape),
                  _const_spec((1, HEAD_W)), _const_spec((1, HEAD_W)), _const_spec((1, HEAD_W)),
                  _const_spec((1, HEAD_W)), _const_spec(lbl.shape)],
        out_specs=(ts, ts, ts, ts, cs, ts, ts, ts, ts),
        compiler_params=_params(("arbitrary",)),
        name="inproj_sample",
    )(x, sh, sc, gmix, w_in, qg, kg, cos, sin, lbl)


def _hgrn_boundary_rows(cum, h):
    c, d = cum.shape
    if 2 * h >= 8:
        grp = 2 * h
        return jnp.concatenate(
            [jnp.broadcast_to(cum[i * grp + h - 1:i * grp + h, :], (grp, d)) for i in range(c // grp)], axis=0)
    cum3 = cum.reshape(c // 8, 8, d)
    sub = lax.broadcasted_iota(jnp.int32, (c // 8, 8, d), 1)
    out = None
    for first in range(0, 8, 2 * h):
        row = jnp.broadcast_to(cum3[:, first + h - 1:first + h, :], cum3.shape)
        out = row if out is None else jnp.where(sub >= first, row, out)
    return out.reshape(c, d)


def _hgrn_prompt_kernel(ltri_ref, q_ref, k_ref, g_ref, v_ref, og_ref, gn_ref, ob_ref, sout_ref, st_ref, *, ct):
    c = HGRN_CHUNK
    tstep = pl.program_id(1)

    @pl.when(tstep == 0)
    def _():
        st_ref[...] = jnp.zeros_like(st_ref)

    tt = lax.broadcasted_iota(jnp.int32, (c, c), 0)
    ss = lax.broadcasted_iota(jnp.int32, (c, c), 1)
    meet = [((tt // (2 * h)) == (ss // (2 * h))) & (((tt // h) % 2) == 1) & (((ss // h) % 2) == 0)
            for h in HGRN_LEVELS]
    eye = tt == ss
    ti = lax.broadcasted_iota(jnp.int32, (c, 1), 0)
    side = [jnp.where(((ti // h) % 2) == 1, 1.0, -1.0) for h in HGRN_LEVELS]
    ltri = ltri_ref[...]
    gn = gn_ref[...]

    def chunk(ci, carry):
        rows = pl.ds(pl.multiple_of(ci * c, c), c)
        g = g_ref[0, rows, :]
        g_hi = g.astype(BF16)
        r1 = g - g_hi.astype(F32)
        g_mid = r1.astype(BF16)
        g_lo = (r1 - g_mid.astype(F32)).astype(BF16)
        cum = (jnp.dot(ltri, g_hi, preferred_element_type=F32)
               + jnp.dot(ltri, g_mid, preferred_element_type=F32)
               + jnp.dot(ltri, g_lo, preferred_element_type=F32))
        q = q_ref[0, rows, :]
        k = k_ref[0, rows, :]
        v = v_ref[0, rows, :]
        og = og_ref[0, rows, :]
        q_in = q * jnp.exp(cum)
        k_out = k * jnp.exp(cum[c - 1:c, :] - cum)
        d_all = jnp.exp(cum[c - 1:c, :])
        ex = [jnp.exp((cum - _hgrn_boundary_rows(cum, h)) * side[li]).astype(BF16)
              for li, h in enumerate(HGRN_LEVELS)]
        qb = q.astype(BF16)
        kb = k.astype(BF16)
        hsl = [slice(h * HEAD_W, (h + 1) * HEAD_W) for h in range(N_HEADS)]
        als = [[lax.dot_general(qb[:, hs] * ex[li][:, hs], kb[:, hs] * ex[li][:, hs], NT_DIMS,
                                preferred_element_type=F32) for li in range(len(HGRN_LEVELS))] for hs in hsl]
        for h in range(N_HEADS):
            hs = hsl[h]
            a = jnp.where(eye, jnp.sum(q[:, hs] * k[:, hs], axis=-1, keepdims=True), 0.0)
            for li in range(len(HGRN_LEVELS)):
                a = jnp.where(meet[li], als[h][li], a)
            vb = v[:, hs].astype(BF16)
            st = st_ref[h]
            o = (jnp.dot(a.astype(BF16), vb, preferred_element_type=F32)
                 + lax.dot_general(q_in[:, hs].astype(BF16), st.astype(BF16), NT_DIMS, preferred_element_type=F32))
            st_ref[h] = d_all[:, hs] * st + lax.dot_general(vb, k_out[:, hs].astype(BF16), TN_DIMS,
                                                            preferred_element_type=F32)
            ob_ref[0, rows, hs] = (_rms(o, gn) * og[:, hs]).astype(BF16)
        return carry

    lax.fori_loop(0, ct // c, chunk, 0, unroll=True)

    @pl.when(tstep == pl.num_programs(1) - 1)
    def _():
        for h in range(N_HEADS):
            sout_ref[0, h] = st_ref[h].T


def _hgrn_prompt(qh, kh, gh, vh, og, gn, ct=1024):
    b, t, _ = qh.shape
    dall = jnp.asarray(np.tril(np.ones((HGRN_CHUNK, HGRN_CHUNK), np.float32)), dtype=BF16)
    tok_spec = pl.BlockSpec((1, ct, D_MODEL), lambda i, j: (i, j, 0))
    return pl.pallas_call(
        functools.partial(_hgrn_prompt_kernel, ct=ct),
        out_shape=(jax.ShapeDtypeStruct((b, t, D_MODEL), BF16),
                   jax.ShapeDtypeStruct((b, N_HEADS, HEAD_W, HEAD_W), F32)),
        grid=(b, t // ct),
        in_specs=[_const_spec(dall.shape), tok_spec, tok_spec, tok_spec, tok_spec, tok_spec, _const_spec((1, HEAD_W))],
        out_specs=(tok_spec, pl.BlockSpec((1, N_HEADS, HEAD_W, HEAD_W), lambda i, j: (i, 0, 0, 0))),
        scratch_shapes=[pltpu.VMEM((N_HEADS, HEAD_W, HEAD_W), F32)],
        compiler_params=_params(("parallel", "arbitrary")),
        name="hgrn_prompt",
    )(dall, qh, kh, gh, vh, og, gn)


def _lambda(lam_ref, lam_init):
    lv = lam_ref[...]
    s1 = jnp.sum(lv[0:1] * lv[1:2], axis=-1, keepdims=True)
    s2 = jnp.sum(lv[2:3] * lv[3:4], axis=-1, keepdims=True)
    return jnp.exp(s1) - jnp.exp(s2) + lam_init


def _attn_prompt_step(lam_ref, g_ref, q_ref, k_ref, v_ref, o_ref, m_ref, acc_ref, *, tq, tk, lam_init,
                      before_diagonal=None):
    qi = pl.program_id(2)
    q = q_ref[0, 0]
    lane = lax.broadcasted_iota(jnp.int32, (1, HEAD_W), 1)
    zero = jnp.zeros_like(q)
    qs = jnp.concatenate([jnp.where(lane < A_DH, q, zero), jnp.where(lane < A_DH, zero, q)], axis=0)
    rg = ATTN_ROW_GROUP
    ng = 2 * tq // rg

    def block(k0, masked):
        geo = []
        for g in range(ng):
            r0 = (g * rg) % tq
            width = min(tk, r0 + rg) if masked else tk
            geo.append((r0, slice(g * rg, (g + 1) * rg), width, pl.ds(pl.multiple_of(k0, tk), width)))

        def scores(g):
            _, gr, _, rows = geo[g]
            return lax.dot_general(qs[gr], k_ref[0, 0, rows, :], NT_DIMS, preferred_element_type=F32)

        ss = [scores(g) for g in range(ATTN_AHEAD)]
        for g, (r0, gr, width, rows) in enumerate(geo):
            s = ss[g]
            if g + ATTN_AHEAD < ng:
                ss.append(scores(g + ATTN_AHEAD))
            if masked:
                qpos = r0 + lax.broadcasted_iota(jnp.int32, (rg, width), 0)
                kpos = lax.broadcasted_iota(jnp.int32, (rg, width), 1)
                s = jnp.where(kpos <= qpos, s, NEG_BIG)
            m_old = m_ref[gr, :]
            m_new = jnp.maximum(m_old, jnp.max(s, axis=-1, keepdims=True))
            alpha = jnp.exp2(m_old - m_new)
            p = jnp.exp2(s - jnp.tile(m_new, (1, width // HEAD_W)))
            pv = jnp.dot(p.astype(BF16), v_ref[0, 0, rows, :], preferred_element_type=F32)
            m_ref[gr, :] = m_new
            acc_ref[gr, :] = jnp.tile(alpha, (1, 2)) * acc_ref[gr, :] + pv

    m_ref[...] = jnp.full(m_ref.shape, NEG_BIG, F32)
    acc_ref[...] = jnp.zeros(acc_ref.shape, F32)

    def two_blocks(i, carry):
        block(2 * i * tk, False)
        block((2 * i + 1) * tk, False)
        return carry

    lax.fori_loop(0, qi // 2, two_blocks, 0)

    @pl.when(qi % 2 == 1)
    def _():
        block((qi - 1) * tk, False)

    if before_diagonal is not None:
        before_diagonal()
    block(qi * tq, True)
    acc = acc_ref[...]
    o = acc[:, 0:HEAD_W] / acc[:, HEAD_W:2 * HEAD_W]
    lam = _lambda(lam_ref, lam_init)
    od = o[0:tq] - lam * o[tq:2 * tq]
    o_ref[0, 0] = _rms(od, g_ref[...]).astype(BF16)


def _mix_kernel(x_ref, gm_ref, oa_ref, ob_ref, ga_ref, gb_ref, wa_ref, wb_ref, wo_ref, o_ref):
    oa = jnp.concatenate([oa_ref[0, h] for h in range(N_HEADS)], axis=-1)
    ya = jnp.dot(oa, wa_ref[...], preferred_element_type=F32)
    yb = jnp.dot(ob_ref[0].astype(BF16), wb_ref[...], preferred_element_type=F32)
    merged = ga_ref[0] * ya + gb_ref[0] * yb
    mix = jnp.dot(merged.astype(BF16), wo_ref[...], preferred_element_type=F32)
    o_ref[0] = x_ref[0] + gm_ref[0] * mix


def _mix(x, gm, oa_hm, ob, ga, gb, wa, wb, wo, tm):
    b, t, _ = x.shape
    rows_mod = gm.shape[1]
    tok_spec = pl.BlockSpec((1, tm, D_MODEL), lambda i, j: (i, j, 0))
    mod_spec = (pl.BlockSpec((1, 1, D_MODEL), lambda i, j: (i, 0, 0)) if rows_mod == 1 else tok_spec)
    return pl.pallas_call(
        _mix_kernel,
        out_shape=jax.ShapeDtypeStruct((b, t, D_MODEL), F32),
        grid=(b, t // tm),
        in_specs=[tok_spec, mod_spec, pl.BlockSpec((1, N_HEADS, tm, HEAD_W), lambda i, j: (i, 0, j, 0)),
                  tok_spec, tok_spec, tok_spec,
                  _const_spec(wa.shape), _const_spec(wb.shape), _const_spec(wo.shape)],
        out_specs=tok_spec,
        compiler_params=_params(("parallel", "parallel")),
        name="mix_out",
    )(x, gm, oa_hm, ob, ga, gb, wa, wb, wo)


def _ffn_kernel(*refs, tm, carried):
    if carried:
        (x_ref, sh_ref, sc_ref, gt_ref, g_ref, wup_ref, cw_ref, cb_ref, wdn_ref, y_ref, tail_ref, prev_ref) = refs
    else:
        (x_ref, sh_ref, sc_ref, gt_ref, g_ref, wup_ref, cw_ref, cb_ref, wdn_ref, b0_ref, b1_ref, y_ref, u_ref) = refs
    x = x_ref[0]
    hn = (_rms(x, g_ref[...]) * (1.0 + sc_ref[0]) + sh_ref[0]).astype(BF16)
    if carried:
        @pl.when(pl.program_id(1) == 0)
        def _():
            prev_ref[...] = jnp.zeros_like(prev_ref)
        row = lax.broadcasted_iota(jnp.int32, (tm, 1), 0)

    def conv(u, cols):
        if carried:
            p6 = prev_ref[6:7, cols]
            p7 = prev_ref[7:8, cols]
            u1 = jnp.where(row == 0, p7, pltpu.roll(u, 1, 0))
            u2 = jnp.where(row == 0, p6, jnp.where(row == 1, p7, pltpu.roll(u, 2, 0)))
            prev_ref[:, cols] = u[tm - 8:tm, :]
            tail_ref[0, :, cols] = u[tm - 8:tm, :]
        else:
            u2 = b0_ref[0, :, cols]
            u1 = b1_ref[0, :, cols]
            u_ref[0, :, cols] = u
        return cb_ref[:, cols] + cw_ref[0:1, cols] * u2 + cw_ref[1:2, cols] * u1 + cw_ref[2:3, cols] * u

    nchunk = D_FF // FFN_COLS

    def cols(c):
        return slice(c * FFN_COLS, (c + 1) * FFN_COLS), slice(D_FF + c * FFN_COLS, D_FF + (c + 1) * FFN_COLS)

    def up(c):
        ca, cg = cols(c)
        return (jnp.dot(hn, wup_ref[:, ca], preferred_element_type=F32),
                jnp.dot(hn, wup_ref[:, cg], preferred_element_type=F32))

    acc = jnp.zeros((tm, D_MODEL), F32)
    ups = [up(c) for c in range(FFN_AHEAD)]
    acts = []
    for c in range(nchunk):
        ua, ug = ups[c]
        if c + FFN_AHEAD < nchunk:
            ups.append(up(c + FFN_AHEAD))
        ca, cg = cols(c)
        acts.append((jax.nn.silu(conv(ua, ca)) * conv(ug, cg)).astype(BF16))
        if c >= FFN_BEHIND:
            acc = acc + jnp.dot(acts[c - FFN_BEHIND], wdn_ref[cols(c - FFN_BEHIND)[0], :],
                                preferred_element_type=F32)
    for c in range(nchunk - FFN_BEHIND, nchunk):
        acc = acc + jnp.dot(acts[c], wdn_ref[cols(c)[0], :], preferred_element_type=F32)
    y_ref[0] = x + gt_ref[0] * acc


def _ffn(x1, sh, sc, gt, g, wup, cw, cb, wdn, tm, bufs=None):
    b, t, _ = x1.shape
    carried = bufs is None
    tok_spec = pl.BlockSpec((1, tm, D_MODEL), lambda i, j: (i, j, 0))
    mod_spec = (pl.BlockSpec((1, 1, D_MODEL), lambda i, j: (i, 0, 0)) if sh.shape[1] == 1 else tok_spec)
    wide_spec = pl.BlockSpec((1, tm, 2 * D_FF), lambda i, j: (i, j, 0))
    in_specs = [tok_spec, mod_spec, mod_spec, mod_spec, _const_spec((1, D_MODEL)), _const_spec(wup.shape),
                _const_spec(cw.shape), _const_spec(cb.shape), _const_spec(wdn.shape)]
    args = [x1, sh, sc, gt, g, wup, cw, cb, wdn]
    if carried:
        out_shape = (jax.ShapeDtypeStruct((b, t, D_MODEL), F32), jax.ShapeDtypeStruct((b, 8, 2 * D_FF), F32))
        out_specs = (tok_spec, pl.BlockSpec((1, 8, 2 * D_FF), lambda i, j: (i, 0, 0)))
        scratch = [pltpu.VMEM((8, 2 * D_FF), F32)]
        sem = ("parallel", "arbitrary")
    else:
        in_specs += [wide_spec, wide_spec]
        args += list(bufs)
        out_shape = (jax.ShapeDtypeStruct((b, t, D_MODEL), F32), jax.ShapeDtypeStruct((b, t, 2 * D_FF), F32))
        out_specs = (tok_spec, wide_spec)
        scratch = []
        sem = ("parallel", "parallel")
    return pl.pallas_call(
        functools.partial(_ffn_kernel, tm=tm, carried=carried),
        out_shape=out_shape,
        grid=(b, t // tm),
        in_specs=in_specs,
        out_specs=out_specs,
        scratch_shapes=scratch,
        compiler_params=_params(sem),
        name="conv_ffn",
    )(*args)


def _hgrn_sample_kernel(ft_ref, q_ref, v_ref, og_ref, gn_ref, s_ref, so_ref, o_ref, *, nb):
    ft = ft_ref[...]
    first_row = lax.broadcasted_iota(jnp.int32, (2 * N_HEADS, HEAD_W), 0) == 0
    for b in range(nb):
        f = ft[:, b:b + 1]
        s_new = f * s_ref[b, 0] + (1.0 - f) * v_ref[b:b + 1, :]
        so_ref[b, 0] = s_new
        lhs = jnp.where(first_row, q_ref[b:b + 1, :], 0.0).astype(BF16)
        o_ref[b:b + 1, :] = jnp.dot(lhs, s_new.astype(BF16), preferred_element_type=F32)[0:1, :]
    o_ref[...] = _rms(o_ref[...], gn_ref[...]) * og_ref[...]


def _hgrn_sample(ft, qh, vh, og, gn, state):
    nb = vh.shape[0]
    st_spec = pl.BlockSpec((nb, 1, HEAD_W, HEAD_W), lambda h: (0, h, 0, 0))
    hd_spec = pl.BlockSpec((nb, HEAD_W), lambda h: (0, h))
    return pl.pallas_call(
        functools.partial(_hgrn_sample_kernel, nb=nb),
        out_shape=(jax.ShapeDtypeStruct(state.shape, F32), jax.ShapeDtypeStruct((nb, D_MODEL), F32)),
        grid=(N_HEADS,),
        in_specs=[pl.BlockSpec((HEAD_W, nb), lambda h: (h, 0)), hd_spec, hd_spec, hd_spec, _const_spec((1, HEAD_W)),
                  st_spec],
        out_specs=(st_spec, hd_spec),
        compiler_params=_params(("parallel",)),
        name="hgrn_sample",
    )(ft, qh, vh, og, gn, state)


def _page_copies(pt_ref, ck_hbm, cv_hbm, kbuf, vbuf, sem, b, n_pages):
    sl = b % 2
    out = []
    for p in range(n_pages):
        pg = pt_ref[b, p]
        out.append(pltpu.make_async_copy(ck_hbm.at[pg], kbuf.at[sl, p], sem.at[0, sl]))
        out.append(pltpu.make_async_copy(cv_hbm.at[pg], vbuf.at[sl, p], sem.at[1, sl]))
    return out


def _attn_sample_scores(b, q_ref, kbuf, *, n_pages):
    slot = b % 2
    q = q_ref[pl.ds(b, 1), :]
    r16 = lax.broadcasted_iota(jnp.int32, (2 * N_HEADS, D_MODEL), 0)
    ln = lax.broadcasted_iota(jnp.int32, (2 * N_HEADS, D_MODEL), 1)
    sel = ((ln // HEAD_W) == (r16 % N_HEADS)) & (((ln // A_DH) % 2) == (r16 // N_HEADS))
    qsel = jnp.where(sel, q, 0.0)
    qblk = qsel.astype(BF16)
    s = jnp.concatenate([jnp.dot(qblk, kbuf[slot, p].astype(BF16), preferred_element_type=F32)
                         for p in range(n_pages)], axis=1)
    return qsel, s


def _attn_sample_one(b, qsel, s, lam_ref, g_ref, kn_ref, vn_ref, o_ref, vbuf, *, n_pages, page, lam_init):
    slot = b % 2
    s_new = jnp.sum(qsel * kn_ref[pl.ds(b, 1), :], axis=-1, keepdims=True)
    m = jnp.maximum(jnp.max(s, axis=-1, keepdims=True), s_new)
    e = jnp.exp2(s - m)
    e_new = jnp.exp2(s_new - m)
    inv_l = 1.0 / (jnp.sum(e, axis=-1, keepdims=True) + e_new)
    lam = _lambda(lam_ref, lam_init)
    pr = e * inv_l
    pr_new = e_new * inv_l
    pd = pr[0:N_HEADS] - lam * pr[N_HEADS:2 * N_HEADS]
    pd_new = pr_new[0:N_HEADS] - lam * pr_new[N_HEADS:2 * N_HEADS]

    per = HEAD_W // N_HEADS
    lane = lax.broadcasted_iota(jnp.int32, (N_HEADS, HEAD_W), 1)
    own = (lane % N_HEADS) == lax.broadcasted_iota(jnp.int32, (N_HEADS, HEAD_W), 0)
    pad = jnp.zeros((N_HEADS, HEAD_W), F32)

    acc = jnp.zeros((2 * N_HEADS, HEAD_W), F32)
    for p in range(n_pages):
        w = pd[:, p * page:(p + 1) * page]
        for g in range(page // per):
            wx = jnp.take_along_axis(w, g * per + lane // N_HEADS, axis=1)
            lhs = jnp.concatenate([jnp.where(own, wx, 0.0), pad], axis=0).astype(BF16)
            rhs = vbuf[slot, p, pl.ds(g * per, per)].reshape(HEAD_W, HEAD_W).astype(BF16)
            acc = acc + jnp.dot(lhs, rhs, preferred_element_type=F32)
    o_ref[b] = _rms(acc[0:N_HEADS] + pd_new * vn_ref[b], g_ref[...])


def _attn_kernel(pt_ref, lam_ref, g_ref, q_ref, k_ref, v_ref, qs_ref, kn_ref, vn_ref, ck_hbm, cv_hbm,
                 o_ref, os_ref, kbuf, vbuf, m_ref, acc_ref, sem, *, tq, lam_init, n_pages, page, per_step):
    step = (pl.program_id(0) * pl.num_programs(1) + pl.program_id(1)) * pl.num_programs(2) + pl.program_id(2)
    n_steps = pl.num_programs(0) * pl.num_programs(1) * pl.num_programs(2)
    first = step * per_step
    copies = functools.partial(_page_copies, pt_ref, ck_hbm, cv_hbm, kbuf, vbuf, sem, n_pages=n_pages)

    @pl.when(step == 0)
    def _():
        for cp in copies(0):
            cp.start()

    if per_step == 1:
        @pl.when(step + 1 < n_steps)
        def _():
            for cp in copies(step + 1):
                cp.start()

    early = []

    def wait_first():
        for cp in copies(first):
            cp.wait()
        early.append(_attn_sample_scores(first, qs_ref, kbuf, n_pages=n_pages))

    _attn_prompt_step(lam_ref, g_ref, q_ref, k_ref, v_ref, o_ref, m_ref, acc_ref, tq=tq, tk=tq, lam_init=lam_init,
                      before_diagonal=wait_first)

    for j in range(per_step):
        b = first + j
        if per_step > 1:
            @pl.when(b + 1 < n_steps * per_step)
            def _():
                for cp in copies(b + 1):
                    cp.start()
        if j > 0:
            for cp in copies(b):
                cp.wait()
        qsel, s = early[0] if j == 0 else _attn_sample_scores(b, qs_ref, kbuf, n_pages=n_pages)
        _attn_sample_one(b, qsel, s, lam_ref, g_ref, kn_ref, vn_ref, os_ref, vbuf,
                         n_pages=n_pages, page=page, lam_init=lam_init)


def _attention(page_table, lam4, g_sub, q_hm, k_hm, v_hm, q_s, k_new, v_new, cache_kt, cache_v, lam_init, tq=1024):
    b, nh, t, _ = q_hm.shape
    nb, n_pages = page_table.shape
    page = cache_v.shape[1]
    n_steps = b * nh * (t // tq)
    assert nb % n_steps == 0, (nb, n_steps)
    full = lambda shape: pl.BlockSpec(shape, lambda i, h, j, pt: (0,) * len(shape))
    any_spec = pl.BlockSpec(memory_space=pl.ANY)
    k_spec = pl.BlockSpec((1, 1, t, HEAD_W), lambda i, h, j, pt: (i, h, 0, 0))
    v_spec = pl.BlockSpec((1, 1, t, 2 * HEAD_W), lambda i, h, j, pt: (i, h, 0, 0))
    q_spec = pl.BlockSpec((1, 1, tq, HEAD_W), lambda i, h, j, pt: (i, h, j, 0))
    grid_spec = pltpu.PrefetchScalarGridSpec(
        num_scalar_prefetch=1,
        grid=(b, nh, t // tq),
        in_specs=[full(lam4.shape), full((1, HEAD_W)), q_spec, k_spec, v_spec,
                  full(q_s.shape), full(k_new.shape), full(v_new.shape), any_spec, any_spec],
        out_specs=(q_spec, full((nb, N_HEADS, HEAD_W))),
        scratch_shapes=[pltpu.VMEM((2, n_pages, D_MODEL, page), F32),
                        pltpu.VMEM((2, n_pages, page, N_HEADS, HEAD_W), F32),
                        pltpu.VMEM((2 * tq, HEAD_W), F32), pltpu.VMEM((2 * tq, 2 * HEAD_W), F32),
                        pltpu.SemaphoreType.DMA((2, 2))],
    )
    return pl.pallas_call(
        functools.partial(_attn_kernel, tq=tq, lam_init=lam_init, n_pages=n_pages, page=page,
                          per_step=nb // n_steps),
        out_shape=(jax.ShapeDtypeStruct((b, nh, t, HEAD_W), BF16), jax.ShapeDtypeStruct((nb, N_HEADS, HEAD_W), F32)),
        grid_spec=grid_spec,
        compiler_params=_params(("arbitrary", "arbitrary", "arbitrary")),
        name="attention",
    )(page_table, lam4, g_sub, q_hm, k_hm, v_hm, q_s, k_new, v_new, cache_kt, cache_v)


def _rope_tables(pos):
    half = A_DH // 2
    inv = ROPE_THETA ** (-np.arange(half, dtype=np.float64) / half)
    ang = np.asarray(pos, np.float64)[:, None] * inv[None, :]
    cos = np.cos(ang)
    sin = np.sin(ang)
    return (jnp.asarray(np.tile(cos, (1, 4)), F32),
            jnp.asarray(np.concatenate([-sin, sin, -sin, sin], axis=-1), F32))


def kernel(x_prompt, x_sample, cache_k, cache_v, state_hgrn, state_conv, page_table, c_prompt, c_sample, w_ada, b_ada,
           rms_mix_g, w_in, q_norm_g, k_norm_g, lambda_q1, lambda_k1, lambda_q2, lambda_k2, subln_g, lb_logits,
           hgrn_norm_g, w_branch_a, w_branch_b, w_out, rms_ffn_g, w_up, conv_w, conv_b, w_down):
    bp, tp, d = x_prompt.shape
    bs, ts, _ = x_sample.shape
    assert d == D_MODEL and ts == 1 and w_in.shape[0] == 1
    n_pool, page = cache_k.shape[1], cache_k.shape[2]
    past_len = page_table.shape[1] * page
    lam_init = 0.8 - 0.6 * math.exp(-0.3 * 0)

    pad = (-(bp + bs)) % 8
    c_all = jnp.concatenate([c_prompt, c_sample, jnp.zeros((pad, d), F32)], axis=0)
    mod = _ada(c_all, w_ada[0], b_ada[0].reshape(1, -1))
    mp = [mod[:bp, i * d:(i + 1) * d].reshape(bp, 1, d) for i in range(6)]
    msamp = [mod[bp:bp + bs, i * d:(i + 1) * d] for i in range(6)]

    w_in_b = w_in[0].astype(BF16)
    wa_b = w_branch_a[0].astype(BF16)
    wb_b = w_branch_b[0].astype(BF16)
    wo_b = w_out[0].astype(BF16)
    wup_b = w_up[0].astype(BF16)
    wdn_b = w_down[0].astype(BF16)
    gmix = rms_mix_g[0].reshape(1, d)
    gffn = rms_ffn_g[0].reshape(1, d)
    qg = jnp.tile(q_norm_g[0], 2).reshape(1, HEAD_W)
    kg = jnp.tile(k_norm_g[0], 2).reshape(1, HEAD_W)
    lam4 = jnp.stack([lambda_q1[0], lambda_k1[0], lambda_q2[0], lambda_k2[0]], axis=0)
    g_sub = (subln_g[0] * (1.0 - lam_init)).reshape(1, HEAD_W)
    gn = hgrn_norm_g[0].reshape(1, HEAD_W)
    cw = conv_w[0]
    cb = conv_b[0].reshape(1, -1)

    cos_p, sin_p = _rope_tables(np.arange(tp))
    (q_hm, k_tok, k_hm, v_tok, v_hm, qh, kh, gh, vh, og, ga, gb) = _inproj_prompt(
        x_prompt, mp[0], mp[1], gmix, w_in_b, qg, kg, cos_p, sin_p, lb_logits)
    ob, s_prompt = _hgrn_prompt(qh, kh, gh, vh, og, gn)
    cos_s, sin_s = _rope_tables(np.full((1,), past_len))
    xs = x_sample.reshape(bs, d)
    (q_s, k_s, v_s, qh_s, ft_s, vh_s, og_s, ga_s, gb_s) = _inproj_sample(
        xs, msamp[0], msamp[1], gmix, w_in_b, qg, kg, cos_s, sin_s, lb_logits)
    s_sample, ob_s = _hgrn_sample(ft_s, qh_s, vh_s, og_s, gn, state_hgrn[0])

    cache_kt = jnp.transpose(cache_k[0], (0, 2, 3, 4, 1)).reshape(n_pool, d, page)
    oa_hm, oa_s = _attention(page_table, lam4, g_sub, q_hm, k_hm, v_hm, q_s, k_s, v_s.reshape(bs, N_HEADS, HEAD_W),
                             cache_kt, cache_v[0], lam_init)

    x1 = _mix(x_prompt, mp[2], oa_hm, ob, ga, gb, wa_b, wb_b, wo_b, tm=1024)
    y_prompt, tail = _ffn(x1, mp[3], mp[4], mp[5], gffn, wup_b, cw, cb, wdn_b, tm=512)
    oa_s = jnp.transpose(oa_s, (1, 0, 2))
    x1_s = _mix(xs[None], msamp[2][None], oa_s[None].astype(BF16), ob_s[None], ga_s[None], gb_s[None],
                wa_b, wb_b, wo_b, tm=bs)
    y_s, u_s = _ffn(x1_s, msamp[3][None], msamp[4][None], msamp[5][None], gffn, wup_b, cw, cb, wdn_b, tm=bs,
                    bufs=(state_conv[0][:, 0][None], state_conv[0][:, 1][None]))

    new_conv_s = jnp.stack([state_conv[0][:, 1], u_s[0]], axis=1)
    return (y_prompt, y_s.reshape(bs, 1, d),
            k_tok.reshape(1, bp, tp, N_HEADS, 2, A_DH), v_tok.reshape(1, bp, tp, N_HEADS, HEAD_W),
            s_prompt[None], tail[:, 6:8][None],
            k_s.reshape(1, bs, 1, N_HEADS, 2, A_DH), v_s.reshape(1, bs, 1, N_HEADS, HEAD_W),
            s_sample[None], new_conv_s[None])
```
